```python
import math
import jax, jax.numpy as jnp
from jax import lax
import numpy as np

D_MODEL = 1024
BATCH = 8
SEQ = 2048
DEPTH = 2

HEAD_DIM = 64
N_A_LAYERS = DEPTH // 2
N_B_LAYERS = DEPTH - N_A_LAYERS
FOX_HEADS = D_MODEL // HEAD_DIM
SWA_Q_HEADS = D_MODEL // HEAD_DIM
SWA_KV_HEADS = SWA_Q_HEADS // 8
SWA_GROUP = SWA_Q_HEADS // SWA_KV_HEADS
WINDOW = 128
Q_BLOCK = 128
D_FF = 4 * D_MODEL
N_BUCKETS = 32
REL_MAX_DIST = 128
NORM_EPS = 1e-6

kernel_name = "yoco_fox_swa_sink_hybrid"


def _rmsnorm(x, g):
    xf = x.astype(jnp.float32)
    y = xf * lax.rsqrt(jnp.mean(xf * xf, axis=-1, keepdims=True) + NORM_EPS)
    return (y * g.astype(jnp.float32)).astype(x.dtype)


def _t5_causal_bucket(dist):
    n = np.maximum(dist, 0)
    max_exact = N_BUCKETS // 2
    large = max_exact + (np.log(np.maximum(n, 1) / max_exact)
                         / np.log(REL_MAX_DIST / max_exact)
                         * (N_BUCKETS - max_exact)).astype(np.int32)
    large = np.minimum(large, N_BUCKETS - 1)
    return np.where(n < max_exact, n, large).astype(np.int32)


def _sq_relu_mlp(h, w_up, w_down):
    u = h @ w_up
    return (jnp.square(jax.nn.relu(u))) @ w_down


def _fox_attention(h, w_in, b_f, g_q, g_k, w_out):
    bsz, seq, _ = h.shape
    hw = FOX_HEADS * HEAD_DIM
    proj = h @ w_in
    q = _rmsnorm(proj[..., :hw].reshape(bsz, seq, FOX_HEADS, HEAD_DIM), g_q)
    k = _rmsnorm(proj[..., hw:2 * hw].reshape(bsz, seq, FOX_HEADS, HEAD_DIM), g_k)
    v = proj[..., 2 * hw:3 * hw].reshape(bsz, seq, FOX_HEADS, HEAD_DIM)
    log_f = jax.nn.log_sigmoid(proj[..., 3 * hw:].astype(jnp.float32)
                               + b_f.astype(jnp.float32))
    c = jnp.cumsum(log_f, axis=1).transpose(0, 2, 1)
    scale = HEAD_DIM ** -0.5
    outs = []
    for blk in range(seq // Q_BLOCK):
        t0, t1 = blk * Q_BLOCK, (blk + 1) * Q_BLOCK
        s = jnp.einsum('bqhd,bkhd->bhqk', q[:, t0:t1], k[:, :t1]).astype(jnp.float32) * scale
        s = s + c[:, :, t0:t1, None] - c[:, :, None, :t1]
        mask = np.arange(t1)[None, :] <= np.arange(t0, t1)[:, None]
        s = jnp.where(mask, s, -jnp.inf)
        p = jax.nn.softmax(s, axis=-1).astype(v.dtype)
        outs.append(jnp.einsum('bhqk,bkhd->bqhd', p, v[:, :t1]))
    o = jnp.concatenate(outs, axis=1).reshape(bsz, seq, hw)
    return o @ w_out


def _shared_kv(h, g_kv, w_kv, g_k):
    bsz, seq, _ = h.shape
    kvw = SWA_KV_HEADS * HEAD_DIM
    kv = _rmsnorm(h, g_kv) @ w_kv
    k = _rmsnorm(kv[..., :kvw].reshape(bsz, seq, SWA_KV_HEADS, HEAD_DIM), g_k)
    v = kv[..., kvw:].reshape(bsz, seq, SWA_KV_HEADS, HEAD_DIM)
    return k, v


def _band(x):
    bsz, seq = x.shape[:2]
    xb = x.reshape(bsz, seq // WINDOW, WINDOW, *x.shape[2:])
    prev = jnp.pad(xb[:, :-1], ((0, 0), (1, 0), (0, 0), (0, 0), (0, 0)))
    return jnp.concatenate([prev, xb], axis=2)


def _swa_sink_attention(h, w_q, g_q, k_band, v_band, sinks, rel_bias, w_out):
    bsz, seq, _ = h.shape
    nblk = seq // WINDOW
    q = _rmsnorm((h @ w_q).reshape(bsz, seq, SWA_Q_HEADS, HEAD_DIM), g_q)
    qb = q.reshape(bsz, nblk, WINDOW, SWA_KV_HEADS, SWA_GROUP, HEAD_DIM)
    s = jnp.einsum('bnqkgd,bnjkd->bnkgqj', qb, k_band).astype(jnp.float32) * (HEAD_DIM ** -0.5)
    dist = np.arange(WINDOW)[:, None] + WINDOW - np.arange(2 * WINDOW)[None, :]
    bias = rel_bias.astype(jnp.float32)[_t5_causal_bucket(dist)]
    bias = bias.transpose(2, 0, 1).reshape(SWA_KV_HEADS, SWA_GROUP, WINDOW, 2 * WINDOW)
    s = s + bias[None, None]
    valid = (dist >= 0) & (dist < WINDOW)
    blk_ok = (np.arange(nblk)[:, None] > 0) | (np.arange(2 * WINDOW)[None, :] >= WINDOW)
    mask = valid[None, :, :] & blk_ok[:, None, :]
    s = jnp.where(mask[None, :, None, None], s, -jnp.inf)
    sink = jnp.broadcast_to(
        sinks.astype(jnp.float32).reshape(1, 1, SWA_KV_HEADS, SWA_GROUP, 1, 1),
        s.shape[:-1] + (1,))
    p = jax.nn.softmax(jnp.concatenate([s, sink], axis=-1), axis=-1)[..., :-1]
    o = jnp.einsum('bnkgqj,bnjkd->bnqkgd', p.astype(v_band.dtype), v_band)
    return o.reshape(bsz, seq, SWA_Q_HEADS * HEAD_DIM) @ w_out


def setup_inputs(seed: int = 0) -> dict:
    key = jax.random.key(seed)
    ks = jax.random.split(key, 20)
    f32 = jnp.float32
    hw = FOX_HEADS * HEAD_DIM
    qw = SWA_Q_HEADS * HEAD_DIM
    kvw = SWA_KV_HEADS * HEAD_DIM

    def nrm(k, shape, fan_in):
        return jax.random.normal(k, shape, f32) * (fan_in ** -0.5)

    def gain(k, shape):
        return 1.0 + 0.05 * jax.random.normal(k, shape, f32)

    return {
        "x": jax.random.normal(ks[0], (BATCH, SEQ, D_MODEL), f32),
        "g_attn": gain(ks[1], (DEPTH, D_MODEL)),
        "g_mlp": gain(ks[2], (DEPTH, D_MODEL)),
        "w_in_a": nrm(ks[3], (N_A_LAYERS, D_MODEL, 3 * hw + FOX_HEADS), D_MODEL),
        "b_f": 3.0 + 0.5 * jax.random.normal(ks[4], (N_A_LAYERS, FOX_HEADS), f32),
        "gq_a": gain(ks[5], (N_A_LAYERS, HEAD_DIM)),
        "gk_a": gain(ks[6], (N_A_LAYERS, HEAD_DIM)),
        "w_out_a": nrm(ks[7], (N_A_LAYERS, hw, D_MODEL), hw),
        "g_kv": gain(ks[8], (D_MODEL,)),
        "w_kv": nrm(ks[9], (D_MODEL, 2 * kvw), D_MODEL),
        "gk_b": gain(ks[10], (HEAD_DIM,)),
        "w_q_b": nrm(ks[11], (N_B_LAYERS, D_MODEL, qw), D_MODEL),
        "gq_b": gain(ks[12], (N_B_LAYERS, HEAD_DIM)),
        "sinks": 0.5 * jax.random.normal(ks[13], (N_B_LAYERS, SWA_Q_HEADS), f32),
        "rel_bias": 0.2 * jax.random.normal(ks[14], (N_BUCKETS, SWA_Q_HEADS), f32),
        "w_out_b": nrm(ks[15], (N_B_LAYERS, qw, D_MODEL), qw),
        "w_up": nrm(ks[16], (DEPTH, D_MODEL, D_FF), D_MODEL),
        "w_down": nrm(ks[17], (DEPTH, D_FF, D_MODEL), D_FF),
    }


def reference(x, g_attn, g_mlp, w_in_a, b_f, gq_a, gk_a, w_out_a, g_kv, w_kv, gk_b,
              w_q_b, gq_b, sinks, rel_bias, w_out_b, w_up, w_down):
    h = x
    k_band = None
    v_band = None
    for layer in range(DEPTH):
        if layer < N_A_LAYERS:
            a = layer
            h = h + _fox_attention(_rmsnorm(h, g_attn[layer]), w_in_a[a], b_f[a],
                                   gq_a[a], gk_a[a], w_out_a[a])
        else:
            b = layer - N_A_LAYERS
            if b == 0:
                k_sh, v_sh = _shared_kv(h, g_kv, w_kv, gk_b)
                k_band, v_band = _band(k_sh), _band(v_sh)
            h = h + _swa_sink_attention(_rmsnorm(h, g_attn[layer]), w_q_b[b], gq_b[b],
                                        k_band, v_band, sinks[b], rel_bias, w_out_b[b])
        h = h + _sq_relu_mlp(_rmsnorm(h, g_mlp[layer]), w_up[layer], w_down[layer])
    return h
```

```python
import functools

import numpy as np
import jax
import jax.numpy as jnp
from jax import lax
from jax.experimental import pallas as pl
from jax.experimental.pallas import tpu as pltpu

D_MODEL = 1024
HEAD_DIM = 64
N_HEADS = 16
N_PAIRS = N_HEADS // 2
KV_HEADS = 2
WINDOW = 128
D_FF = 4 * D_MODEL
N_BUCKETS = 32
REL_MAX_DIST = 128
NORM_EPS = 1e-6
LANES = 128
MXU_N = 256
QK_SCALE = HEAD_DIM ** -0.5

F32 = jnp.float32
BF16 = jnp.bfloat16
NEG_INF = float("-inf")
VMEM_LIMIT = 56 * 1024 * 1024


def _rms_scale(x):
    return lax.rsqrt(jnp.mean(x * x, axis=-1, keepdims=True) + NORM_EPS)


def _head_pair_norm(y, g2):
    lo = lax.broadcasted_iota(jnp.int32, (1, LANES), 1) < HEAD_DIM
    sq = y * y
    s_lo = jnp.sum(jnp.where(lo, sq, 0.0), axis=-1, keepdims=True)
    s_hi = jnp.sum(jnp.where(lo, 0.0, sq), axis=-1, keepdims=True)
    ms = jnp.where(lo, s_lo, s_hi) * (1.0 / HEAD_DIM)
    return (y * lax.rsqrt(ms + NORM_EPS)) * g2


def _resident(shape):
    zeros = (0,) * len(shape)
    return pl.BlockSpec(shape, lambda *_: zeros, pipeline_mode=pl.Buffered(1))


def _fox_pre_kernel(x_ref, g_ref, wqkv_ref, wf_ref, gq_ref, gk_ref,
                    q_ref, k_ref, v_ref, f_ref):
    x = x_ref[...]
    hn = ((x * _rms_scale(x)) * g_ref[...]).astype(BF16)
    n_chunks = D_MODEL // MXU_N
    for c in range(3 * n_chunks):
        y = jnp.dot(hn, wqkv_ref[:, c * MXU_N:(c + 1) * MXU_N],
                    preferred_element_type=F32)
        part, cc = divmod(c, n_chunks)
        for half in range(MXU_N // LANES):
            yy = y[:, half * LANES:(half + 1) * LANES]
            dst = pl.ds(cc * MXU_N + half * LANES, LANES)
            if part == 0:
                q_ref[:, dst] = (_head_pair_norm(yy, gq_ref[...]) * QK_SCALE).astype(BF16)
            elif part == 1:
                k_ref[:, dst] = _head_pair_norm(yy, gk_ref[...]).astype(BF16)
            else:
                v_ref[:, dst] = yy.astype(BF16)
    f = jnp.dot(hn, wf_ref[...], preferred_element_type=F32)
    f_ref[...] = f[:, :N_HEADS]


def _fox_pre(x2, g, wqkv, wf, gq2, gk2, tm):
    t = x2.shape[0]
    tok = lambda w: pl.BlockSpec((tm, w), lambda i: (i, 0))
    return pl.pallas_call(
        _fox_pre_kernel,
        grid=(t // tm,),
        in_specs=[tok(D_MODEL), _resident((1, D_MODEL)),
                  _resident((D_MODEL, 3 * D_MODEL)), _resident((D_MODEL, LANES)),
                  _resident((1, LANES)), _resident((1, LANES))],
        out_specs=[tok(D_MODEL), tok(D_MODEL), tok(D_MODEL), tok(N_HEADS)],
        out_shape=[jax.ShapeDtypeStruct((t, D_MODEL), BF16)] * 3
        + [jax.ShapeDtypeStruct((t, N_HEADS), F32)],
        compiler_params=pltpu.CompilerParams(
            dimension_semantics=("parallel",), vmem_limit_bytes=VMEM_LIMIT),
        name="fox_pre",
    )(x2, g, wqkv, wf, gq2, gk2)


def _fox_decay_kernel(f_ref, b_ref, c_ref, *, blk):
    x = f_ref[0] + b_ref[...]
    log_f = -(jnp.maximum(-x, 0.0) + jnp.log1p(jnp.exp(-jnp.abs(x))))
    r = lax.broadcasted_iota(jnp.int32, (blk, blk), 0)
    c = lax.broadcasted_iota(jnp.int32, (blk, blk), 1)
    upper = (r <= c).astype(F32)
    carry = jnp.zeros((N_HEADS, 1), F32)
    for b in range(x.shape[1] // blk):
        cb = jnp.dot(log_f[:, b * blk:(b + 1) * blk], upper,
                     precision=lax.Precision.HIGHEST,
                     preferred_element_type=F32) + carry
        c_ref[0, :, b * blk:(b + 1) * blk] = cb
        carry = cb[:, blk - 1:blk]


def _fox_decay(f_rows, b_col):
    bsz, _, seq = f_rows.shape
    spec = pl.BlockSpec((1, N_HEADS, seq), lambda b: (b, 0, 0))
    return pl.pallas_call(
        functools.partial(_fox_decay_kernel, blk=MXU_N),
        grid=(bsz,),
        in_specs=[spec, _resident((N_HEADS, 1))],
        out_specs=spec,
        out_shape=jax.ShapeDtypeStruct(f_rows.shape, F32),
        compiler_params=pltpu.CompilerParams(dimension_semantics=("parallel",)),
        name="fox_decay",
    )(f_rows, b_col)


def _fox_attn_kernel(q_ref, k_ref, v_ref, c_ref, o_ref, *, tq):
    i = pl.program_id(2)
    q2 = q_ref[...]
    lane = lax.broadcasted_iota(jnp.int32, (1, LANES), 1)
    row = lax.broadcasted_iota(jnp.int32, (tq, tq), 0)
    col = lax.broadcasted_iota(jnp.int32, (tq, tq), 1)
    causal = col <= row
    t0 = pl.multiple_of(i * tq, tq)
    outs = []
    for hh in range(2):
        sel = (lane < HEAD_DIM) if hh == 0 else (lane >= HEAD_DIM)
        qm = jnp.where(sel, q2, jnp.zeros_like(q2))
        c0 = c_ref[hh, :, pl.ds(t0, LANES)][:, 0:1]

        def step(j, carry, masked, hh=hh, qm=qm, c0=c0):
            m, l, acc = carry
            ks = pl.multiple_of(j * tq, tq)
            k = k_ref[pl.ds(ks, tq), :]
            v = v_ref[pl.ds(ks, tq), :]
            s = lax.dot_general(qm, k, (((1,), (1,)), ((), ())),
                                preferred_element_type=F32)
            s = s - (c_ref[hh, :, pl.ds(ks, tq)] - c0)
            if masked:
                s = jnp.where(causal, s, NEG_INF)
            m_new = jnp.maximum(m, jnp.max(s, axis=-1, keepdims=True))
            alpha = jnp.exp(m - m_new)
            p = jnp.exp(s - m_new)
            l = alpha * l + jnp.sum(p, axis=-1, keepdims=True)
            acc = alpha * acc + jnp.dot(p.astype(BF16), v,
                                        preferred_element_type=F32)
            return m_new, l, acc

        init = (jnp.full((tq, 1), NEG_INF, F32), jnp.zeros((tq, 1), F32),
                jnp.zeros((tq, LANES), F32))
        carry = lax.fori_loop(0, i, functools.partial(step, masked=False), init)
        _, l, acc = step(i, carry, True)
        outs.append(acc / l)
    o_ref[...] = jnp.where(lane < HEAD_DIM, outs[0], outs[1]).astype(BF16)


def _fox_attn(q, k, v, c3, bsz, seq, tq):
    nq = seq // tq
    return pl.pallas_call(
        functools.partial(_fox_attn_kernel, tq=tq),
        grid=(bsz, N_PAIRS, nq),
        in_specs=[
            pl.BlockSpec((tq, LANES), lambda b, p, i: (b * nq + i, p)),
            pl.BlockSpec((seq, LANES), lambda b, p, i: (b, p)),
            pl.BlockSpec((seq, LANES), lambda b, p, i: (b, p)),
            pl.BlockSpec((2, 1, seq), lambda b, p, i: (b * N_PAIRS + p, 0, 0)),
        ],
        out_specs=pl.BlockSpec((tq, LANES), lambda b, p, i: (b * nq + i, p)),
        out_shape=jax.ShapeDtypeStruct(q.shape, BF16),
        compiler_params=pltpu.CompilerParams(
            dimension_semantics=("parallel", "parallel", "arbitrary")),
        name="fox_attn",
    )(q, k, v, c3)


def _post_mlp_kernel(*refs, emit_next, ff_chunk):
    if emit_next:
        (h_ref, o_ref, wo_ref, gm_ref, wup_ref, wdn_ref,
         ga_ref, gkv_ref, wq_ref, wkv_ref, gq_ref, gk_ref,
         out_ref, qb_ref, kd_ref, vd_ref) = refs
    else:
        h_ref, o_ref, wo_ref, gm_ref, wup_ref, wdn_ref, out_ref = refs
    h1 = h_ref[...] + jnp.dot(o_ref[...], wo_ref[...], preferred_element_type=F32)
    m = ((h1 * _rms_scale(h1)) * gm_ref[...]).astype(BF16)
    acc = h1
    for c in range(D_FF // ff_chunk):
        u = jnp.dot(m, wup_ref[:, c * ff_chunk:(c + 1) * ff_chunk],
                    preferred_element_type=F32)
        u = jnp.square(jnp.maximum(u, 0.0)).astype(BF16)
        acc = acc + jnp.dot(u, wdn_ref[c * ff_chunk:(c + 1) * ff_chunk, :],
                            preferred_element_type=F32)
    out_ref[...] = acc
    if emit_next:
        hn = acc * _rms_scale(acc)
        a = (hn * ga_ref[...]).astype(BF16)
        for c in range(D_MODEL // MXU_N):
            y = jnp.dot(a, wq_ref[:, c * MXU_N:(c + 1) * MXU_N],
                        preferred_element_type=F32)
            for half in range(MXU_N // LANES):
                yy = y[:, half * LANES:(half + 1) * LANES]
                dst = pl.ds(c * MXU_N + half * LANES, LANES)
                qb_ref[:, dst] = (_head_pair_norm(yy, gq_ref[...]) * QK_SCALE).astype(BF16)
        kvn = (hn * gkv_ref[...]).astype(BF16)
        kv = jnp.dot(kvn, wkv_ref[...], preferred_element_type=F32)
        for j in range(KV_HEADS):
            sl = slice(j * LANES, (j + 1) * LANES)
            kd_ref[:, sl] = _head_pair_norm(kv[:, sl], gk_ref[...]).astype(BF16)
            vd_ref[:, sl] = kv[:, KV_HEADS * LANES + j * LANES:
                               KV_HEADS * LANES + (j + 1) * LANES].astype(BF16)


def _post_mlp(h, o, wo, gm, wup, wdn, nxt, tm, ff_chunk=1024):
    t = h.shape[0]
    tok = lambda w: pl.BlockSpec((tm, w), lambda i: (i, 0))
    in_specs = [tok(D_MODEL), tok(D_MODEL), _resident((D_MODEL, D_MODEL)),
                _resident((1, D_MODEL)), _resident((D_MODEL, D_FF)),
                _resident((D_FF, D_MODEL))]
    out_specs = [tok(D_MODEL)]
    out_shape = [jax.ShapeDtypeStruct((t, D_MODEL), F32)]
    args = [h, o, wo, gm, wup, wdn]
    if nxt is not None:
        kvw = 2 * KV_HEADS * LANES
        in_specs += [_resident((1, D_MODEL)), _resident((1, D_MODEL)),
                     _resident((D_MODEL, D_MODEL)), _resident((D_MODEL, kvw)),
                     _resident((1, LANES)), _resident((1, LANES))]
        out_specs += [tok(D_MODEL), tok(KV_HEADS * LANES), tok(KV_HEADS * LANES)]
        out_shape += [jax.ShapeDtypeStruct((t, D_MODEL), BF16),
                      jax.ShapeDtypeStruct((t, KV_HEADS * LANES), BF16),
                      jax.ShapeDtypeStruct((t, KV_HEADS * LANES), BF16)]
        args += list(nxt)
    return pl.pallas_call(
        functools.partial(_post_mlp_kernel, emit_next=nxt is not None,
                          ff_chunk=ff_chunk),
        grid=(t // tm,),
        in_specs=in_specs, out_specs=out_specs, out_shape=out_shape,
        compiler_params=pltpu.CompilerParams(
            dimension_semantics=("parallel",), vmem_limit_bytes=VMEM_LIMIT),
        name="post_mlp_next" if nxt is not None else "post_mlp",
    )(*args)


def _swa_attn_kernel(sink_ref, q_ref, kp_ref, kc_ref, vp_ref, vc_ref, bias_ref,
                     o_ref):
    n = pl.program_id(1)
    kband = jnp.concatenate([kp_ref[...], kc_ref[...]], axis=0)
    vband = jnp.concatenate([vp_ref[...], vc_ref[...]], axis=0)
    lane = lax.broadcasted_iota(jnp.int32, (1, LANES), 1)
    col = lax.broadcasted_iota(jnp.int32, (1, 2 * WINDOW), 1)
    pen = jnp.where((col < WINDOW) & (n == 0), NEG_INF, 0.0)
    for p in range(N_PAIRS):
        kvh = p // (N_PAIRS // KV_HEADS)
        kd = kband[:, kvh * LANES:(kvh + 1) * LANES]
        vd = vband[:, kvh * LANES:(kvh + 1) * LANES]
        q2 = q_ref[:, p * LANES:(p + 1) * LANES]
        outs = []
        for hh in range(2):
            h = 2 * p + hh
            sel = (lane < HEAD_DIM) if hh == 0 else (lane >= HEAD_DIM)
            qm = jnp.where(sel, q2, jnp.zeros_like(q2))
            s = lax.dot_general(qm, kd, (((1,), (1,)), ((), ())),
                                preferred_element_type=F32)
            s = s + bias_ref[h] + pen
            sink = sink_ref[h]
            m = jnp.maximum(jnp.max(s, axis=-1, keepdims=True), sink)
            e = jnp.exp(s - m)
            denom = jnp.sum(e, axis=-1, keepdims=True) + jnp.exp(sink - m)
            outs.append(jnp.dot(e.astype(BF16), vd,
                                preferred_element_type=F32) / denom)
        o_ref[:, p * LANES:(p + 1) * LANES] = jnp.where(
            lane < HEAD_DIM, outs[0], outs[1]).astype(BF16)


def _swa_attn(sinks, q, kd, vd, bias, bsz, seq):
    nblk = seq // WINDOW
    kvw = KV_HEADS * LANES
    cur = lambda b, n: (b * nblk + n, 0)
    prev = lambda b, n: (b * nblk + jnp.maximum(n - 1, 0), 0)
    return pl.pallas_call(
        _swa_attn_kernel,
        grid=(bsz, nblk),
        in_specs=[
            pl.BlockSpec(memory_space=pltpu.SMEM),
            pl.BlockSpec((WINDOW, D_MODEL), cur),
            pl.BlockSpec((WINDOW, kvw), prev), pl.BlockSpec((WINDOW, kvw), cur),
            pl.BlockSpec((WINDOW, kvw), prev), pl.BlockSpec((WINDOW, kvw), cur),
            _resident((N_HEADS, WINDOW, 2 * WINDOW)),
        ],
        out_specs=pl.BlockSpec((WINDOW, D_MODEL), cur),
        out_shape=jax.ShapeDtypeStruct(q.shape, BF16),
        compiler_params=pltpu.CompilerParams(
            dimension_semantics=("parallel", "arbitrary")),
        name="swa_attn",
    )(sinks, q, kd, kd, vd, vd, bias)


def _t5_causal_bucket(dist):
    n = np.maximum(dist, 0)
    max_exact = N_BUCKETS // 2
    large = max_exact + (np.log(np.maximum(n, 1) / max_exact)
                         / np.log(REL_MAX_DIST / max_exact)
                         * (N_BUCKETS - max_exact)).astype(np.int32)
    large = np.minimum(large, N_BUCKETS - 1)
    return np.where(n < max_exact, n, large).astype(np.int32)


def _swa_bias_table(rel_bias):
    dist = np.arange(WINDOW)[:, None] + WINDOW - np.arange(2 * WINDOW)[None, :]
    bias = rel_bias.astype(F32)[_t5_causal_bucket(dist)].transpose(2, 0, 1)
    valid = (dist >= 0) & (dist < WINDOW)
    return jnp.where(valid[None], bias, NEG_INF)


def _dup_heads(w):
    d = w.shape[0]
    w = w.reshape(d, KV_HEADS, 1, HEAD_DIM)
    return jnp.broadcast_to(w, (d, KV_HEADS, 2, HEAD_DIM)).reshape(d, KV_HEADS * LANES)


def kernel(x, g_attn, g_mlp, w_in_a, b_f, gq_a, gk_a, w_out_a, g_kv, w_kv, gk_b,
           w_q_b, gq_b, sinks, rel_bias, w_out_b, w_up, w_down):
    bsz, seq, d = x.shape
    t = bsz * seq
    hw = N_HEADS * HEAD_DIM
    kvw = KV_HEADS * HEAD_DIM
    row = lambda g: g.reshape(1, -1).astype(F32)
    pair = lambda g: jnp.tile(g.astype(F32), 2).reshape(1, LANES)

    x2 = x.reshape(t, d)
    w_in = w_in_a[0]
    wqkv = w_in[:, :3 * hw].astype(BF16)
    wf = jnp.pad(w_in[:, 3 * hw:], ((0, 0), (0, LANES - N_HEADS))).astype(BF16)

    q, k, v, f = _fox_pre(x2, row(g_attn[0]), wqkv, wf, pair(gq_a[0]),
                          pair(gk_a[0]), tm=512)
    f_rows = f.reshape(bsz, seq, N_HEADS).transpose(0, 2, 1)
    c = _fox_decay(f_rows, b_f[0].reshape(N_HEADS, 1).astype(F32))
    o = _fox_attn(q, k, v, c.reshape(bsz * N_HEADS, 1, seq), bsz, seq, tq=256)

    wkv_dup = jnp.concatenate(
        [_dup_heads(w_kv[:, :kvw]), _dup_heads(w_kv[:, kvw:])], axis=1).astype(BF16)
    nxt = (row(g_attn[1]), row(g_kv), w_q_b[0].astype(BF16), wkv_dup,
           pair(gq_b[0]), pair(gk_b))
    h, qb, kd, vd = _post_mlp(x2, o, w_out_a[0].astype(BF16), row(g_mlp[0]),
                              w_up[0].astype(BF16), w_down[0].astype(BF16),
                              nxt, tm=512)

    o2 = _swa_attn(sinks[0].astype(F32), qb, kd, vd, _swa_bias_table(rel_bias),
                   bsz, seq)
    (out,) = _post_mlp(h, o2, w_out_b[0].astype(BF16), row(g_mlp[1]),
                       w_up[1].astype(BF16), w_down[1].astype(BF16), None, tm=512)
    return out.reshape(bsz, seq, d)
```

```python
import functools

import numpy as np
import jax
import jax.numpy as jnp
from jax import lax
from jax.experimental import pallas as pl
from jax.experimental.pallas import tpu as pltpu

D_MODEL = 1024
HEAD_DIM = 64
N_HEADS = 16
N_PAIRS = N_HEADS // 2
KV_HEADS = 2
WINDOW = 128
D_FF = 4 * D_MODEL
N_BUCKETS = 32
REL_MAX_DIST = 128
NORM_EPS = 1e-6
LANES = 128
MXU_N = 256
QK_SCALE = HEAD_DIM ** -0.5
N_SPLIT = 3

F32 = jnp.float32
BF16 = jnp.bfloat16
NEG_INF = float("-inf")
VMEM_LIMIT = 56 * 1024 * 1024


def _rms_scale(x):
    return lax.rsqrt(jnp.mean(x * x, axis=-1, keepdims=True) + NORM_EPS)


def _head_pair_norm(y, g2):
    lo = lax.broadcasted_iota(jnp.int32, (1, LANES), 1) < HEAD_DIM
    sq = y * y
    s_lo = jnp.sum(jnp.where(lo, sq, 0.0), axis=-1, keepdims=True)
    s_hi = jnp.sum(jnp.where(lo, 0.0, sq), axis=-1, keepdims=True)
    ms = jnp.where(lo, s_lo, s_hi) * (1.0 / HEAD_DIM)
    return (y * lax.rsqrt(ms + NORM_EPS)) * g2


def _split_bf16(x):
    terms = []
    for _ in range(N_SPLIT):
        t = x.astype(BF16)
        terms.append(t)
        x = x - t.astype(F32)
    return terms


def _reduce_rows(x, op, group=32):
    parts = [x[r:r + group] for r in range(0, x.shape[0], group)]
    while len(parts) > 1:
        parts = [op(parts[a], parts[a + 1]) for a in range(0, len(parts), 2)]
    red = jnp.max if op is jnp.maximum else jnp.sum
    return red(parts[0], axis=0, keepdims=True)


def _resident(shape):
    zeros = (0,) * len(shape)
    return pl.BlockSpec(shape, lambda *_: zeros, pipeline_mode=pl.Buffered(1))


def _fox_pre_kernel(x_ref, g_ref, wqkv_ref, wf_ref, gq_ref, gk_ref,
                    q_ref, k_ref, vt_ref, f_ref):
    x = x_ref[...]
    hn = ((x * _rms_scale(x)) * g_ref[...]).astype(BF16)
    n_chunks = D_MODEL // MXU_N
    for c in range(3 * n_chunks):
        y = jnp.dot(hn, wqkv_ref[:, c * MXU_N:(c + 1) * MXU_N],
                    preferred_element_type=F32)
        part, cc = divmod(c, n_chunks)
        if part == 2:
            vt_ref[cc * MXU_N:(cc + 1) * MXU_N, :] = y.T.astype(BF16)
            continue
        for half in range(MXU_N // LANES):
            yy = y[:, half * LANES:(half + 1) * LANES]
            dst = pl.ds(cc * MXU_N + half * LANES, LANES)
            if part == 0:
                q_ref[:, dst] = (_head_pair_norm(yy, gq_ref[...]) * QK_SCALE).astype(BF16)
            else:
                k_ref[:, dst] = _head_pair_norm(yy, gk_ref[...]).astype(BF16)
    f_ref[...] = jnp.dot(hn, wf_ref[...], preferred_element_type=F32)


def _fox_pre(x2, g, wqkv, wf, gq2, gk2, tm):
    t = x2.shape[0]
    tok = lambda w: pl.BlockSpec((tm, w), lambda i: (i, 0))
    return pl.pallas_call(
        _fox_pre_kernel,
        grid=(t // tm,),
        in_specs=[tok(D_MODEL), _resident((1, D_MODEL)),
                  _resident((D_MODEL, 3 * D_MODEL)), _resident((D_MODEL, LANES)),
                  _resident((1, LANES)), _resident((1, LANES))],
        out_specs=[tok(D_MODEL), tok(D_MODEL),
                   pl.BlockSpec((D_MODEL, tm), lambda i: (0, i)), tok(LANES)],
        out_shape=[jax.ShapeDtypeStruct((t, D_MODEL), BF16),
                   jax.ShapeDtypeStruct((t, D_MODEL), BF16),
                   jax.ShapeDtypeStruct((D_MODEL, t), BF16),
                   jax.ShapeDtypeStruct((t, LANES), F32)],
        compiler_params=pltpu.CompilerParams(
            dimension_semantics=("parallel",), vmem_limit_bytes=VMEM_LIMIT),
        name="fox_pre",
    )(x2, g, wqkv, wf, gq2, gk2)


def _fox_decay_kernel(f_ref, b_ref, c_ref, *, blk):
    seq = f_ref.shape[0]
    lane = lax.broadcasted_iota(jnp.int32, (1, LANES), 1)
    r = lax.broadcasted_iota(jnp.int32, (blk, blk), 0)
    c = lax.broadcasted_iota(jnp.int32, (blk, blk), 1)
    lower = jnp.where(r >= c, 1.0, 0.0).astype(BF16)
    carry = jnp.zeros((1, LANES), F32)
    for b in range(seq // blk):
        rows = slice(b * blk, (b + 1) * blk)
        x = f_ref[rows, :] + b_ref[...]
        log_f = -(jnp.maximum(-x, 0.0) + jnp.log1p(jnp.exp(-jnp.abs(x))))
        cb = carry
        for term in _split_bf16(log_f):
            cb = cb + jnp.dot(lower, term, preferred_element_type=F32)
        carry = cb[blk - 1:blk, :]
        out = jnp.zeros((blk, LANES), BF16)
        for t, term in reversed(list(enumerate(_split_bf16(cb)))):
            out = jnp.where(lane < (t + 1) * N_HEADS, term, out)
        c_ref[rows, :] = out


def _fox_decay(f3, b3, bsz, seq):
    spec = pl.BlockSpec((seq, LANES), lambda b: (b, 0))
    return pl.pallas_call(
        functools.partial(_fox_decay_kernel, blk=MXU_N),
        grid=(bsz,),
        in_specs=[spec, _resident((1, LANES))],
        out_specs=spec,
        out_shape=jax.ShapeDtypeStruct(f3.shape, BF16),
        compiler_params=pltpu.CompilerParams(dimension_semantics=("parallel",)),
        name="fox_decay",
    )(f3, b3)


def _fox_attn_kernel(q_ref, k_ref, vt_ref, c_ref, o_ref,
                     st_sc, m_sc, l_sc, acc_sc, *, tq):
    p = pl.program_id(1)
    i = pl.program_id(2)
    tk = 2 * tq
    q2 = q_ref[...]
    lane = lax.broadcasted_iota(jnp.int32, (1, LANES), 1)
    t0 = pl.multiple_of(i * tq, tq)
    qa = []
    for hh in range(2):
        own = (lane < HEAD_DIM) if hh == 0 else (lane >= HEAD_DIM)
        qh = jnp.where(own, q2, jnp.zeros_like(q2))
        pick = (lane < N_SPLIT * N_HEADS) & ((lane & (N_HEADS - 1)) == 2 * p + hh)
        neg = jnp.where(pick, -1.0, 0.0).astype(BF16)
        qa.append(jnp.concatenate([qh, jnp.broadcast_to(neg, (tq, LANES))], axis=1))
    m_sc[...] = jnp.full(m_sc.shape, NEG_INF, F32)
    l_sc[...] = jnp.zeros(l_sc.shape, F32)
    acc_sc[...] = jnp.zeros(acc_sc.shape, F32)

    def scores(j, slot):
        ks = pl.multiple_of(j * tk, tk)
        ka = jnp.concatenate([k_ref[pl.ds(ks, tk), :], c_ref[pl.ds(ks, tk), :]],
                             axis=1)
        for hh in range(2):
            st_sc[slot, hh] = lax.dot_general(
                ka, qa[hh], (((1,), (1,)), ((), ())), preferred_element_type=F32)

    def consume(j, slot, masked):
        ks = pl.multiple_of(j * tk, tk)
        if masked:
            krow = lax.broadcasted_iota(jnp.int32, (tk, tq), 0)
            qcol = lax.broadcasted_iota(jnp.int32, (tk, tq), 1)
            visible = krow <= qcol + (t0 - ks)
        old = [(m_sc[hh], l_sc[hh], acc_sc[hh]) for hh in range(2)]
        new = []
        for hh in range(2):
            m_old, l_old, acc_old = old[hh]
            st = st_sc[slot, hh]
            if masked:
                st = jnp.where(visible, st, NEG_INF)
            m_new = jnp.maximum(m_old, _reduce_rows(st, jnp.maximum))
            alpha = jnp.exp(m_old - m_new)
            pt = jnp.exp(st - m_new)
            l_new = alpha * l_old + _reduce_rows(pt, jnp.add)
            vt = vt_ref[hh * HEAD_DIM:(hh + 1) * HEAD_DIM, pl.ds(ks, tk)]
            acc_new = alpha * acc_old + jnp.dot(
                vt, pt.astype(BF16), preferred_element_type=F32)
            new.append((m_new, l_new, acc_new))
        for hh in range(2):
            m_sc[hh], l_sc[hh], acc_sc[hh] = new[hh]

    last = i // 2
    scores(0, 0)

    def step(j, slot):
        scores(j + 1, 1 - slot)
        consume(j, slot, False)

    def two_steps(t, carry):
        step(2 * t, 0)
        step(2 * t + 1, 1)
        return carry

    lax.fori_loop(0, last // 2, two_steps, 0)

    @pl.when((last % 2) == 1)
    def _():
        step(last - 1, 0)
        consume(last, 1, True)

    @pl.when((last % 2) == 0)
    def _():
        consume(last, 0, True)
    ot = jnp.concatenate([acc_sc[0] / l_sc[0], acc_sc[1] / l_sc[1]], axis=0)
    o_ref[...] = ot.T.astype(BF16)


def _fox_attn(q, k, vt, caug, bsz, seq, tq):
    nq = seq // tq
    return pl.pallas_call(
        functools.partial(_fox_attn_kernel, tq=tq),
        grid=(bsz, N_PAIRS, nq),
        in_specs=[
            pl.BlockSpec((tq, LANES), lambda b, p, i: (b * nq + i, p)),
            pl.BlockSpec((seq, LANES), lambda b, p, i: (b, p)),
            pl.BlockSpec((LANES, seq), lambda b, p, i: (p, b)),
            pl.BlockSpec((seq, LANES), lambda b, p, i: (b, 0)),
        ],
        out_specs=pl.BlockSpec((tq, LANES), lambda b, p, i: (b * nq + i, p)),
        out_shape=jax.ShapeDtypeStruct(q.shape, BF16),
        scratch_shapes=[pltpu.VMEM((2, 2, 2 * tq, tq), F32),
                        pltpu.VMEM((2, 1, tq), F32), pltpu.VMEM((2, 1, tq), F32),
                        pltpu.VMEM((2, HEAD_DIM, tq), F32)],
        compiler_params=pltpu.CompilerParams(
            dimension_semantics=("parallel", "parallel", "arbitrary")),
        name="fox_attn",
    )(q, k, vt, caug)


def _post_mlp_kernel(*refs, emit_next, ff_chunk):
    if emit_next:
        (h_ref, o_ref, wo_ref, gm_ref, wup_ref, wdn_ref,
         ga_ref, gkv_ref, wq_ref, wkv_ref, gq_ref, gk_ref,
         out_ref, qb_ref, kd_ref, vd_ref) = refs
    else:
        h_ref, o_ref, wo_ref, gm_ref, wup_ref, wdn_ref, out_ref = refs
    h1 = h_ref[...] + jnp.dot(o_ref[...], wo_ref[...], preferred_element_type=F32)
    m = ((h1 * _rms_scale(h1)) * gm_ref[...]).astype(BF16)
    acc = h1
    for c in range(D_FF // ff_chunk):
        u = jnp.dot(m, wup_ref[:, c * ff_chunk:(c + 1) * ff_chunk],
                    preferred_element_type=F32)
        u = jnp.square(jnp.maximum(u, 0.0)).astype(BF16)
        acc = acc + jnp.dot(u, wdn_ref[c * ff_chunk:(c + 1) * ff_chunk, :],
                            preferred_element_type=F32)
    out_ref[...] = acc
    if emit_next:
        hn = acc * _rms_scale(acc)
        a = (hn * ga_ref[...]).astype(BF16)
        for c in range(D_MODEL // MXU_N):
            y = jnp.dot(a, wq_ref[:, c * MXU_N:(c + 1) * MXU_N],
                        preferred_element_type=F32)
            for half in range(MXU_N // LANES):
                yy = y[:, half * LANES:(half + 1) * LANES]
                dst = pl.ds(c * MXU_N + half * LANES, LANES)
                qb_ref[:, dst] = (_head_pair_norm(yy, gq_ref[...]) * QK_SCALE).astype(BF16)
        kvn = (hn * gkv_ref[...]).astype(BF16)
        kv = jnp.dot(kvn, wkv_ref[...], preferred_element_type=F32)
        for j in range(KV_HEADS):
            sl = slice(j * LANES, (j + 1) * LANES)
            kd_ref[:, sl] = _head_pair_norm(kv[:, sl], gk_ref[...]).astype(BF16)
            vd_ref[:, sl] = kv[:, KV_HEADS * LANES + j * LANES:
                               KV_HEADS * LANES + (j + 1) * LANES].astype(BF16)


def _post_mlp(h, o, wo, gm, wup, wdn, nxt, tm, ff_chunk=1024):
    t = h.shape[0]
    tok = lambda w: pl.BlockSpec((tm, w), lambda i: (i, 0))
    in_specs = [tok(D_MODEL), tok(D_MODEL), _resident((D_MODEL, D_MODEL)),
                _resident((1, D_MODEL)), _resident((D_MODEL, D_FF)),
                _resident((D_FF, D_MODEL))]
    out_specs = [tok(D_MODEL)]
    out_shape = [jax.ShapeDtypeStruct((t, D_MODEL), F32)]
    args = [h, o, wo, gm, wup, wdn]
    if nxt is not None:
        kvw = 2 * KV_HEADS * LANES
        in_specs += [_resident((1, D_MODEL)), _resident((1, D_MODEL)),
                     _resident((D_MODEL, D_MODEL)), _resident((D_MODEL, kvw)),
                     _resident((1, LANES)), _resident((1, LANES))]
        out_specs += [tok(D_MODEL), tok(KV_HEADS * LANES), tok(KV_HEADS * LANES)]
        out_shape += [jax.ShapeDtypeStruct((t, D_MODEL), BF16),
                      jax.ShapeDtypeStruct((t, KV_HEADS * LANES), BF16),
                      jax.ShapeDtypeStruct((t, KV_HEADS * LANES), BF16)]
        args += list(nxt)
    return pl.pallas_call(
        functools.partial(_post_mlp_kernel, emit_next=nxt is not None,
                          ff_chunk=ff_chunk),
        grid=(t // tm,),
        in_specs=in_specs, out_specs=out_specs, out_shape=out_shape,
        compiler_params=pltpu.CompilerParams(
            dimension_semantics=("parallel",), vmem_limit_bytes=VMEM_LIMIT),
        name="post_mlp_next" if nxt is not None else "post_mlp",
    )(*args)


def _swa_attn_kernel(sink_ref, q_ref, kp_ref, kc_ref, vp_ref, vc_ref, bias_ref,
                     o_ref):
    n = pl.program_id(1)
    kband = jnp.concatenate([kp_ref[...], kc_ref[...]], axis=0)
    vband = jnp.concatenate([vp_ref[...], vc_ref[...]], axis=0)
    lane = lax.broadcasted_iota(jnp.int32, (1, LANES), 1)
    col = lax.broadcasted_iota(jnp.int32, (1, 2 * WINDOW), 1)
    pen = jnp.where((col < WINDOW) & (n == 0), NEG_INF, 0.0)
    for p in range(N_PAIRS):
        kvh = p // (N_PAIRS // KV_HEADS)
        kd = kband[:, kvh * LANES:(kvh + 1) * LANES]
        vd = vband[:, kvh * LANES:(kvh + 1) * LANES]
        q2 = q_ref[:, p * LANES:(p + 1) * LANES]
        outs = []
        for hh in range(2):
            h = 2 * p + hh
            sel = (lane < HEAD_DIM) if hh == 0 else (lane >= HEAD_DIM)
            qm = jnp.where(sel, q2, jnp.zeros_like(q2))
            s = lax.dot_general(qm, kd, (((1,), (1,)), ((), ())),
                                preferred_element_type=F32)
            s = s + bias_ref[h] + pen
            sink = sink_ref[h]
            m = jnp.maximum(jnp.max(s, axis=-1, keepdims=True), sink)
            e = jnp.exp(s - m)
            denom = jnp.sum(e, axis=-1, keepdims=True) + jnp.exp(sink - m)
            outs.append(jnp.dot(e.astype(BF16), vd,
                                preferred_element_type=F32) / denom)
        o_ref[:, p * LANES:(p + 1) * LANES] = jnp.where(
            lane < HEAD_DIM, outs[0], outs[1]).astype(BF16)


def _swa_attn(sinks, q, kd, vd, bias, bsz, seq):
    nblk = seq // WINDOW
    kvw = KV_HEADS * LANES
    cur = lambda b, n: (b * nblk + n, 0)
    prev = lambda b, n: (b * nblk + jnp.maximum(n - 1, 0), 0)
    return pl.pallas_call(
        _swa_attn_kernel,
        grid=(bsz, nblk),
        in_specs=[
            pl.BlockSpec(memory_space=pltpu.SMEM),
            pl.BlockSpec((WINDOW, D_MODEL), cur),
            pl.BlockSpec((WINDOW, kvw), prev), pl.BlockSpec((WINDOW, kvw), cur),
            pl.BlockSpec((WINDOW, kvw), prev), pl.BlockSpec((WINDOW, kvw), cur),
            _resident((N_HEADS, WINDOW, 2 * WINDOW)),
        ],
        out_specs=pl.BlockSpec((WINDOW, D_MODEL), cur),
        out_shape=jax.ShapeDtypeStruct(q.shape, BF16),
        compiler_params=pltpu.CompilerParams(
            dimension_semantics=("parallel", "arbitrary")),
        name="swa_attn",
    )(sinks, q, kd, kd, vd, vd, bias)


def _t5_causal_bucket(dist):
    n = np.maximum(dist, 0)
    max_exact = N_BUCKETS // 2
    large = max_exact + (np.log(np.maximum(n, 1) / max_exact)
                         / np.log(REL_MAX_DIST / max_exact)
                         * (N_BUCKETS - max_exact)).astype(np.int32)
    large = np.minimum(large, N_BUCKETS - 1)
    return np.where(n < max_exact, n, large).astype(np.int32)


def _swa_bias_table(rel_bias):
    dist = np.arange(WINDOW)[:, None] + WINDOW - np.arange(2 * WINDOW)[None, :]
    bias = rel_bias.astype(F32)[_t5_causal_bucket(dist)].transpose(2, 0, 1)
    valid = (dist >= 0) & (dist < WINDOW)
    return jnp.where(valid[None], bias, NEG_INF)


def _dup_heads(w):
    d = w.shape[0]
    w = w.reshape(d, KV_HEADS, 1, HEAD_DIM)
    return jnp.broadcast_to(w, (d, KV_HEADS, 2, HEAD_DIM)).reshape(d, KV_HEADS * LANES)


def _lane_groups(v):
    return jnp.pad(jnp.tile(v, N_SPLIT), (0, LANES - N_SPLIT * N_HEADS)).reshape(1, LANES)


def kernel(x, g_attn, g_mlp, w_in_a, b_f, gq_a, gk_a, w_out_a, g_kv, w_kv, gk_b,
           w_q_b, gq_b, sinks, rel_bias, w_out_b, w_up, w_down):
    bsz, seq, d = x.shape
    t = bsz * seq
    hw = N_HEADS * HEAD_DIM
    kvw = KV_HEADS * HEAD_DIM
    row = lambda g: g.reshape(1, -1).astype(F32)
    pair = lambda g: jnp.tile(g.astype(F32), 2).reshape(1, LANES)

    x2 = x.reshape(t, d)
    w_in = w_in_a[0]
    wqkv = w_in[:, :3 * hw].astype(BF16)
    wf = jnp.pad(jnp.tile(w_in[:, 3 * hw:], (1, N_SPLIT)),
                 ((0, 0), (0, LANES - N_SPLIT * N_HEADS))).astype(BF16)

    q, k, vt, f3 = _fox_pre(x2, row(g_attn[0]), wqkv, wf, pair(gq_a[0]),
                            pair(gk_a[0]), tm=512)
    caug = _fox_decay(f3, _lane_groups(b_f[0].astype(F32)), bsz, seq)
    o = _fox_attn(q, k, vt, caug, bsz, seq, tq=256)

    wkv_dup = jnp.concatenate(
        [_dup_heads(w_kv[:, :kvw]), _dup_heads(w_kv[:, kvw:])], axis=1).astype(BF16)
    nxt = (row(g_attn[1]), row(g_kv), w_q_b[0].astype(BF16), wkv_dup,
           pair(gq_b[0]), pair(gk_b))
    h, qb, kd, vd = _post_mlp(x2, o, w_out_a[0].astype(BF16), row(g_mlp[0]),
                              w_up[0].astype(BF16), w_down[0].astype(BF16),
                              nxt, tm=512)

    o2 = _swa_attn(sinks[0].astype(F32), qb, kd, vd, _swa_bias_table(rel_bias),
                   bsz, seq)
    (out,) = _post_mlp(h, o2, w_out_b[0].astype(BF16), row(g_mlp[1]),
                       w_up[1].astype(BF16), w_down[1].astype(BF16), None, tm=512)
    return out.reshape(bsz, seq, d)
```

```python
import functools

import numpy as np
import jax
import jax.numpy as jnp
from jax import lax
from jax.experimental import pallas as pl
from jax.experimental.pallas import tpu as pltpu

D_MODEL = 1024
HEAD_DIM = 64
N_HEADS = 16
N_PAIRS = N_HEADS // 2
KV_HEADS = 2
WINDOW = 128
D_FF = 4 * D_MODEL
N_BUCKETS = 32
REL_MAX_DIST = 128
NORM_EPS = 1e-6
LANES = 128
MXU_N = 256
QK_SCALE = HEAD_DIM ** -0.5
N_SPLIT = 3

F32 = jnp.float32
BF16 = jnp.bfloat16
NEG_INF = float("-inf")
VMEM_LIMIT = 56 * 1024 * 1024


def _rms_scale(x):
    return lax.rsqrt(jnp.mean(x * x, axis=-1, keepdims=True) + NORM_EPS)


def _head_pair_norm(y, g2):
    lo = lax.broadcasted_iota(jnp.int32, (1, LANES), 1) < HEAD_DIM
    sq = y * y
    s_lo = jnp.sum(jnp.where(lo, sq, 0.0), axis=-1, keepdims=True)
    s_hi = jnp.sum(jnp.where(lo, 0.0, sq), axis=-1, keepdims=True)
    ms = jnp.where(lo, s_lo, s_hi) * (1.0 / HEAD_DIM)
    return (y * lax.rsqrt(ms + NORM_EPS)) * g2


def _split_bf16(x):
    terms = []
    for _ in range(N_SPLIT):
        t = x.astype(BF16)
        terms.append(t)
        x = x - t.astype(F32)
    return terms


def _reduce_rows(x, op, group=32):
    parts = [x[r:r + group] for r in range(0, x.shape[0], group)]
    while len(parts) > 1:
        parts = [op(parts[a], parts[a + 1]) for a in range(0, len(parts), 2)]
    red = jnp.max if op is jnp.maximum else jnp.sum
    return red(parts[0], axis=0, keepdims=True)


def _resident(shape):
    zeros = (0,) * len(shape)
    return pl.BlockSpec(shape, lambda *_: zeros, pipeline_mode=pl.Buffered(1))


def _fox_pre_kernel(x_ref, g_ref, wqkv_ref, wf_ref, gq_ref, gk_ref,
                    q_ref, k_ref, vt_ref, f_ref):
    x = x_ref[...]
    hn = ((x * _rms_scale(x)) * g_ref[...]).astype(BF16)
    n_chunks = D_MODEL // MXU_N
    for c in range(3 * n_chunks):
        y = jnp.dot(hn, wqkv_ref[:, c * MXU_N:(c + 1) * MXU_N],
                    preferred_element_type=F32)
        part, cc = divmod(c, n_chunks)
        if part == 2:
            vt_ref[cc * MXU_N:(cc + 1) * MXU_N, :] = y.T.astype(BF16)
            continue
        for half in range(MXU_N // LANES):
            yy = y[:, half * LANES:(half + 1) * LANES]
            dst = pl.ds(cc * MXU_N + half * LANES, LANES)
            if part == 0:
                q_ref[:, dst] = (_head_pair_norm(yy, gq_ref[...]) * QK_SCALE).astype(BF16)
            else:
                k_ref[:, dst] = _head_pair_norm(yy, gk_ref[...]).astype(BF16)
    f_ref[...] = jnp.dot(hn, wf_ref[...], preferred_element_type=F32)


def _fox_pre(x2, g, wqkv, wf, gq2, gk2, tm):
    t = x2.shape[0]
    tok = lambda w: pl.BlockSpec((tm, w), lambda i: (i, 0))
    return pl.pallas_call(
        _fox_pre_kernel,
        grid=(t // tm,),
        in_specs=[tok(D_MODEL), _resident((1, D_MODEL)),
                  _resident((D_MODEL, 3 * D_MODEL)), _resident((D_MODEL, LANES)),
                  _resident((1, LANES)), _resident((1, LANES))],
        out_specs=[tok(D_MODEL), tok(D_MODEL),
                   pl.BlockSpec((D_MODEL, tm), lambda i: (0, i)), tok(LANES)],
        out_shape=[jax.ShapeDtypeStruct((t, D_MODEL), BF16),
                   jax.ShapeDtypeStruct((t, D_MODEL), BF16),
                   jax.ShapeDtypeStruct((D_MODEL, t), BF16),
                   jax.ShapeDtypeStruct((t, LANES), F32)],
        compiler_params=pltpu.CompilerParams(
            dimension_semantics=("parallel",), vmem_limit_bytes=VMEM_LIMIT),
        name="fox_pre",
    )(x2, g, wqkv, wf, gq2, gk2)


def _fox_decay_kernel(f_ref, b_ref, c_ref, *, blk):
    seq = f_ref.shape[0]
    lane = lax.broadcasted_iota(jnp.int32, (1, LANES), 1)
    r = lax.broadcasted_iota(jnp.int32, (blk, blk), 0)
    c = lax.broadcasted_iota(jnp.int32, (blk, blk), 1)
    lower = jnp.where(r >= c, 1.0, 0.0).astype(BF16)
    carry = jnp.zeros((1, LANES), F32)
    for b in range(seq // blk):
        rows = slice(b * blk, (b + 1) * blk)
        x = f_ref[rows, :] + b_ref[...]
        log_f = -(jnp.maximum(-x, 0.0) + jnp.log1p(jnp.exp(-jnp.abs(x))))
        cb = carry
        for term in _split_bf16(log_f):
            cb = cb + jnp.dot(lower, term, preferred_element_type=F32)
        carry = cb[blk - 1:blk, :]
        out = jnp.zeros((blk, LANES), BF16)
        for t, term in reversed(list(enumerate(_split_bf16(cb)))):
            out = jnp.where(lane < (t + 1) * N_HEADS, term, out)
        c_ref[rows, :] = out


def _fox_decay(f3, b3, bsz, seq):
    spec = pl.BlockSpec((seq, LANES), lambda b: (b, 0))
    return pl.pallas_call(
        functools.partial(_fox_decay_kernel, blk=MXU_N),
        grid=(bsz,),
        in_specs=[spec, _resident((1, LANES))],
        out_specs=spec,
        out_shape=jax.ShapeDtypeStruct(f3.shape, BF16),
        compiler_params=pltpu.CompilerParams(dimension_semantics=("parallel",)),
        name="fox_decay",
    )(f3, b3)


def _fox_attn_kernel(q_ref, k_ref, vt_ref, c_ref, o_ref, st_sc, *, tq, seq):
    p = pl.program_id(1)
    tk = 2 * tq
    lane = lax.broadcasted_iota(jnp.int32, (1, LANES), 1)
    own = [lane < HEAD_DIM, lane >= HEAD_DIM]
    neg = [jnp.broadcast_to(
        jnp.where((lane < N_SPLIT * N_HEADS)
                  & ((lane & (N_HEADS - 1)) == 2 * p + hh), -1.0, 0.0).astype(BF16),
        (tq, LANES)) for hh in range(2)]

    items = []
    for i in range(seq // tq):
        n_keys = (i + 1) * tq
        starts = list(range(0, n_keys, tk))
        for ks in starts:
            items.append((i, ks, min(tk, n_keys - ks), ks == starts[-1]))

    def query_blocks(i):
        q2 = q_ref[i * tq:(i + 1) * tq, :]
        return [jnp.concatenate(
            [jnp.where(own[hh], q2, jnp.zeros_like(q2)), neg[hh]], axis=1)
            for hh in range(2)]

    def scores(item, qa, slot):
        _, ks, width, _ = item
        ka = jnp.concatenate([k_ref[ks:ks + width, :], c_ref[ks:ks + width, :]],
                             axis=1)
        for hh in range(2):
            st_sc[slot, hh, 0:width, :] = lax.dot_general(
                ka, qa[hh], (((1,), (1,)), ((), ())), preferred_element_type=F32)

    def consume(item, slot, state):
        i, ks, width, is_last = item
        if is_last:
            krow = lax.broadcasted_iota(jnp.int32, (width, tq), 0)
            qcol = lax.broadcasted_iota(jnp.int32, (width, tq), 1)
            visible = krow <= qcol + (i * tq - ks)
        new = []
        for hh in range(2):
            m_old, l_old, acc_old = state[hh]
            st = st_sc[slot, hh, 0:width, :]
            if is_last:
                st = jnp.where(visible, st, NEG_INF)
            m_new = jnp.maximum(m_old, _reduce_rows(st, jnp.maximum))
            alpha = jnp.exp(m_old - m_new)
            pt = jnp.exp(st - m_new)
            l_new = alpha * l_old + _reduce_rows(pt, jnp.add)
            vt = vt_ref[hh * HEAD_DIM:(hh + 1) * HEAD_DIM, ks:ks + width]
            acc_new = alpha * acc_old + jnp.dot(
                vt, pt.astype(BF16), preferred_element_type=F32)
            new.append((m_new, l_new, acc_new))
        return new

    fresh = [(jnp.full((1, tq), NEG_INF, F32), jnp.zeros((1, tq), F32),
              jnp.zeros((HEAD_DIM, tq), F32))] * 2
    qa = query_blocks(0)
    scores(items[0], qa, 0)
    state = fresh
    for t, item in enumerate(items):
        if t + 1 < len(items):
            if items[t + 1][0] != item[0]:
                qa = query_blocks(items[t + 1][0])
            scores(items[t + 1], qa, (t + 1) % 2)
        state = consume(item, t % 2, state)
        if item[3]:
            i = item[0]
            ot = jnp.concatenate([acc / l for _, l, acc in state], axis=0)
            o_ref[i * tq:(i + 1) * tq, :] = ot.T.astype(BF16)
            state = fresh


def _fox_attn(q, k, vt, caug, bsz, seq, tq):
    tok = pl.BlockSpec((seq, LANES), lambda b, p: (b, p))
    return pl.pallas_call(
        functools.partial(_fox_attn_kernel, tq=tq, seq=seq),
        grid=(bsz, N_PAIRS),
        in_specs=[tok, tok,
                  pl.BlockSpec((LANES, seq), lambda b, p: (p, b)),
                  pl.BlockSpec((seq, LANES), lambda b, p: (b, 0))],
        out_specs=tok,
        out_shape=jax.ShapeDtypeStruct(q.shape, BF16),
        scratch_shapes=[pltpu.VMEM((2, 2, 2 * tq, tq), F32)],
        compiler_params=pltpu.CompilerParams(
            dimension_semantics=("parallel", "parallel")),
        name="fox_attn",
    )(q, k, vt, caug)


def _post_mlp_kernel(*refs, emit_next, ff_chunk):
    if emit_next:
        (h_ref, o_ref, wo_ref, gm_ref, wup_ref, wdn_ref,
         ga_ref, gkv_ref, wq_ref, wkv_ref, gq_ref, gk_ref,
         out_ref, qb_ref, kd_ref, vd_ref) = refs
    else:
        h_ref, o_ref, wo_ref, gm_ref, wup_ref, wdn_ref, out_ref = refs
    h1 = h_ref[...] + jnp.dot(o_ref[...], wo_ref[...], preferred_element_type=F32)
    m = ((h1 * _rms_scale(h1)) * gm_ref[...]).astype(BF16)
    acc = h1
    for c in range(D_FF // ff_chunk):
        u = jnp.dot(m, wup_ref[:, c * ff_chunk:(c + 1) * ff_chunk],
                    preferred_element_type=F32)
        u = jnp.square(jnp.maximum(u, 0.0)).astype(BF16)
        acc = acc + jnp.dot(u, wdn_ref[c * ff_chunk:(c + 1) * ff_chunk, :],
                            preferred_element_type=F32)
    out_ref[...] = acc
    if emit_next:
        hn = acc * _rms_scale(acc)
        a = (hn * ga_ref[...]).astype(BF16)
        for c in range(D_MODEL // MXU_N):
            y = jnp.dot(a, wq_ref[:, c * MXU_N:(c + 1) * MXU_N],
                        preferred_element_type=F32)
            for half in range(MXU_N // LANES):
                yy = y[:, half * LANES:(half + 1) * LANES]
                dst = pl.ds(c * MXU_N + half * LANES, LANES)
                qb_ref[:, dst] = (_head_pair_norm(yy, gq_ref[...]) * QK_SCALE).astype(BF16)
        kvn = (hn * gkv_ref[...]).astype(BF16)
        kv = jnp.dot(kvn, wkv_ref[...], preferred_element_type=F32)
        for j in range(KV_HEADS):
            sl = slice(j * LANES, (j + 1) * LANES)
            kd_ref[:, sl] = _head_pair_norm(kv[:, sl], gk_ref[...]).astype(BF16)
            vd_ref[:, sl] = kv[:, KV_HEADS * LANES + j * LANES:
                               KV_HEADS * LANES + (j + 1) * LANES].astype(BF16)


def _post_mlp(h, o, wo, gm, wup, wdn, nxt, tm, ff_chunk=1024):
    t = h.shape[0]
    tok = lambda w: pl.BlockSpec((tm, w), lambda i: (i, 0))
    in_specs = [tok(D_MODEL), tok(D_MODEL), _resident((D_MODEL, D_MODEL)),
                _resident((1, D_MODEL)), _resident((D_MODEL, D_FF)),
                _resident((D_FF, D_MODEL))]
    out_specs = [tok(D_MODEL)]
    out_shape = [jax.ShapeDtypeStruct((t, D_MODEL), F32)]
    args = [h, o, wo, gm, wup, wdn]
    if nxt is not None:
        kvw = 2 * KV_HEADS * LANES
        in_specs += [_resident((1, D_MODEL)), _resident((1, D_MODEL)),
                     _resident((D_MODEL, D_MODEL)), _resident((D_MODEL, kvw)),
                     _resident((1, LANES)), _resident((1, LANES))]
        out_specs += [tok(D_MODEL), tok(KV_HEADS * LANES), tok(KV_HEADS * LANES)]
        out_shape += [jax.ShapeDtypeStruct((t, D_MODEL), BF16),
                      jax.ShapeDtypeStruct((t, KV_HEADS * LANES), BF16),
                      jax.ShapeDtypeStruct((t, KV_HEADS * LANES), BF16)]
        args += list(nxt)
    return pl.pallas_call(
        functools.partial(_post_mlp_kernel, emit_next=nxt is not None,
                          ff_chunk=ff_chunk),
        grid=(t // tm,),
        in_specs=in_specs, out_specs=out_specs, out_shape=out_shape,
        compiler_params=pltpu.CompilerParams(
            dimension_semantics=("parallel",), vmem_limit_bytes=VMEM_LIMIT),
        name="post_mlp_next" if nxt is not None else "post_mlp",
    )(*args)


def _swa_attn_kernel(sink_ref, q_ref, kp_ref, kc_ref, vp_ref, vc_ref, bias_ref,
                     o_ref):
    n = pl.program_id(1)
    kband = jnp.concatenate([kp_ref[...], kc_ref[...]], axis=0)
    vband = jnp.concatenate([vp_ref[...], vc_ref[...]], axis=0)
    lane = lax.broadcasted_iota(jnp.int32, (1, LANES), 1)
    col = lax.broadcasted_iota(jnp.int32, (1, 2 * WINDOW), 1)
    pen = jnp.where((col < WINDOW) & (n == 0), NEG_INF, 0.0)
    for p in range(N_PAIRS):
        kvh = p // (N_PAIRS // KV_HEADS)
        kd = kband[:, kvh * LANES:(kvh + 1) * LANES]
        vd = vband[:, kvh * LANES:(kvh + 1) * LANES]
        q2 = q_ref[:, p * LANES:(p + 1) * LANES]
        outs = []
        for hh in range(2):
            h = 2 * p + hh
            sel = (lane < HEAD_DIM) if hh == 0 else (lane >= HEAD_DIM)
            qm = jnp.where(sel, q2, jnp.zeros_like(q2))
            s = lax.dot_general(qm, kd, (((1,), (1,)), ((), ())),
                                preferred_element_type=F32)
            s = s + bias_ref[h] + pen
            sink = sink_ref[h]
            m = jnp.maximum(jnp.max(s, axis=-1, keepdims=True), sink)
            e = jnp.exp(s - m)
            denom = jnp.sum(e, axis=-1, keepdims=True) + jnp.exp(sink - m)
            outs.append(jnp.dot(e.astype(BF16), vd,
                                preferred_element_type=F32) / denom)
        o_ref[:, p * LANES:(p + 1) * LANES] = jnp.where(
            lane < HEAD_DIM, outs[0], outs[1]).astype(BF16)


def _swa_attn(sinks, q, kd, vd, bias, bsz, seq):
    nblk = seq // WINDOW
    kvw = KV_HEADS * LANES
    cur = lambda b, n: (b * nblk + n, 0)
    prev = lambda b, n: (b * nblk + jnp.maximum(n - 1, 0), 0)
    return pl.pallas_call(
        _swa_attn_kernel,
        grid=(bsz, nblk),
        in_specs=[
            pl.BlockSpec(memory_space=pltpu.SMEM),
            pl.BlockSpec((WINDOW, D_MODEL), cur),
            pl.BlockSpec((WINDOW, kvw), prev), pl.BlockSpec((WINDOW, kvw), cur),
            pl.BlockSpec((WINDOW, kvw), prev), pl.BlockSpec((WINDOW, kvw), cur),
            _resident((N_HEADS, WINDOW, 2 * WINDOW)),
        ],
        out_specs=pl.BlockSpec((WINDOW, D_MODEL), cur),
        out_shape=jax.ShapeDtypeStruct(q.shape, BF16),
        compiler_params=pltpu.CompilerParams(
            dimension_semantics=("parallel", "arbitrary")),
        name="swa_attn",
    )(sinks, q, kd, kd, vd, vd, bias)


def _t5_causal_bucket(dist):
    n = np.maximum(dist, 0)
    max_exact = N_BUCKETS // 2
    large = max_exact + (np.log(np.maximum(n, 1) / max_exact)
                         / np.log(REL_MAX_DIST / max_exact)
                         * (N_BUCKETS - max_exact)).astype(np.int32)
    large = np.minimum(large, N_BUCKETS - 1)
    return np.where(n < max_exact, n, large).astype(np.int32)


def _swa_bias_table(rel_bias):
    dist = np.arange(WINDOW)[:, None] + WINDOW - np.arange(2 * WINDOW)[None, :]
    bias = rel_bias.astype(F32)[_t5_causal_bucket(dist)].transpose(2, 0, 1)
    valid = (dist >= 0) & (dist < WINDOW)
    return jnp.where(valid[None], bias, NEG_INF)


def _dup_heads(w):
    d = w.shape[0]
    w = w.reshape(d, KV_HEADS, 1, HEAD_DIM)
    return jnp.broadcast_to(w, (d, KV_HEADS, 2, HEAD_DIM)).reshape(d, KV_HEADS * LANES)


def _lane_groups(v):
    return jnp.pad(jnp.tile(v, N_SPLIT), (0, LANES - N_SPLIT * N_HEADS)).reshape(1, LANES)


def kernel(x, g_attn, g_mlp, w_in_a, b_f, gq_a, gk_a, w_out_a, g_kv, w_kv, gk_b,
           w_q_b, gq_b, sinks, rel_bias, w_out_b, w_up, w_down):
    bsz, seq, d = x.shape
    t = bsz * seq
    hw = N_HEADS * HEAD_DIM
    kvw = KV_HEADS * HEAD_DIM
    row = lambda g: g.reshape(1, -1).astype(F32)
    pair = lambda g: jnp.tile(g.astype(F32), 2).reshape(1, LANES)

    x2 = x.reshape(t, d)
    w_in = w_in_a[0]
    wqkv = w_in[:, :3 * hw].astype(BF16)
    wf = jnp.pad(jnp.tile(w_in[:, 3 * hw:], (1, N_SPLIT)),
                 ((0, 0), (0, LANES - N_SPLIT * N_HEADS))).astype(BF16)

    q, k, vt, f3 = _fox_pre(x2, row(g_attn[0]), wqkv, wf, pair(gq_a[0]),
                            pair(gk_a[0]), tm=512)
    caug = _fox_decay(f3, _lane_groups(b_f[0].astype(F32)), bsz, seq)
    o = _fox_attn(q, k, vt, caug, bsz, seq, tq=256)

    wkv_dup = jnp.concatenate(
        [_dup_heads(w_kv[:, :kvw]), _dup_heads(w_kv[:, kvw:])], axis=1).astype(BF16)
    nxt = (row(g_attn[1]), row(g_kv), w_q_b[0].astype(BF16), wkv_dup,
           pair(gq_b[0]), pair(gk_b))
    h, qb, kd, vd = _post_mlp(x2, o, w_out_a[0].astype(BF16), row(g_mlp[0]),
                              w_up[0].astype(BF16), w_down[0].astype(BF16),
                              nxt, tm=512)

    o2 = _swa_attn(sinks[0].astype(F32), qb, kd, vd, _swa_bias_table(rel_bias),
                   bsz, seq)
    (out,) = _post_mlp(h, o2, w_out_b[0].astype(BF16), row(g_mlp[1]),
                       w_up[1].astype(BF16), w_down[1].astype(BF16), None, tm=512)
    return out.reshape(bsz, seq, d)
```

```python
import functools

import numpy as np
import jax
import jax.numpy as jnp
from jax import lax
from jax.experimental import pallas as pl
from jax.experimental.pallas import tpu as pltpu

D_MODEL = 1024
HEAD_DIM = 64
N_HEADS = 16
N_PAIRS = N_HEADS // 2
KV_HEADS = 2
WINDOW = 128
D_FF = 4 * D_MODEL
N_BUCKETS = 32
REL_MAX_DIST = 128
NORM_EPS = 1e-6
LANES = 128
MXU_N = 256
QK_SCALE = HEAD_DIM ** -0.5
N_SPLIT = 3

F32 = jnp.float32
BF16 = jnp.bfloat16
NEG_INF = float("-inf")
VMEM_LIMIT = 56 * 1024 * 1024


def _rms_scale(x):
    return lax.rsqrt(jnp.mean(x * x, axis=-1, keepdims=True) + NORM_EPS)


def _head_pair_norm(y, g2):
    lo = lax.broadcasted_iota(jnp.int32, (1, LANES), 1) < HEAD_DIM
    sq = y * y
    s_lo = jnp.sum(jnp.where(lo, sq, 0.0), axis=-1, keepdims=True)
    s_hi = jnp.sum(jnp.where(lo, 0.0, sq), axis=-1, keepdims=True)
    ms = jnp.where(lo, s_lo, s_hi) * (1.0 / HEAD_DIM)
    return (y * lax.rsqrt(ms + NORM_EPS)) * g2


def _split_bf16(x):
    terms = []
    for _ in range(N_SPLIT):
        t = x.astype(BF16)
        terms.append(t)
        x = x - t.astype(F32)
    return terms


def _reduce_rows(x, op, group=32):
    parts = [x[r:r + group] for r in range(0, x.shape[0], group)]
    while len(parts) > 1:
        parts = [op(parts[a], parts[a + 1]) for a in range(0, len(parts), 2)]
    red = jnp.max if op is jnp.maximum else jnp.sum
    return red(parts[0], axis=0, keepdims=True)


def _resident(shape):
    zeros = (0,) * len(shape)
    return pl.BlockSpec(shape, lambda *_: zeros, pipeline_mode=pl.Buffered(1))


def _fox_pre_kernel(x_ref, g_ref, wqkv_ref, wf_ref, gq_ref, gk_ref,
                    q_ref, k_ref, vt_ref, f_ref):
    x = x_ref[...]
    hn = ((x * _rms_scale(x)) * g_ref[...]).astype(BF16)
    n_chunks = D_MODEL // MXU_N
    for c in range(3 * n_chunks):
        y = jnp.dot(hn, wqkv_ref[:, c * MXU_N:(c + 1) * MXU_N],
                    preferred_element_type=F32)
        part, cc = divmod(c, n_chunks)
        if part == 2:
            vt_ref[cc * MXU_N:(cc + 1) * MXU_N, :] = y.T.astype(BF16)
            continue
        for half in range(MXU_N // LANES):
            yy = y[:, half * LANES:(half + 1) * LANES]
            dst = pl.ds(cc * MXU_N + half * LANES, LANES)
            if part == 0:
                q_ref[:, dst] = (_head_pair_norm(yy, gq_ref[...]) * QK_SCALE).astype(BF16)
            else:
                k_ref[:, dst] = _head_pair_norm(yy, gk_ref[...]).astype(BF16)
    f_ref[...] = jnp.dot(hn, wf_ref[...], preferred_element_type=F32)


def _fox_pre(x2, g, wqkv, wf, gq2, gk2, tm):
    t = x2.shape[0]
    tok = lambda w: pl.BlockSpec((tm, w), lambda i: (i, 0))
    return pl.pallas_call(
        _fox_pre_kernel,
        grid=(t // tm,),
        in_specs=[tok(D_MODEL), _resident((1, D_MODEL)),
                  _resident((D_MODEL, 3 * D_MODEL)), _resident((D_MODEL, LANES)),
                  _resident((1, LANES)), _resident((1, LANES))],
        out_specs=[tok(D_MODEL), tok(D_MODEL),
                   pl.BlockSpec((D_MODEL, tm), lambda i: (0, i)), tok(LANES)],
        out_shape=[jax.ShapeDtypeStruct((t, D_MODEL), BF16),
                   jax.ShapeDtypeStruct((t, D_MODEL), BF16),
                   jax.ShapeDtypeStruct((D_MODEL, t), BF16),
                   jax.ShapeDtypeStruct((t, LANES), F32)],
        compiler_params=pltpu.CompilerParams(
            dimension_semantics=("parallel",), vmem_limit_bytes=VMEM_LIMIT),
        name="fox_pre",
    )(x2, g, wqkv, wf, gq2, gk2)


def _fox_decay_kernel(f_ref, b_ref, c_ref, *, blk):
    seq = f_ref.shape[0]
    lane = lax.broadcasted_iota(jnp.int32, (1, LANES), 1)
    r = lax.broadcasted_iota(jnp.int32, (blk, blk), 0)
    c = lax.broadcasted_iota(jnp.int32, (blk, blk), 1)
    lower = jnp.where(r >= c, 1.0, 0.0).astype(BF16)
    carry = jnp.zeros((1, LANES), F32)
    for b in range(seq // blk):
        rows = slice(b * blk, (b + 1) * blk)
        x = f_ref[rows, :] + b_ref[...]
        log_f = -(jnp.maximum(-x, 0.0) + jnp.log1p(jnp.exp(-jnp.abs(x))))
        cb = carry
        for term in _split_bf16(log_f):
            cb = cb + jnp.dot(lower, term, preferred_element_type=F32)
        carry = cb[blk - 1:blk, :]
        out = jnp.zeros((blk, LANES), BF16)
        for t, term in reversed(list(enumerate(_split_bf16(cb)))):
            out = jnp.where(lane < (t + 1) * N_HEADS, term, out)
        c_ref[rows, :] = out


def _fox_decay(f3, b3, bsz, seq):
    spec = pl.BlockSpec((seq, LANES), lambda b: (b, 0))
    return pl.pallas_call(
        functools.partial(_fox_decay_kernel, blk=MXU_N),
        grid=(bsz,),
        in_specs=[spec, _resident((1, LANES))],
        out_specs=spec,
        out_shape=jax.ShapeDtypeStruct(f3.shape, BF16),
        compiler_params=pltpu.CompilerParams(dimension_semantics=("parallel",)),
        name="fox_decay",
    )(f3, b3)


def _fox_attn_kernel(q_ref, k_ref, vt_ref, c_ref, o_ref, st_sc, *, tq, seq):
    p = pl.program_id(1)
    tk = 2 * tq
    lane = lax.broadcasted_iota(jnp.int32, (1, LANES), 1)
    own = [lane < HEAD_DIM, lane >= HEAD_DIM]
    neg = [jnp.broadcast_to(
        jnp.where((lane < N_SPLIT * N_HEADS)
                  & ((lane & (N_HEADS - 1)) == 2 * p + hh), -1.0, 0.0).astype(BF16),
        (tq, LANES)) for hh in range(2)]

    items = []
    for i in range(seq // tq):
        n_keys = (i + 1) * tq
        starts = list(range(0, n_keys, tk))
        for ks in starts:
            items.append((i, ks, min(tk, n_keys - ks), ks == starts[-1]))

    def query_blocks(i):
        q2 = q_ref[i * tq:(i + 1) * tq, :]
        return [jnp.concatenate(
            [jnp.where(own[hh], q2, jnp.zeros_like(q2)), neg[hh]], axis=1)
            for hh in range(2)]

    def scores(item, qa, slot):
        _, ks, width, _ = item
        ka = jnp.concatenate([k_ref[ks:ks + width, :], c_ref[ks:ks + width, :]],
                             axis=1)
        for hh in range(2):
            st_sc[slot, hh, 0:width, :] = lax.dot_general(
                ka, qa[hh], (((1,), (1,)), ((), ())), preferred_element_type=F32)

    def consume(item, slot, state):
        i, ks, width, is_last = item
        if is_last:
            krow = lax.broadcasted_iota(jnp.int32, (width, tq), 0)
            qcol = lax.broadcasted_iota(jnp.int32, (width, tq), 1)
            visible = krow <= qcol + (i * tq - ks)
        new = []
        for hh in range(2):
            m_old, l_old, acc_old = state[hh]
            st = st_sc[slot, hh, 0:width, :]
            if is_last:
                st = jnp.where(visible, st, NEG_INF)
            m_new = jnp.maximum(m_old, _reduce_rows(st, jnp.maximum))
            alpha = jnp.exp(m_old - m_new)
            pt = jnp.exp(st - m_new)
            l_new = alpha * l_old + _reduce_rows(pt, jnp.add)
            vt = vt_ref[hh * HEAD_DIM:(hh + 1) * HEAD_DIM, ks:ks + width]
            acc_new = alpha * acc_old + jnp.dot(
                vt, pt.astype(BF16), preferred_element_type=F32)
            new.append((m_new, l_new, acc_new))
        return new

    fresh = [(jnp.full((1, tq), NEG_INF, F32), jnp.zeros((1, tq), F32),
              jnp.zeros((HEAD_DIM, tq), F32))] * 2
    qa = query_blocks(0)
    scores(items[0], qa, 0)
    state = fresh
    for t, item in enumerate(items):
        if t + 1 < len(items):
            if items[t + 1][0] != item[0]:
                qa = query_blocks(items[t + 1][0])
            scores(items[t + 1], qa, (t + 1) % 2)
        state = consume(item, t % 2, state)
        if item[3]:
            i = item[0]
            ot = jnp.concatenate([acc / l for _, l, acc in state], axis=0)
            o_ref[i * tq:(i + 1) * tq, :] = ot.T.astype(BF16)
            state = fresh


def _fox_attn(q, k, vt, caug, bsz, seq, tq):
    tok = pl.BlockSpec((seq, LANES), lambda b, p: (b, p))
    return pl.pallas_call(
        functools.partial(_fox_attn_kernel, tq=tq, seq=seq),
        grid=(bsz, N_PAIRS),
        in_specs=[tok, tok,
                  pl.BlockSpec((LANES, seq), lambda b, p: (p, b)),
                  pl.BlockSpec((seq, LANES), lambda b, p: (b, 0))],
        out_specs=tok,
        out_shape=jax.ShapeDtypeStruct(q.shape, BF16),
        scratch_shapes=[pltpu.VMEM((2, 2, 2 * tq, tq), F32)],
        compiler_params=pltpu.CompilerParams(
            dimension_semantics=("parallel", "parallel")),
        name="fox_attn",
    )(q, k, vt, caug)


def _post_mlp_kernel(*refs, emit_next, ff_chunk):
    if emit_next:
        (h_ref, o_ref, wo_ref, gm_ref, wup_ref, wdn_ref,
         ga_ref, gkv_ref, wq_ref, wkv_ref, gq_ref, gk_ref,
         out_ref, qb_ref, kd_ref, vt_ref) = refs
    else:
        h_ref, o_ref, wo_ref, gm_ref, wup_ref, wdn_ref, out_ref = refs
    h1 = h_ref[...] + jnp.dot(o_ref[...], wo_ref[...], preferred_element_type=F32)
    m = ((h1 * _rms_scale(h1)) * gm_ref[...]).astype(BF16)
    acc = h1
    for c in range(D_FF // ff_chunk):
        u = jnp.dot(m, wup_ref[:, c * ff_chunk:(c + 1) * ff_chunk],
                    preferred_element_type=F32)
        u = jnp.square(jnp.maximum(u, 0.0)).astype(BF16)
        acc = acc + jnp.dot(u, wdn_ref[c * ff_chunk:(c + 1) * ff_chunk, :],
                            preferred_element_type=F32)
    out_ref[...] = acc
    if emit_next:
        hn = acc * _rms_scale(acc)
        a = (hn * ga_ref[...]).astype(BF16)
        for c in range(D_MODEL // MXU_N):
            y = jnp.dot(a, wq_ref[:, c * MXU_N:(c + 1) * MXU_N],
                        preferred_element_type=F32)
            for half in range(MXU_N // LANES):
                yy = y[:, half * LANES:(half + 1) * LANES]
                dst = pl.ds(c * MXU_N + half * LANES, LANES)
                qb_ref[:, dst] = (_head_pair_norm(yy, gq_ref[...]) * QK_SCALE).astype(BF16)
        kvn = (hn * gkv_ref[...]).astype(BF16)
        kv = jnp.dot(kvn, wkv_ref[...], preferred_element_type=F32)
        for j in range(KV_HEADS):
            sl = slice(j * LANES, (j + 1) * LANES)
            kd_ref[:, sl] = _head_pair_norm(kv[:, sl], gk_ref[...]).astype(BF16)
        vt_ref[...] = kv[:, KV_HEADS * LANES:].T.astype(BF16)


def _post_mlp(h, o, wo, gm, wup, wdn, nxt, tm, ff_chunk=1024):
    t = h.shape[0]
    tok = lambda w: pl.BlockSpec((tm, w), lambda i: (i, 0))
    in_specs = [tok(D_MODEL), tok(D_MODEL), _resident((D_MODEL, D_MODEL)),
                _resident((1, D_MODEL)), _resident((D_MODEL, D_FF)),
                _resident((D_FF, D_MODEL))]
    out_specs = [tok(D_MODEL)]
    out_shape = [jax.ShapeDtypeStruct((t, D_MODEL), F32)]
    args = [h, o, wo, gm, wup, wdn]
    if nxt is not None:
        kvw = KV_HEADS * LANES + KV_HEADS * HEAD_DIM
        in_specs += [_resident((1, D_MODEL)), _resident((1, D_MODEL)),
                     _resident((D_MODEL, D_MODEL)), _resident((D_MODEL, kvw)),
                     _resident((1, LANES)), _resident((1, LANES))]
        out_specs += [tok(D_MODEL), tok(KV_HEADS * LANES),
                      pl.BlockSpec((KV_HEADS * HEAD_DIM, tm), lambda i: (0, i))]
        out_shape += [jax.ShapeDtypeStruct((t, D_MODEL), BF16),
                      jax.ShapeDtypeStruct((t, KV_HEADS * LANES), BF16),
                      jax.ShapeDtypeStruct((KV_HEADS * HEAD_DIM, t), BF16)]
        args += list(nxt)
    return pl.pallas_call(
        functools.partial(_post_mlp_kernel, emit_next=nxt is not None,
                          ff_chunk=ff_chunk),
        grid=(t // tm,),
        in_specs=in_specs, out_specs=out_specs, out_shape=out_shape,
        compiler_params=pltpu.CompilerParams(
            dimension_semantics=("parallel",), vmem_limit_bytes=VMEM_LIMIT),
        name="post_mlp_next" if nxt is not None else "post_mlp",
    )(*args)


def _t5_causal_bucket(dist):
    n = np.maximum(dist, 0)
    max_exact = N_BUCKETS // 2
    large = max_exact + (np.log(np.maximum(n, 1) / max_exact)
                         / np.log(REL_MAX_DIST / max_exact)
                         * (N_BUCKETS - max_exact)).astype(np.int32)
    large = np.minimum(large, N_BUCKETS - 1)
    return np.where(n < max_exact, n, large).astype(np.int32)


def _bucket_ranges():
    buckets = _t5_causal_bucket(np.arange(WINDOW))
    assert np.all(np.diff(buckets) >= 0)
    out = []
    for k in np.unique(buckets):
        idx = np.nonzero(buckets == k)[0]
        out.append((int(k), int(idx[0]), int(idx[-1]) + 1))
    return out


def _swa_attn_kernel(sink_ref, rb_ref, q_ref, kp_ref, kc_ref, vtp_ref, vtc_ref,
                     o_ref, bias_sc, st_sc, e_sc):
    n = pl.program_id(1)
    band = 2 * WINDOW

    @pl.when((pl.program_id(0) == 0) & (n == 0))
    def _():
        krow = lax.broadcasted_iota(jnp.int32, (band, WINDOW), 0)
        qcol = lax.broadcasted_iota(jnp.int32, (band, WINDOW), 1)
        dist = qcol + WINDOW - krow
        for h in range(N_HEADS):
            tile = jnp.full((band, WINDOW), NEG_INF, F32)
            for k, lo, hi in _bucket_ranges():
                tile = jnp.where((dist >= lo) & (dist < hi), rb_ref[k, h], tile)
            cols = slice((h % 2) * WINDOW, (h % 2 + 1) * WINDOW)
            bias_sc[0, h // 2, :, cols] = tile
            bias_sc[1, h // 2, :, cols] = jnp.where(krow < WINDOW, NEG_INF, tile)

    first = (n == 0).astype(jnp.int32)
    kband = jnp.concatenate([kp_ref[...], kc_ref[...]], axis=0)
    vtband = jnp.concatenate([vtp_ref[...], vtc_ref[...]], axis=1)
    lane = lax.broadcasted_iota(jnp.int32, (1, LANES), 1)
    col = lax.broadcasted_iota(jnp.int32, (1, band), 1)
    pairs_per_kv = N_PAIRS // KV_HEADS

    def scores(p, slot):
        g = p // pairs_per_kv
        q2 = q_ref[:, p * LANES:(p + 1) * LANES]
        zero = jnp.zeros_like(q2)
        qs = jnp.concatenate([jnp.where(lane < HEAD_DIM, q2, zero),
                              jnp.where(lane >= HEAD_DIM, q2, zero)], axis=0)
        st_sc[slot] = lax.dot_general(
            kband[:, g * LANES:(g + 1) * LANES], qs, (((1,), (1,)), ((), ())),
            preferred_element_type=F32)

    def softmax(p, slot):
        st = st_sc[slot] + bias_sc[first, p]
        sink = jnp.where(col < WINDOW, sink_ref[2 * p], sink_ref[2 * p + 1])
        m = jnp.maximum(_reduce_rows(st, jnp.maximum), sink)
        e = jnp.exp(st - m)
        e_sc[slot] = e.astype(BF16)
        return _reduce_rows(e, jnp.add) + jnp.exp(sink - m)

    def values(p, slot, denom):
        g = p // pairs_per_kv
        ot = jnp.dot(vtband[g * HEAD_DIM:(g + 1) * HEAD_DIM, :], e_sc[slot],
                     preferred_element_type=F32) / denom
        o2 = jnp.concatenate([ot[:, :WINDOW], ot[:, WINDOW:]], axis=0).T
        o_ref[:, p * LANES:(p + 1) * LANES] = o2.astype(BF16)

    scores(0, 0)
    scores(1, 1)
    denom = softmax(0, 0)
    for p in range(N_PAIRS):
        if p + 2 < N_PAIRS:
            scores(p + 2, p % 2)
        values(p, p % 2, denom)
        if p + 1 < N_PAIRS:
            denom = softmax(p + 1, (p + 1) % 2)


def _swa_attn(sinks, rel_bias, q, kd, vt, bsz, seq):
    nblk = seq // WINDOW
    kvw = KV_HEADS * LANES
    vtw = KV_HEADS * HEAD_DIM
    cur = lambda b, n: (b * nblk + n, 0)
    prev = lambda b, n: (b * nblk + jnp.maximum(n - 1, 0), 0)
    cur_t = lambda b, n: (0, b * nblk + n)
    prev_t = lambda b, n: (0, b * nblk + jnp.maximum(n - 1, 0))
    smem = pl.BlockSpec(memory_space=pltpu.SMEM)
    return pl.pallas_call(
        _swa_attn_kernel,
        grid=(bsz, nblk),
        in_specs=[
            smem, smem,
            pl.BlockSpec((WINDOW, D_MODEL), cur),
            pl.BlockSpec((WINDOW, kvw), prev), pl.BlockSpec((WINDOW, kvw), cur),
            pl.BlockSpec((vtw, WINDOW), prev_t), pl.BlockSpec((vtw, WINDOW), cur_t),
        ],
        out_specs=pl.BlockSpec((WINDOW, D_MODEL), cur),
        out_shape=jax.ShapeDtypeStruct(q.shape, BF16),
        scratch_shapes=[pltpu.VMEM((2, N_PAIRS, 2 * WINDOW, 2 * WINDOW), F32),
                        pltpu.VMEM((2, 2 * WINDOW, 2 * WINDOW), F32),
                        pltpu.VMEM((2, 2 * WINDOW, 2 * WINDOW), BF16)],
        compiler_params=pltpu.CompilerParams(
            dimension_semantics=("arbitrary", "arbitrary")),
        name="swa_attn",
    )(sinks, rel_bias, q, kd, kd, vt, vt)


def _dup_heads(w):
    d = w.shape[0]
    w = w.reshape(d, KV_HEADS, 1, HEAD_DIM)
    return jnp.broadcast_to(w, (d, KV_HEADS, 2, HEAD_DIM)).reshape(d, KV_HEADS * LANES)


def _lane_groups(v):
    return jnp.pad(jnp.tile(v, N_SPLIT), (0, LANES - N_SPLIT * N_HEADS)).reshape(1, LANES)


def kernel(x, g_attn, g_mlp, w_in_a, b_f, gq_a, gk_a, w_out_a, g_kv, w_kv, gk_b,
           w_q_b, gq_b, sinks, rel_bias, w_out_b, w_up, w_down):
    bsz, seq, d = x.shape
    t = bsz * seq
    hw = N_HEADS * HEAD_DIM
    kvw = KV_HEADS * HEAD_DIM
    row = lambda g: g.reshape(1, -1).astype(F32)
    pair = lambda g: jnp.tile(g.astype(F32), 2).reshape(1, LANES)

    x2 = x.reshape(t, d)
    w_in = w_in_a[0]
    wqkv = w_in[:, :3 * hw].astype(BF16)
    wf = jnp.pad(jnp.tile(w_in[:, 3 * hw:], (1, N_SPLIT)),
                 ((0, 0), (0, LANES - N_SPLIT * N_HEADS))).astype(BF16)

    q, k, vt, f3 = _fox_pre(x2, row(g_attn[0]), wqkv, wf, pair(gq_a[0]),
                            pair(gk_a[0]), tm=512)
    caug = _fox_decay(f3, _lane_groups(b_f[0].astype(F32)), bsz, seq)
    o = _fox_attn(q, k, vt, caug, bsz, seq, tq=256)

    wkv = jnp.concatenate([_dup_heads(w_kv[:, :kvw]), w_kv[:, kvw:]],
                          axis=1).astype(BF16)
    nxt = (row(g_attn[1]), row(g_kv), w_q_b[0].astype(BF16), wkv,
           pair(gq_b[0]), pair(gk_b))
    h, qb, kd, vt_b = _post_mlp(x2, o, w_out_a[0].astype(BF16), row(g_mlp[0]),
                                w_up[0].astype(BF16), w_down[0].astype(BF16),
                                nxt, tm=512)

    o2 = _swa_attn(sinks[0].astype(F32), rel_bias.astype(F32), qb, kd, vt_b,
                   bsz, seq)
    (out,) = _post_mlp(h, o2, w_out_b[0].astype(BF16), row(g_mlp[1]),
                       w_up[1].astype(BF16), w_down[1].astype(BF16), None, tm=512)
    return out.reshape(bsz, seq, d)
```

```python
import functools

import numpy as np
import jax
import jax.numpy as jnp
from jax import lax
from jax.experimental import pallas as pl
from jax.experimental.pallas import tpu as pltpu

D_MODEL = 1024
HEAD_DIM = 64
N_HEADS = 16
N_PAIRS = N_HEADS // 2
KV_HEADS = 2
WINDOW = 128
D_FF = 4 * D_MODEL
N_BUCKETS = 32
REL_MAX_DIST = 128
NORM_EPS = 1e-6
LANES = 128
MXU_N = 256
QK_SCALE = HEAD_DIM ** -0.5
N_SPLIT = 3
SUM_ROWS = 16
ROW_CHUNK = 32

F32 = jnp.float32
BF16 = jnp.bfloat16
NEG_INF = float("-inf")
VMEM_LIMIT = 56 * 1024 * 1024


def _rms_scale(x):
    return lax.rsqrt(jnp.mean(x * x, axis=-1, keepdims=True) + NORM_EPS)


def _head_pair_norm(y, g2):
    lo = lax.broadcasted_iota(jnp.int32, (1, LANES), 1) < HEAD_DIM
    sq = y * y
    s_lo = jnp.sum(jnp.where(lo, sq, 0.0), axis=-1, keepdims=True)
    s_hi = jnp.sum(jnp.where(lo, 0.0, sq), axis=-1, keepdims=True)
    ms = jnp.where(lo, s_lo, s_hi) * (1.0 / HEAD_DIM)
    return (y * lax.rsqrt(ms + NORM_EPS)) * g2


def _split_bf16(x):
    terms = []
    for _ in range(N_SPLIT):
        t = x.astype(BF16)
        terms.append(t)
        x = x - t.astype(F32)
    return terms


def _reduce_rows(x, op, group=32):
    acc = x[0:group]
    for r in range(group, x.shape[0], group):
        acc = op(acc, x[r:r + group])
    red = jnp.max if op is jnp.maximum else jnp.sum
    return red(acc, axis=0, keepdims=True)


def _resident(shape):
    zeros = (0,) * len(shape)
    return pl.BlockSpec(shape, lambda *_: zeros, pipeline_mode=pl.Buffered(1))


def _fox_pre_kernel(x_ref, g_ref, wqkv_ref, wf_ref, gq_ref, gk_ref,
                    qt_ref, k_ref, vt_ref, f_ref):
    x = x_ref[...]
    hn = ((x * _rms_scale(x)) * g_ref[...]).astype(BF16)
    n_chunks = D_MODEL // MXU_N
    for c in range(3 * n_chunks):
        y = jnp.dot(hn, wqkv_ref[:, c * MXU_N:(c + 1) * MXU_N],
                    preferred_element_type=F32)
        part, cc = divmod(c, n_chunks)
        rows = slice(cc * MXU_N, (cc + 1) * MXU_N)
        if part == 2:
            vt_ref[rows, :] = y.T.astype(BF16)
            continue
        g2 = gq_ref[...] if part == 0 else gk_ref[...]
        yn = jnp.concatenate(
            [_head_pair_norm(y[:, h * LANES:(h + 1) * LANES], g2)
             for h in range(MXU_N // LANES)], axis=1)
        if part == 0:
            qt_ref[rows, :] = (yn * QK_SCALE).T.astype(BF16)
        else:
            k_ref[:, rows] = yn.astype(BF16)
    f_ref[...] = jnp.dot(hn, wf_ref[...], preferred_element_type=F32)


def _fox_pre(x2, g, wqkv, wf, gq2, gk2, tm):
    t = x2.shape[0]
    tok = lambda w: pl.BlockSpec((tm, w), lambda i: (i, 0))
    return pl.pallas_call(
        _fox_pre_kernel,
        grid=(t // tm,),
        in_specs=[tok(D_MODEL), _resident((1, D_MODEL)),
                  _resident((D_MODEL, 3 * D_MODEL)), _resident((D_MODEL, LANES)),
                  _resident((1, LANES)), _resident((1, LANES))],
        out_specs=[pl.BlockSpec((D_MODEL, tm), lambda i: (0, i)), tok(D_MODEL),
                   pl.BlockSpec((D_MODEL, tm), lambda i: (0, i)), tok(LANES)],
        out_shape=[jax.ShapeDtypeStruct((D_MODEL, t), BF16),
                   jax.ShapeDtypeStruct((t, D_MODEL), BF16),
                   jax.ShapeDtypeStruct((D_MODEL, t), BF16),
                   jax.ShapeDtypeStruct((t, LANES), F32)],
        compiler_params=pltpu.CompilerParams(
            dimension_semantics=("parallel",), vmem_limit_bytes=VMEM_LIMIT),
        name="fox_pre",
    )(x2, g, wqkv, wf, gq2, gk2)


def _fox_decay_kernel(f_ref, b_ref, c_ref, *, blk):
    seq = f_ref.shape[0]
    lane = lax.broadcasted_iota(jnp.int32, (1, LANES), 1)
    r = lax.broadcasted_iota(jnp.int32, (blk, blk), 0)
    c = lax.broadcasted_iota(jnp.int32, (blk, blk), 1)
    lower = jnp.where(r >= c, 1.0, 0.0).astype(BF16)
    carry = jnp.zeros((1, LANES), F32)
    for b in range(seq // blk):
        rows = slice(b * blk, (b + 1) * blk)
        x = f_ref[rows, :] + b_ref[...]
        log_f = -(jnp.maximum(-x, 0.0) + jnp.log1p(jnp.exp(-jnp.abs(x))))
        cb = carry
        for term in _split_bf16(log_f):
            cb = cb + jnp.dot(lower, term, preferred_element_type=F32)
        carry = cb[blk - 1:blk, :]
        out = jnp.zeros((blk, LANES), BF16)
        for t, term in reversed(list(enumerate(_split_bf16(cb)))):
            out = jnp.where(lane < (t + 1) * N_HEADS, term, out)
        c_ref[rows, :] = out


def _fox_decay(f3, b3, bsz, seq):
    spec = pl.BlockSpec((seq, LANES), lambda b: (b, 0))
    return pl.pallas_call(
        functools.partial(_fox_decay_kernel, blk=MXU_N),
        grid=(bsz,),
        in_specs=[spec, _resident((1, LANES))],
        out_specs=spec,
        out_shape=jax.ShapeDtypeStruct(f3.shape, BF16),
        compiler_params=pltpu.CompilerParams(dimension_semantics=("parallel",)),
        name="fox_decay",
    )(f3, b3)


def _fox_attn_kernel(qt_ref, k_ref, vt_ref, c_ref, o_ref, st_sc, pt_sc, *, tq, seq):
    p = pl.program_id(1)
    tk = 2 * tq
    feat = lax.broadcasted_iota(jnp.int32, (LANES, 1), 0)
    own = [feat < HEAD_DIM, feat >= HEAD_DIM]
    neg = [jnp.broadcast_to(
        jnp.where((feat < N_SPLIT * N_HEADS)
                  & ((feat & (N_HEADS - 1)) == 2 * p + hh), -1.0, 0.0).astype(BF16),
        (LANES, tq)) for hh in range(2)]

    items = []
    for i in range(seq // tq):
        n_keys = (i + 1) * tq
        starts = list(range(0, n_keys, tk))
        for ks in starts:
            items.append((i, ks, min(tk, n_keys - ks), ks == starts[-1]))

    def query_blocks(i):
        q2 = qt_ref[:, i * tq:(i + 1) * tq]
        return [jnp.concatenate(
            [jnp.where(own[hh], q2, jnp.zeros_like(q2)), neg[hh]], axis=0)
            for hh in range(2)]

    def scores(item, qa, slot):
        _, ks, width, _ = item
        ka = jnp.concatenate([k_ref[ks:ks + width, :], c_ref[ks:ks + width, :]],
                             axis=1)

        def one(hh):
            st_sc[slot, hh, 0:width, :] = jnp.dot(
                ka, qa[hh], preferred_element_type=F32)
        return [functools.partial(one, hh) for hh in range(2)]

    def probabilities(item, t, m_old, res):
        i, ks, width, is_last = item
        chunks = list(range(0, width, ROW_CHUNK))
        if is_last:
            krow = lax.broadcasted_iota(jnp.int32, (ROW_CHUNK, tq), 0)
            qcol = lax.broadcasted_iota(jnp.int32, (ROW_CHUNK, tq), 1)
            diag = qcol - krow + (i * tq - ks)
        pieces = []
        for hh in range(2):
            src = st_sc.at[t % 3, hh]

            def col_max(hh=hh, src=src):
                acc = None
                for r in chunks:
                    x = src[r:r + ROW_CHUNK, :]
                    if is_last:
                        x = jnp.where(diag >= r, x, NEG_INF)
                        src[r:r + ROW_CHUNK, :] = x
                    acc = x if acc is None else jnp.maximum(acc, x)
                m_new = jnp.maximum(m_old[hh], jnp.max(acc, axis=0, keepdims=True))
                res[hh] = (m_new, jnp.exp(m_old[hh] - m_new))

            def exps(rows, hh=hh, src=src):
                for r in rows:
                    pt_sc[t % 2, hh, r:r + ROW_CHUNK, :] = jnp.exp(
                        src[r:r + ROW_CHUNK, :] - res[hh][0]).astype(BF16)

            half = len(chunks) // 2
            pieces += [col_max, functools.partial(exps, chunks[:half]),
                       functools.partial(exps, chunks[half:])]
        return pieces

    def values(item, t, alpha, acc_old, new):
        _, ks, width, _ = item

        def one(hh):
            vt = jnp.concatenate(
                [vt_ref[hh * HEAD_DIM:(hh + 1) * HEAD_DIM, ks:ks + width],
                 jnp.ones((SUM_ROWS, width), BF16)], axis=0)
            new[hh] = alpha[hh] * acc_old[hh] + jnp.dot(
                vt, pt_sc[t % 2, hh, 0:width, :],
                preferred_element_type=F32)
        return [functools.partial(one, hh) for hh in range(2)]

    n_items = len(items)
    m_fresh = [jnp.full((1, tq), NEG_INF, F32)] * 2
    acc_fresh = [jnp.zeros((HEAD_DIM + SUM_ROWS, tq), F32)] * 2
    alpha = {}
    qa_tile, qa = -1, None
    m_run, acc_run = m_fresh, acc_fresh
    for t in range(-2, n_items):
        matmuls, vector = [], []
        if t + 2 < n_items:
            nxt = items[t + 2]
            if nxt[0] != qa_tile:
                qa_tile, qa = nxt[0], query_blocks(nxt[0])
            matmuls += scores(nxt, qa, (t + 2) % 3)
        acc_new = [None, None]
        if t >= 0:
            matmuls += values(items[t], t, alpha.pop(t), acc_run, acc_new)
        res = [None, None]
        if 0 <= t + 1 < n_items:
            vector = probabilities(items[t + 1], t + 1, m_run, res)
        per = -(-len(vector) // max(len(matmuls), 1))
        for n, mm in enumerate(matmuls):
            mm()
            for piece in vector[n * per:(n + 1) * per]:
                piece()
        for piece in vector[len(matmuls) * per:]:
            piece()
        if t >= 0:
            acc_run = acc_new
            if items[t][3]:
                i = items[t][0]
                ot = jnp.concatenate(
                    [acc[:HEAD_DIM] / acc[HEAD_DIM:HEAD_DIM + 1] for acc in acc_run],
                    axis=0)
                o_ref[i * tq:(i + 1) * tq, :] = ot.T.astype(BF16)
                acc_run = acc_fresh
        if 0 <= t + 1 < n_items:
            m_run = m_fresh if items[t + 1][3] else [r[0] for r in res]
            alpha[t + 1] = [r[1] for r in res]


def _fox_attn(qt, k, vt, caug, bsz, seq, tq):
    tok = pl.BlockSpec((seq, LANES), lambda b, p: (b, p))
    feat = pl.BlockSpec((LANES, seq), lambda b, p: (p, b))
    return pl.pallas_call(
        functools.partial(_fox_attn_kernel, tq=tq, seq=seq),
        grid=(bsz, N_PAIRS),
        in_specs=[feat, tok, feat,
                  pl.BlockSpec((seq, LANES), lambda b, p: (b, 0))],
        out_specs=tok,
        out_shape=jax.ShapeDtypeStruct(k.shape, BF16),
        scratch_shapes=[pltpu.VMEM((3, 2, 2 * tq, tq), F32),
                        pltpu.VMEM((2, 2, 2 * tq, tq), BF16)],
        compiler_params=pltpu.CompilerParams(
            dimension_semantics=("parallel", "parallel")),
        name="fox_attn",
    )(qt, k, vt, caug)


def _post_mlp_kernel(*refs, emit_next, ff_chunk):
    if emit_next:
        (h_ref, o_ref, wo_ref, gm_ref, wup_ref, wdn_ref,
         ga_ref, gkv_ref, wq_ref, wkv_ref, gq_ref, gk_ref,
         out_ref, qb_ref, kd_ref, vt_ref) = refs
    else:
        h_ref, o_ref, wo_ref, gm_ref, wup_ref, wdn_ref, out_ref = refs
    h1 = h_ref[...] + jnp.dot(o_ref[...], wo_ref[...], preferred_element_type=F32)
    m = ((h1 * _rms_scale(h1)) * gm_ref[...]).astype(BF16)
    acc = h1
    for c in range(D_FF // ff_chunk):
        u = jnp.dot(m, wup_ref[:, c * ff_chunk:(c + 1) * ff_chunk],
                    preferred_element_type=F32)
        u = jnp.square(jnp.maximum(u, 0.0)).astype(BF16)
        acc = acc + jnp.dot(u, wdn_ref[c * ff_chunk:(c + 1) * ff_chunk, :],
                            preferred_element_type=F32)
    out_ref[...] = acc
    if emit_next:
        hn = acc * _rms_scale(acc)
        a = (hn * ga_ref[...]).astype(BF16)
        for c in range(D_MODEL // MXU_N):
            y = jnp.dot(a, wq_ref[:, c * MXU_N:(c + 1) * MXU_N],
                        preferred_element_type=F32)
            for half in range(MXU_N // LANES):
                yy = y[:, half * LANES:(half + 1) * LANES]
                dst = pl.ds(c * MXU_N + half * LANES, LANES)
                qb_ref[:, dst] = (_head_pair_norm(yy, gq_ref[...]) * QK_SCALE).astype(BF16)
        kvn = (hn * gkv_ref[...]).astype(BF16)
        kv = jnp.dot(kvn, wkv_ref[...], preferred_element_type=F32)
        for j in range(KV_HEADS):
            sl = slice(j * LANES, (j + 1) * LANES)
            kd_ref[:, sl] = _head_pair_norm(kv[:, sl], gk_ref[...]).astype(BF16)
        vt_ref[...] = kv[:, KV_HEADS * LANES:].T.astype(BF16)


def _post_mlp(h, o, wo, gm, wup, wdn, nxt, tm, ff_chunk=1024):
    t = h.shape[0]
    tok = lambda w: pl.BlockSpec((tm, w), lambda i: (i, 0))
    in_specs = [tok(D_MODEL), tok(D_MODEL), _resident((D_MODEL, D_MODEL)),
                _resident((1, D_MODEL)), _resident((D_MODEL, D_FF)),
                _resident((D_FF, D_MODEL))]
    out_specs = [tok(D_MODEL)]
    out_shape = [jax.ShapeDtypeStruct((t, D_MODEL), F32)]
    args = [h, o, wo, gm, wup, wdn]
    if nxt is not None:
        kvw = KV_HEADS * LANES + KV_HEADS * HEAD_DIM
        in_specs += [_resident((1, D_MODEL)), _resident((1, D_MODEL)),
                     _resident((D_MODEL, D_MODEL)), _resident((D_MODEL, kvw)),
                     _resident((1, LANES)), _resident((1, LANES))]
        out_specs += [tok(D_MODEL), tok(KV_HEADS * LANES),
                      pl.BlockSpec((KV_HEADS * HEAD_DIM, tm), lambda i: (0, i))]
        out_shape += [jax.ShapeDtypeStruct((t, D_MODEL), BF16),
                      jax.ShapeDtypeStruct((t, KV_HEADS * LANES), BF16),
                      jax.ShapeDtypeStruct((KV_HEADS * HEAD_DIM, t), BF16)]
        args += list(nxt)
    return pl.pallas_call(
        functools.partial(_post_mlp_kernel, emit_next=nxt is not None,
                          ff_chunk=ff_chunk),
        grid=(t // tm,),
        in_specs=in_specs, out_specs=out_specs, out_shape=out_shape,
        compiler_params=pltpu.CompilerParams(
            dimension_semantics=("parallel",), vmem_limit_bytes=VMEM_LIMIT),
        name="post_mlp_next" if nxt is not None else "post_mlp",
    )(*args)


def _t5_causal_bucket(dist):
    n = np.maximum(dist, 0)
    max_exact = N_BUCKETS // 2
    large = max_exact + (np.log(np.maximum(n, 1) / max_exact)
                         / np.log(REL_MAX_DIST / max_exact)
                         * (N_BUCKETS - max_exact)).astype(np.int32)
    large = np.minimum(large, N_BUCKETS - 1)
    return np.where(n < max_exact, n, large).astype(np.int32)


def _bucket_ranges():
    buckets = _t5_causal_bucket(np.arange(WINDOW))
    assert np.all(np.diff(buckets) >= 0)
    out = []
    for k in np.unique(buckets):
        idx = np.nonzero(buckets == k)[0]
        out.append((int(k), int(idx[0]), int(idx[-1]) + 1))
    return out


def _swa_attn_kernel(sink_ref, rb_ref, q_ref, kp_ref, kc_ref, vtp_ref, vtc_ref,
                     o_ref, bias_sc, st_sc, e_sc):
    n = pl.program_id(1)
    band = 2 * WINDOW

    @pl.when((pl.program_id(0) == 0) & (n == 0))
    def _():
        krow = lax.broadcasted_iota(jnp.int32, (band, WINDOW), 0)
        qcol = lax.broadcasted_iota(jnp.int32, (band, WINDOW), 1)
        dist = qcol + WINDOW - krow
        for h in range(N_HEADS):
            tile = jnp.full((band, WINDOW), NEG_INF, F32)
            for k, lo, hi in _bucket_ranges():
                tile = jnp.where((dist >= lo) & (dist < hi), rb_ref[k, h], tile)
            cols = slice((h % 2) * WINDOW, (h % 2 + 1) * WINDOW)
            bias_sc[0, h // 2, :, cols] = tile
            bias_sc[1, h // 2, :, cols] = jnp.where(krow < WINDOW, NEG_INF, tile)

    first = (n == 0).astype(jnp.int32)
    kband = jnp.concatenate([kp_ref[...], kc_ref[...]], axis=0)
    vtband = jnp.concatenate([vtp_ref[...], vtc_ref[...]], axis=1)
    lane = lax.broadcasted_iota(jnp.int32, (1, LANES), 1)
    col = lax.broadcasted_iota(jnp.int32, (1, band), 1)
    pairs_per_kv = N_PAIRS // KV_HEADS

    def scores(p, slot):
        g = p // pairs_per_kv
        q2 = q_ref[:, p * LANES:(p + 1) * LANES]
        zero = jnp.zeros_like(q2)
        qs = jnp.concatenate([jnp.where(lane < HEAD_DIM, q2, zero),
                              jnp.where(lane >= HEAD_DIM, q2, zero)], axis=0)
        st_sc[slot] = lax.dot_general(
            kband[:, g * LANES:(g + 1) * LANES], qs, (((1,), (1,)), ((), ())),
            preferred_element_type=F32)

    def softmax(p, slot):
        st = st_sc[slot] + bias_sc[first, p]
        sink = jnp.where(col < WINDOW, sink_ref[2 * p], sink_ref[2 * p + 1])
        m = jnp.maximum(_reduce_rows(st, jnp.maximum), sink)
        e = jnp.exp(st - m)
        e_sc[slot] = e.astype(BF16)
        return _reduce_rows(e, jnp.add) + jnp.exp(sink - m)

    def values(p, slot, denom):
        g = p // pairs_per_kv
        ot = jnp.dot(vtband[g * HEAD_DIM:(g + 1) * HEAD_DIM, :], e_sc[slot],
                     preferred_element_type=F32) / denom
        o2 = jnp.concatenate([ot[:, :WINDOW], ot[:, WINDOW:]], axis=0).T
        o_ref[:, p * LANES:(p + 1) * LANES] = o2.astype(BF16)

    scores(0, 0)
    scores(1, 1)
    denom = softmax(0, 0)
    for p in range(N_PAIRS):
        if p + 2 < N_PAIRS:
            scores(p + 2, p % 2)
        values(p, p % 2, denom)
        if p + 1 < N_PAIRS:
            denom = softmax(p + 1, (p + 1) % 2)


def _swa_attn(sinks, rel_bias, q, kd, vt, bsz, seq):
    nblk = seq // WINDOW
    kvw = KV_HEADS * LANES
    vtw = KV_HEADS * HEAD_DIM
    cur = lambda b, n: (b * nblk + n, 0)
    prev = lambda b, n: (b * nblk + jnp.maximum(n - 1, 0), 0)
    cur_t = lambda b, n: (0, b * nblk + n)
    prev_t = lambda b, n: (0, b * nblk + jnp.maximum(n - 1, 0))
    smem = pl.BlockSpec(memory_space=pltpu.SMEM)
    return pl.pallas_call(
        _swa_attn_kernel,
        grid=(bsz, nblk),
        in_specs=[
            smem, smem,
            pl.BlockSpec((WINDOW, D_MODEL), cur),
            pl.BlockSpec((WINDOW, kvw), prev), pl.BlockSpec((WINDOW, kvw), cur),
            pl.BlockSpec((vtw, WINDOW), prev_t), pl.BlockSpec((vtw, WINDOW), cur_t),
        ],
        out_specs=pl.BlockSpec((WINDOW, D_MODEL), cur),
        out_shape=jax.ShapeDtypeStruct(q.shape, BF16),
        scratch_shapes=[pltpu.VMEM((2, N_PAIRS, 2 * WINDOW, 2 * WINDOW), F32),
                        pltpu.VMEM((2, 2 * WINDOW, 2 * WINDOW), F32),
                        pltpu.VMEM((2, 2 * WINDOW, 2 * WINDOW), BF16)],
        compiler_params=pltpu.CompilerParams(
            dimension_semantics=("arbitrary", "arbitrary")),
        name="swa_attn",
    )(sinks, rel_bias, q, kd, kd, vt, vt)


def _dup_heads(w):
    d = w.shape[0]
    w = w.reshape(d, KV_HEADS, 1, HEAD_DIM)
    return jnp.broadcast_to(w, (d, KV_HEADS, 2, HEAD_DIM)).reshape(d, KV_HEADS * LANES)


def _lane_groups(v):
    return jnp.pad(jnp.tile(v, N_SPLIT), (0, LANES - N_SPLIT * N_HEADS)).reshape(1, LANES)


def kernel(x, g_attn, g_mlp, w_in_a, b_f, gq_a, gk_a, w_out_a, g_kv, w_kv, gk_b,
           w_q_b, gq_b, sinks, rel_bias, w_out_b, w_up, w_down):
    bsz, seq, d = x.shape
    t = bsz * seq
    hw = N_HEADS * HEAD_DIM
    kvw = KV_HEADS * HEAD_DIM
    row = lambda g: g.reshape(1, -1).astype(F32)
    pair = lambda g: jnp.tile(g.astype(F32), 2).reshape(1, LANES)

    x2 = x.reshape(t, d)
    w_in = w_in_a[0]
    wqkv = w_in[:, :3 * hw].astype(BF16)
    wf = jnp.pad(jnp.tile(w_in[:, 3 * hw:], (1, N_SPLIT)),
                 ((0, 0), (0, LANES - N_SPLIT * N_HEADS))).astype(BF16)

    qt, k, vt, f3 = _fox_pre(x2, row(g_attn[0]), wqkv, wf, pair(gq_a[0]),
                            pair(gk_a[0]), tm=512)
    caug = _fox_decay(f3, _lane_groups(b_f[0].astype(F32)), bsz, seq)
    o = _fox_attn(qt, k, vt, caug, bsz, seq, tq=256)

    wkv = jnp.concatenate([_dup_heads(w_kv[:, :kvw]), w_kv[:, kvw:]],
                          axis=1).astype(BF16)
    nxt = (row(g_attn[1]), row(g_kv), w_q_b[0].astype(BF16), wkv,
           pair(gq_b[0]), pair(gk_b))
    h, qb, kd, vt_b = _post_mlp(x2, o, w_out_a[0].astype(BF16), row(g_mlp[0]),
                                w_up[0].astype(BF16), w_down[0].astype(BF16),
                                nxt, tm=512)

    o2 = _swa_attn(sinks[0].astype(F32), rel_bias.astype(F32), qb, kd, vt_b,
                   bsz, seq)
    (out,) = _post_mlp(h, o2, w_out_b[0].astype(BF16), row(g_mlp[1]),
                       w_up[1].astype(BF16), w_down[1].astype(BF16), None, tm=512)
    return out.reshape(bsz, seq, d)
```

```python
import functools

import numpy as np
import jax
import jax.numpy as jnp
from jax import lax
from jax.experimental import pallas as pl
from jax.experimental.pallas import tpu as pltpu

D_MODEL = 1024
HEAD_DIM = 64
N_HEADS = 16
N_PAIRS = N_HEADS // 2
KV_HEADS = 2
WINDOW = 128
D_FF = 4 * D_MODEL
N_BUCKETS = 32
REL_MAX_DIST = 128
NORM_EPS = 1e-6
LANES = 128
MXU_N = 256
QK_SCALE = HEAD_DIM ** -0.5
N_SPLIT = 3
SUM_ROWS = 16
ROW_CHUNK = 32

F32 = jnp.float32
BF16 = jnp.bfloat16
NEG_INF = float("-inf")
VMEM_LIMIT = 56 * 1024 * 1024


def _rms_scale(x):
    return lax.rsqrt(jnp.mean(x * x, axis=-1, keepdims=True) + NORM_EPS)


def _head_pair_norm(y, g2):
    lo = lax.broadcasted_iota(jnp.int32, (1, LANES), 1) < HEAD_DIM
    sq = y * y
    s_lo = jnp.sum(jnp.where(lo, sq, 0.0), axis=-1, keepdims=True)
    s_hi = jnp.sum(jnp.where(lo, 0.0, sq), axis=-1, keepdims=True)
    ms = jnp.where(lo, s_lo, s_hi) * (1.0 / HEAD_DIM)
    return (y * lax.rsqrt(ms + NORM_EPS)) * g2


def _split_bf16(x):
    terms = []
    for _ in range(N_SPLIT):
        t = x.astype(BF16)
        terms.append(t)
        x = x - t.astype(F32)
    return terms


def _reduce_rows(x, op, group=32):
    acc = x[0:group]
    for r in range(group, x.shape[0], group):
        acc = op(acc, x[r:r + group])
    red = jnp.max if op is jnp.maximum else jnp.sum
    return red(acc, axis=0, keepdims=True)


def _resident(shape):
    zeros = (0,) * len(shape)
    return pl.BlockSpec(shape, lambda *_: zeros, pipeline_mode=pl.Buffered(1))


def _layer(shape, layer):
    index = (layer,) + (0,) * len(shape)
    return pl.BlockSpec((None,) + tuple(shape), lambda *_: index,
                        pipeline_mode=pl.Buffered(1))


def _fox_pre_kernel(x_ref, g_ref, wqkv_ref, wf_ref, gq_ref, gk_ref,
                    qt_ref, k_ref, vt_ref, f_ref):
    x = x_ref[...]
    hn = ((x * _rms_scale(x)) * g_ref[...]).astype(BF16)
    n_chunks = D_MODEL // MXU_N
    for c in range(3 * n_chunks):
        y = jnp.dot(hn, wqkv_ref[:, c * MXU_N:(c + 1) * MXU_N],
                    preferred_element_type=F32)
        part, cc = divmod(c, n_chunks)
        rows = slice(cc * MXU_N, (cc + 1) * MXU_N)
        if part == 2:
            vt_ref[rows, :] = y.T.astype(BF16)
            continue
        g2 = gq_ref[...] if part == 0 else gk_ref[...]
        yn = jnp.concatenate(
            [_head_pair_norm(y[:, h * LANES:(h + 1) * LANES], g2)
             for h in range(MXU_N // LANES)], axis=1)
        if part == 0:
            qt_ref[rows, :] = (yn * QK_SCALE).T.astype(BF16)
        else:
            k_ref[:, rows] = yn.astype(BF16)
    f_ref[...] = jnp.dot(hn, wf_ref[...], preferred_element_type=F32)


def _fox_pre(x2, g, w_in, wf, gq2, gk2, tm):
    t = x2.shape[0]
    tok = lambda w: pl.BlockSpec((tm, w), lambda i: (i, 0))
    return pl.pallas_call(
        _fox_pre_kernel,
        grid=(t // tm,),
        in_specs=[tok(D_MODEL), _resident((1, D_MODEL)),
                  _layer(w_in.shape[1:], 0), _resident((D_MODEL, LANES)),
                  _resident((1, LANES)), _resident((1, LANES))],
        out_specs=[pl.BlockSpec((D_MODEL, tm), lambda i: (0, i)), tok(D_MODEL),
                   pl.BlockSpec((D_MODEL, tm), lambda i: (0, i)), tok(LANES)],
        out_shape=[jax.ShapeDtypeStruct((D_MODEL, t), BF16),
                   jax.ShapeDtypeStruct((t, D_MODEL), BF16),
                   jax.ShapeDtypeStruct((D_MODEL, t), BF16),
                   jax.ShapeDtypeStruct((t, LANES), F32)],
        compiler_params=pltpu.CompilerParams(
            dimension_semantics=("parallel",), vmem_limit_bytes=VMEM_LIMIT),
        name="fox_pre",
    )(x2, g, w_in, wf, gq2, gk2)


def _fox_decay_kernel(f_ref, b_ref, c_ref, *, blk):
    seq = f_ref.shape[0]
    lane = lax.broadcasted_iota(jnp.int32, (1, LANES), 1)
    r = lax.broadcasted_iota(jnp.int32, (blk, blk), 0)
    c = lax.broadcasted_iota(jnp.int32, (blk, blk), 1)
    lower = jnp.where(r >= c, 1.0, 0.0).astype(BF16)
    carry = jnp.zeros((1, LANES), F32)
    for b in range(seq // blk):
        rows = slice(b * blk, (b + 1) * blk)
        x = f_ref[rows, :] + b_ref[...]
        log_f = -(jnp.maximum(-x, 0.0) + jnp.log1p(jnp.exp(-jnp.abs(x))))
        cb = carry
        for term in _split_bf16(log_f):
            cb = cb + jnp.dot(lower, term, preferred_element_type=F32)
        carry = cb[blk - 1:blk, :]
        out = jnp.zeros((blk, LANES), BF16)
        for t, term in reversed(list(enumerate(_split_bf16(cb)))):
            out = jnp.where(lane < (t + 1) * N_HEADS, term, out)
        c_ref[rows, :] = out


def _fox_decay(f3, b3, bsz, seq):
    spec = pl.BlockSpec((seq, LANES), lambda b: (b, 0))
    return pl.pallas_call(
        functools.partial(_fox_decay_kernel, blk=MXU_N),
        grid=(bsz,),
        in_specs=[spec, _resident((1, LANES))],
        out_specs=spec,
        out_shape=jax.ShapeDtypeStruct(f3.shape, BF16),
        compiler_params=pltpu.CompilerParams(dimension_semantics=("parallel",)),
        name="fox_decay",
    )(f3, b3)


def _fox_attn_kernel(qt_ref, k_ref, vt_ref, c_ref, o_ref, st_sc, pt_sc, *, tq, seq):
    p = pl.program_id(1)
    tk = 2 * tq
    feat = lax.broadcasted_iota(jnp.int32, (LANES, 1), 0)
    own = [feat < HEAD_DIM, feat >= HEAD_DIM]
    neg = [jnp.broadcast_to(
        jnp.where((feat < N_SPLIT * N_HEADS)
                  & ((feat & (N_HEADS - 1)) == 2 * p + hh), -1.0, 0.0).astype(BF16),
        (LANES, tq)) for hh in range(2)]

    items = []
    for i in range(seq // tq):
        n_keys = (i + 1) * tq
        starts = list(range(0, n_keys, tk))
        for ks in starts:
            items.append((i, ks, min(tk, n_keys - ks), ks == starts[-1]))

    def query_blocks(i):
        q2 = qt_ref[:, i * tq:(i + 1) * tq]
        return [jnp.concatenate(
            [jnp.where(own[hh], q2, jnp.zeros_like(q2)), neg[hh]], axis=0)
            for hh in range(2)]

    def scores(item, qa, slot):
        _, ks, width, _ = item
        ka = jnp.concatenate([k_ref[ks:ks + width, :], c_ref[ks:ks + width, :]],
                             axis=1)

        def one(hh):
            st_sc[slot, hh, 0:width, :] = jnp.dot(
                ka, qa[hh], preferred_element_type=F32)
        return [functools.partial(one, hh) for hh in range(2)]

    def probabilities(item, t, m_old, res):
        i, ks, width, is_last = item
        chunks = list(range(0, width, ROW_CHUNK))
        if is_last:
            krow = lax.broadcasted_iota(jnp.int32, (ROW_CHUNK, tq), 0)
            qcol = lax.broadcasted_iota(jnp.int32, (ROW_CHUNK, tq), 1)
            diag = qcol - krow + (i * tq - ks)
        pieces = []
        for hh in range(2):
            src = st_sc.at[t % 3, hh]

            def col_max(hh=hh, src=src):
                acc = None
                for r in chunks:
                    x = src[r:r + ROW_CHUNK, :]
                    if is_last:
                        x = jnp.where(diag >= r, x, NEG_INF)
                        src[r:r + ROW_CHUNK, :] = x
                    acc = x if acc is None else jnp.maximum(acc, x)
                m_new = jnp.maximum(m_old[hh], jnp.max(acc, axis=0, keepdims=True))
                res[hh] = (m_new, jnp.exp(m_old[hh] - m_new))

            def exps(rows, hh=hh, src=src):
                for r in rows:
                    pt_sc[t % 2, hh, r:r + ROW_CHUNK, :] = jnp.exp(
                        src[r:r + ROW_CHUNK, :] - res[hh][0]).astype(BF16)

            half = len(chunks) // 2
            pieces += [col_max, functools.partial(exps, chunks[:half]),
                       functools.partial(exps, chunks[half:])]
        return pieces

    def values(item, t, alpha, acc_old, new):
        _, ks, width, _ = item

        def one(hh):
            vt = jnp.concatenate(
                [vt_ref[hh * HEAD_DIM:(hh + 1) * HEAD_DIM, ks:ks + width],
                 jnp.ones((SUM_ROWS, width), BF16)], axis=0)
            new[hh] = alpha[hh] * acc_old[hh] + jnp.dot(
                vt, pt_sc[t % 2, hh, 0:width, :],
                preferred_element_type=F32)
        return [functools.partial(one, hh) for hh in range(2)]

    n_items = len(items)
    m_fresh = [jnp.full((1, tq), NEG_INF, F32)] * 2
    acc_fresh = [jnp.zeros((HEAD_DIM + SUM_ROWS, tq), F32)] * 2
    alpha = {}
    qa_tile, qa = -1, None
    m_run, acc_run = m_fresh, acc_fresh
    for t in range(-2, n_items):
        matmuls, vector = [], []
        if t + 2 < n_items:
            nxt = items[t + 2]
            if nxt[0] != qa_tile:
                qa_tile, qa = nxt[0], query_blocks(nxt[0])
            matmuls += scores(nxt, qa, (t + 2) % 3)
        acc_new = [None, None]
        if t >= 0:
            matmuls += values(items[t], t, alpha.pop(t), acc_run, acc_new)
        res = [None, None]
        if 0 <= t + 1 < n_items:
            vector = probabilities(items[t + 1], t + 1, m_run, res)
        per = -(-len(vector) // max(len(matmuls), 1))
        for n, mm in enumerate(matmuls):
            mm()
            for piece in vector[n * per:(n + 1) * per]:
                piece()
        for piece in vector[len(matmuls) * per:]:
            piece()
        if t >= 0:
            acc_run = acc_new
            if items[t][3]:
                i = items[t][0]
                ot = jnp.concatenate(
                    [acc[:HEAD_DIM] / acc[HEAD_DIM:HEAD_DIM + 1] for acc in acc_run],
                    axis=0)
                o_ref[i * tq:(i + 1) * tq, :] = ot.T.astype(BF16)
                acc_run = acc_fresh
        if 0 <= t + 1 < n_items:
            m_run = m_fresh if items[t + 1][3] else [r[0] for r in res]
            alpha[t + 1] = [r[1] for r in res]


def _fox_attn(qt, k, vt, caug, bsz, seq, tq):
    tok = pl.BlockSpec((seq, LANES), lambda b, p: (b, p))
    feat = pl.BlockSpec((LANES, seq), lambda b, p: (p, b))
    return pl.pallas_call(
        functools.partial(_fox_attn_kernel, tq=tq, seq=seq),
        grid=(bsz, N_PAIRS),
        in_specs=[feat, tok, feat,
                  pl.BlockSpec((seq, LANES), lambda b, p: (b, 0))],
        out_specs=tok,
        out_shape=jax.ShapeDtypeStruct(k.shape, BF16),
        scratch_shapes=[pltpu.VMEM((3, 2, 2 * tq, tq), F32),
                        pltpu.VMEM((2, 2, 2 * tq, tq), BF16)],
        compiler_params=pltpu.CompilerParams(
            dimension_semantics=("parallel", "parallel")),
        name="fox_attn",
    )(qt, k, vt, caug)


def _post_mlp_kernel(*refs, emit_next, ff_chunk):
    if emit_next:
        (h_ref, o_ref, wo_ref, gm_ref, wup_ref, wdn_ref,
         ga_ref, gkv_ref, wq_ref, wkv_ref, gq_ref, gk_ref,
         out_ref, qb_ref, kd_ref, vt_ref) = refs
    else:
        h_ref, o_ref, wo_ref, gm_ref, wup_ref, wdn_ref, out_ref = refs
    h1 = h_ref[...] + jnp.dot(o_ref[...], wo_ref[...], preferred_element_type=F32)
    m = ((h1 * _rms_scale(h1)) * gm_ref[...]).astype(BF16)
    acc = h1
    for c in range(D_FF // ff_chunk):
        u = jnp.dot(m, wup_ref[:, c * ff_chunk:(c + 1) * ff_chunk],
                    preferred_element_type=F32)
        u = jnp.square(jnp.maximum(u, 0.0)).astype(BF16)
        acc = acc + jnp.dot(u, wdn_ref[c * ff_chunk:(c + 1) * ff_chunk, :],
                            preferred_element_type=F32)
    out_ref[...] = acc
    if emit_next:
        hn = acc * _rms_scale(acc)
        a = (hn * ga_ref[...]).astype(BF16)
        for c in range(D_MODEL // MXU_N):
            y = jnp.dot(a, wq_ref[:, c * MXU_N:(c + 1) * MXU_N],
                        preferred_element_type=F32)
            for half in range(MXU_N // LANES):
                yy = y[:, half * LANES:(half + 1) * LANES]
                dst = pl.ds(c * MXU_N + half * LANES, LANES)
                qb_ref[:, dst] = (_head_pair_norm(yy, gq_ref[...]) * QK_SCALE).astype(BF16)
        kvn = (hn * gkv_ref[...]).astype(BF16)
        kv = jnp.dot(kvn, wkv_ref[...], preferred_element_type=F32)
        for j in range(KV_HEADS):
            sl = slice(j * LANES, (j + 1) * LANES)
            kd_ref[:, sl] = _head_pair_norm(kv[:, sl], gk_ref[...]).astype(BF16)
        vt_ref[...] = kv[:, KV_HEADS * LANES:].T.astype(BF16)


def _post_mlp(h, o, wo, gm, wup, wdn, layer, nxt, tm, ff_chunk=1024):
    t = h.shape[0]
    tok = lambda w: pl.BlockSpec((tm, w), lambda i: (i, 0))
    in_specs = [tok(D_MODEL), tok(D_MODEL), _layer((D_MODEL, D_MODEL), 0),
                _resident((1, D_MODEL)), _layer((D_MODEL, D_FF), layer),
                _layer((D_FF, D_MODEL), layer)]
    out_specs = [tok(D_MODEL)]
    out_shape = [jax.ShapeDtypeStruct((t, D_MODEL), F32)]
    args = [h, o, wo, gm, wup, wdn]
    if nxt is not None:
        kvw = KV_HEADS * LANES + KV_HEADS * HEAD_DIM
        in_specs += [_resident((1, D_MODEL)), _resident((1, D_MODEL)),
                     _layer((D_MODEL, D_MODEL), 0), _resident((D_MODEL, kvw)),
                     _resident((1, LANES)), _resident((1, LANES))]
        out_specs += [tok(D_MODEL), tok(KV_HEADS * LANES),
                      pl.BlockSpec((KV_HEADS * HEAD_DIM, tm), lambda i: (0, i))]
        out_shape += [jax.ShapeDtypeStruct((t, D_MODEL), BF16),
                      jax.ShapeDtypeStruct((t, KV_HEADS * LANES), BF16),
                      jax.ShapeDtypeStruct((KV_HEADS * HEAD_DIM, t), BF16)]
        args += list(nxt)
    return pl.pallas_call(
        functools.partial(_post_mlp_kernel, emit_next=nxt is not None,
                          ff_chunk=ff_chunk),
        grid=(t // tm,),
        in_specs=in_specs, out_specs=out_specs, out_shape=out_shape,
        compiler_params=pltpu.CompilerParams(
            dimension_semantics=("parallel",), vmem_limit_bytes=VMEM_LIMIT),
        name="post_mlp_next" if nxt is not None else "post_mlp",
    )(*args)


def _t5_causal_bucket(dist):
    n = np.maximum(dist, 0)
    max_exact = N_BUCKETS // 2
    large = max_exact + (np.log(np.maximum(n, 1) / max_exact)
                         / np.log(REL_MAX_DIST / max_exact)
                         * (N_BUCKETS - max_exact)).astype(np.int32)
    large = np.minimum(large, N_BUCKETS - 1)
    return np.where(n < max_exact, n, large).astype(np.int32)


def _bucket_ranges():
    buckets = _t5_causal_bucket(np.arange(WINDOW))
    assert np.all(np.diff(buckets) >= 0)
    out = []
    for k in np.unique(buckets):
        idx = np.nonzero(buckets == k)[0]
        out.append((int(k), int(idx[0]), int(idx[-1]) + 1))
    return out


def _swa_attn_kernel(sink_ref, rb_ref, q_ref, kp_ref, kc_ref, vtp_ref, vtc_ref,
                     o_ref, bias_sc, st_sc, e_sc, *, n_blocks):
    n = pl.program_id(1)
    band = 2 * WINDOW

    @pl.when((pl.program_id(0) == 0) & (n == 0))
    def _():
        krow = lax.broadcasted_iota(jnp.int32, (band, WINDOW), 0)
        qcol = lax.broadcasted_iota(jnp.int32, (band, WINDOW), 1)
        dist = qcol + WINDOW - krow
        for h in range(N_HEADS):
            tile = jnp.full((band, WINDOW), NEG_INF, F32)
            for k, lo, hi in _bucket_ranges():
                tile = jnp.where((dist >= lo) & (dist < hi), rb_ref[k, h], tile)
            cols = slice((h % 2) * WINDOW, (h % 2 + 1) * WINDOW)
            bias_sc[0, h // 2, :, cols] = tile
            bias_sc[1, h // 2, :, cols] = jnp.where(krow < WINDOW, NEG_INF, tile)

    first = (n == 0).astype(jnp.int32)
    keys = jnp.concatenate([kp_ref[...], kc_ref[...]], axis=0)
    vts = jnp.concatenate([vtp_ref[...], vtc_ref[...]], axis=1)
    lane = lax.broadcasted_iota(jnp.int32, (1, LANES), 1)
    col = lax.broadcasted_iota(jnp.int32, (1, band), 1)
    pairs_per_kv = N_PAIRS // KV_HEADS
    items = [(j, p) for j in range(n_blocks) for p in range(N_PAIRS)]

    def scores(item, slot):
        j, p = item
        g = p // pairs_per_kv
        q2 = q_ref[j * WINDOW:(j + 1) * WINDOW, p * LANES:(p + 1) * LANES]
        zero = jnp.zeros_like(q2)
        qs = jnp.concatenate([jnp.where(lane < HEAD_DIM, q2, zero),
                              jnp.where(lane >= HEAD_DIM, q2, zero)], axis=0)
        st_sc[slot] = lax.dot_general(
            keys[j * WINDOW:j * WINDOW + band, g * LANES:(g + 1) * LANES], qs,
            (((1,), (1,)), ((), ())), preferred_element_type=F32)

    def softmax(item, slot):
        j, p = item
        st = st_sc[slot] + bias_sc[first if j == 0 else 0, p]
        sink = jnp.where(col < WINDOW, sink_ref[2 * p], sink_ref[2 * p + 1])
        m = jnp.maximum(_reduce_rows(st, jnp.maximum), sink)
        e = jnp.exp(st - m)
        e_sc[slot] = e.astype(BF16)
        return _reduce_rows(e, jnp.add) + jnp.exp(sink - m)

    def values(item, slot, denom):
        j, p = item
        g = p // pairs_per_kv
        ot = jnp.dot(vts[g * HEAD_DIM:(g + 1) * HEAD_DIM,
                         j * WINDOW:j * WINDOW + band], e_sc[slot],
                     preferred_element_type=F32) / denom
        o2 = jnp.concatenate([ot[:, :WINDOW], ot[:, WINDOW:]], axis=0).T
        o_ref[j * WINDOW:(j + 1) * WINDOW, p * LANES:(p + 1) * LANES] = o2.astype(BF16)

    scores(items[0], 0)
    scores(items[1], 1)
    denom = softmax(items[0], 0)
    for t, item in enumerate(items):
        if t + 2 < len(items):
            scores(items[t + 2], t % 2)
        values(item, t % 2, denom)
        if t + 1 < len(items):
            denom = softmax(items[t + 1], (t + 1) % 2)


def _swa_attn(sinks, rel_bias, q, kd, vt, bsz, seq, n_blocks):
    nblk = seq // WINDOW
    steps = nblk // n_blocks
    kvw = KV_HEADS * LANES
    vtw = KV_HEADS * HEAD_DIM
    tq = n_blocks * WINDOW
    prev_block = lambda b, n: b * nblk + jnp.maximum(n * n_blocks - 1, 0)
    smem = pl.BlockSpec(memory_space=pltpu.SMEM)
    return pl.pallas_call(
        functools.partial(_swa_attn_kernel, n_blocks=n_blocks),
        grid=(bsz, steps),
        in_specs=[
            smem, smem,
            pl.BlockSpec((tq, D_MODEL), lambda b, n: (b * steps + n, 0)),
            pl.BlockSpec((WINDOW, kvw), lambda b, n: (prev_block(b, n), 0)),
            pl.BlockSpec((tq, kvw), lambda b, n: (b * steps + n, 0)),
            pl.BlockSpec((vtw, WINDOW), lambda b, n: (0, prev_block(b, n))),
            pl.BlockSpec((vtw, tq), lambda b, n: (0, b * steps + n)),
        ],
        out_specs=pl.BlockSpec((tq, D_MODEL), lambda b, n: (b * steps + n, 0)),
        out_shape=jax.ShapeDtypeStruct((bsz * seq, D_MODEL), BF16),
        scratch_shapes=[pltpu.VMEM((2, N_PAIRS, 2 * WINDOW, 2 * WINDOW), F32),
                        pltpu.VMEM((2, 2 * WINDOW, 2 * WINDOW), F32),
                        pltpu.VMEM((2, 2 * WINDOW, 2 * WINDOW), BF16)],
        compiler_params=pltpu.CompilerParams(
            dimension_semantics=("arbitrary", "arbitrary")),
        name="swa_attn",
    )(sinks, rel_bias, q, kd, kd, vt, vt)


def _dup_heads(w):
    d = w.shape[0]
    w = w.reshape(d, KV_HEADS, 1, HEAD_DIM)
    return jnp.broadcast_to(w, (d, KV_HEADS, 2, HEAD_DIM)).reshape(d, KV_HEADS * LANES)


def _lane_groups(v):
    return jnp.pad(jnp.tile(v, N_SPLIT), (0, LANES - N_SPLIT * N_HEADS)).reshape(1, LANES)


def kernel(x, g_attn, g_mlp, w_in_a, b_f, gq_a, gk_a, w_out_a, g_kv, w_kv, gk_b,
           w_q_b, gq_b, sinks, rel_bias, w_out_b, w_up, w_down):
    bsz, seq, d = x.shape
    t = bsz * seq
    hw = N_HEADS * HEAD_DIM
    kvw = KV_HEADS * HEAD_DIM
    row = lambda g: g.reshape(1, -1).astype(F32)
    pair = lambda g: jnp.tile(g.astype(F32), 2).reshape(1, LANES)

    x2 = x.reshape(t, d)
    wf = jnp.pad(jnp.tile(w_in_a[0, :, 3 * hw:], (1, N_SPLIT)),
                 ((0, 0), (0, LANES - N_SPLIT * N_HEADS))).astype(BF16)
    w_up_b, w_down_b = w_up.astype(BF16), w_down.astype(BF16)

    qt, k, vt, f3 = _fox_pre(x2, row(g_attn[0]), w_in_a.astype(BF16), wf,
                            pair(gq_a[0]), pair(gk_a[0]), tm=512)
    caug = _fox_decay(f3, _lane_groups(b_f[0].astype(F32)), bsz, seq)
    o = _fox_attn(qt, k, vt, caug, bsz, seq, tq=256)

    wkv = jnp.concatenate([_dup_heads(w_kv[:, :kvw]), w_kv[:, kvw:]],
                          axis=1).astype(BF16)
    nxt = (row(g_attn[1]), row(g_kv), w_q_b.astype(BF16), wkv,
           pair(gq_b[0]), pair(gk_b))
    h, qb, kd, vt_b = _post_mlp(x2, o, w_out_a.astype(BF16), row(g_mlp[0]),
                                w_up_b, w_down_b, 0, nxt, tm=512)

    o2 = _swa_attn(sinks[0].astype(F32), rel_bias.astype(F32), qb, kd, vt_b,
                   bsz, seq, n_blocks=4)
    (out,) = _post_mlp(h, o2, w_out_b.astype(BF16), row(g_mlp[1]),
                       w_up_b, w_down_b, 1, None, tm=512)
    return out.reshape(bsz, seq, d)
```

```python
import functools

import numpy as np
import jax
import jax.numpy as jnp
from jax import lax
from jax.experimental import pallas as pl
from jax.experimental.pallas import tpu as pltpu

D_MODEL = 1024
HEAD_DIM = 64
N_HEADS = 16
N_PAIRS = N_HEADS // 2
KV_HEADS = 2
WINDOW = 128
D_FF = 4 * D_MODEL
N_BUCKETS = 32
REL_MAX_DIST = 128
NORM_EPS = 1e-6
LANES = 128
MXU_N = 256
QK_SCALE = HEAD_DIM ** -0.5
LOG2_E = 1.4426950408889634
N_SPLIT = 3
SUM_ROWS = 16
ROW_CHUNK = 32

F32 = jnp.float32
BF16 = jnp.bfloat16
NEG_INF = float("-inf")
VMEM_LIMIT = 56 * 1024 * 1024


def _rms_scale(x):
    return lax.rsqrt(jnp.mean(x * x, axis=-1, keepdims=True) + NORM_EPS)


def _head_pair_norm(y, g2):
    lo = lax.broadcasted_iota(jnp.int32, (1, LANES), 1) < HEAD_DIM
    sq = y * y
    s_lo = jnp.sum(jnp.where(lo, sq, 0.0), axis=-1, keepdims=True)
    s_hi = jnp.sum(jnp.where(lo, 0.0, sq), axis=-1, keepdims=True)
    ms = jnp.where(lo, s_lo, s_hi) * (1.0 / HEAD_DIM)
    return (y * lax.rsqrt(ms + NORM_EPS)) * g2


def _split_bf16(x):
    terms = []
    for _ in range(N_SPLIT):
        t = x.astype(BF16)
        terms.append(t)
        x = x - t.astype(F32)
    return terms


def _reduce_rows(x, op, group=32):
    acc = x[0:group]
    for r in range(group, x.shape[0], group):
        acc = op(acc, x[r:r + group])
    red = jnp.max if op is jnp.maximum else jnp.sum
    return red(acc, axis=0, keepdims=True)


def _resident(shape):
    zeros = (0,) * len(shape)
    return pl.BlockSpec(shape, lambda *_: zeros, pipeline_mode=pl.Buffered(1))


def _layer(shape, layer):
    index = (layer,) + (0,) * len(shape)
    return pl.BlockSpec((None,) + tuple(shape), lambda *_: index,
                        pipeline_mode=pl.Buffered(1))


def _fox_pre_kernel(x_ref, g_ref, wqkv_ref, wf_ref, gq_ref, gk_ref,
                    qt_ref, k_ref, vt_ref, f_ref):
    x = x_ref[...]
    hn = ((x * _rms_scale(x)) * g_ref[...]).astype(BF16)
    n_chunks = D_MODEL // MXU_N
    for c in range(3 * n_chunks):
        y = jnp.dot(hn, wqkv_ref[:, c * MXU_N:(c + 1) * MXU_N],
                    preferred_element_type=F32)
        part, cc = divmod(c, n_chunks)
        rows = slice(cc * MXU_N, (cc + 1) * MXU_N)
        if part == 2:
            vt_ref[rows, :] = y.T.astype(BF16)
            continue
        g2 = gq_ref[...] if part == 0 else gk_ref[...]
        yn = jnp.concatenate(
            [_head_pair_norm(y[:, h * LANES:(h + 1) * LANES], g2)
             for h in range(MXU_N // LANES)], axis=1)
        if part == 0:
            qt_ref[rows, :] = (yn * QK_SCALE).T.astype(BF16)
        else:
            k_ref[:, rows] = yn.astype(BF16)
    f_ref[...] = jnp.dot(hn, wf_ref[...], preferred_element_type=F32)


def _fox_pre(x2, g, w_in, wf, gq2, gk2, tm):
    t = x2.shape[0]
    tok = lambda w: pl.BlockSpec((tm, w), lambda i: (i, 0))
    return pl.pallas_call(
        _fox_pre_kernel,
        grid=(t // tm,),
        in_specs=[tok(D_MODEL), _resident((1, D_MODEL)),
                  _layer(w_in.shape[1:], 0), _resident((D_MODEL, LANES)),
                  _resident((1, LANES)), _resident((1, LANES))],
        out_specs=[pl.BlockSpec((D_MODEL, tm), lambda i: (0, i)), tok(D_MODEL),
                   pl.BlockSpec((D_MODEL, tm), lambda i: (0, i)), tok(LANES)],
        out_shape=[jax.ShapeDtypeStruct((D_MODEL, t), BF16),
                   jax.ShapeDtypeStruct((t, D_MODEL), BF16),
                   jax.ShapeDtypeStruct((D_MODEL, t), BF16),
                   jax.ShapeDtypeStruct((t, LANES), F32)],
        compiler_params=pltpu.CompilerParams(
            dimension_semantics=("parallel",), vmem_limit_bytes=VMEM_LIMIT),
        name="fox_pre",
    )(x2, g, w_in, wf, gq2, gk2)


def _fox_decay_kernel(f_ref, b_ref, c_ref, *, blk):
    seq = f_ref.shape[0]
    lane = lax.broadcasted_iota(jnp.int32, (1, LANES), 1)
    r = lax.broadcasted_iota(jnp.int32, (blk, blk), 0)
    c = lax.broadcasted_iota(jnp.int32, (blk, blk), 1)
    lower = jnp.where(r >= c, 1.0, 0.0).astype(BF16)
    carry = jnp.zeros((1, LANES), F32)
    for b in range(seq // blk):
        rows = slice(b * blk, (b + 1) * blk)
        x = f_ref[rows, :] + b_ref[...]
        log_f = -(jnp.maximum(-x, 0.0) + jnp.log1p(jnp.exp(-jnp.abs(x))))
        cb = carry
        for term in _split_bf16(log_f):
            cb = cb + jnp.dot(lower, term, preferred_element_type=F32)
        carry = cb[blk - 1:blk, :]
        out = jnp.zeros((blk, LANES), BF16)
        for t, term in reversed(list(enumerate(_split_bf16(cb)))):
            out = jnp.where(lane < (t + 1) * N_HEADS, term, out)
        c_ref[rows, :] = out


def _fox_decay(f3, b3, bsz, seq):
    spec = pl.BlockSpec((seq, LANES), lambda b: (b, 0))
    return pl.pallas_call(
        functools.partial(_fox_decay_kernel, blk=MXU_N),
        grid=(bsz,),
        in_specs=[spec, _resident((1, LANES))],
        out_specs=spec,
        out_shape=jax.ShapeDtypeStruct(f3.shape, BF16),
        compiler_params=pltpu.CompilerParams(dimension_semantics=("parallel",)),
        name="fox_decay",
    )(f3, b3)


def _fox_attn_kernel(qt_ref, k_ref, vt_ref, c_ref, o_ref, st_sc, pt_sc,
                     *, tq, tk, seq):
    p = pl.program_id(1)
    feat = lax.broadcasted_iota(jnp.int32, (LANES, 1), 0)
    own = [feat < HEAD_DIM, feat >= HEAD_DIM]
    neg = [jnp.broadcast_to(
        jnp.where((feat < N_SPLIT * N_HEADS)
                  & ((feat & (N_HEADS - 1)) == 2 * p + hh), -1.0, 0.0).astype(BF16),
        (LANES, tq)) for hh in range(2)]

    items = []
    for i in range(seq // tq):
        n_keys = (i + 1) * tq
        starts = list(range(0, n_keys, tk))
        for ks in starts:
            items.append((i, ks, min(tk, n_keys - ks), ks == starts[-1]))

    def query_blocks(i):
        q2 = qt_ref[:, i * tq:(i + 1) * tq]
        return [jnp.concatenate(
            [jnp.where(own[hh], q2, jnp.zeros_like(q2)), neg[hh]], axis=0)
            for hh in range(2)]

    def scores(item, qa, slot):
        _, ks, width, _ = item
        ka = jnp.concatenate([k_ref[ks:ks + width, :], c_ref[ks:ks + width, :]],
                             axis=1)

        def one(hh):
            st_sc[slot, hh, 0:width, :] = jnp.dot(
                ka, qa[hh], preferred_element_type=F32)
        return [functools.partial(one, hh) for hh in range(2)]

    def probabilities(item, t, m_old, res):
        i, ks, width, is_last = item
        chunks = list(range(0, width, ROW_CHUNK))
        if is_last:
            krow = lax.broadcasted_iota(jnp.int32, (ROW_CHUNK, tq), 0)
            qcol = lax.broadcasted_iota(jnp.int32, (ROW_CHUNK, tq), 1)
            diag = qcol - krow + (i * tq - ks)
        pieces = []
        for hh in range(2):
            src = st_sc.at[t % 3, hh]

            def col_max(hh=hh, src=src):
                acc = None
                for r in chunks:
                    x = src[r:r + ROW_CHUNK, :]
                    if is_last:
                        x = jnp.where(diag >= r, x, NEG_INF)
                        src[r:r + ROW_CHUNK, :] = x
                    acc = x if acc is None else jnp.maximum(acc, x)
                m_new = jnp.maximum(m_old[hh], jnp.max(acc, axis=0, keepdims=True))
                res[hh] = (m_new, jnp.exp(m_old[hh] - m_new))

            def exps(rows, hh=hh, src=src):
                for r in rows:
                    pt_sc[t % 2, hh, r:r + ROW_CHUNK, :] = jnp.exp(
                        src[r:r + ROW_CHUNK, :] - res[hh][0]).astype(BF16)

            half = len(chunks) // 2
            pieces += [col_max, functools.partial(exps, chunks[:half]),
                       functools.partial(exps, chunks[half:])]
        return pieces

    def values(item, t, alpha, acc_old, new):
        _, ks, width, _ = item

        def one(hh):
            vt = jnp.concatenate(
                [vt_ref[hh * HEAD_DIM:(hh + 1) * HEAD_DIM, ks:ks + width],
                 jnp.ones((SUM_ROWS, width), BF16)], axis=0)
            new[hh] = alpha[hh] * acc_old[hh] + jnp.dot(
                vt, pt_sc[t % 2, hh, 0:width, :],
                preferred_element_type=F32)
        return [functools.partial(one, hh) for hh in range(2)]

    n_items = len(items)
    m_fresh = [jnp.full((1, tq), NEG_INF, F32)] * 2
    acc_fresh = [jnp.zeros((HEAD_DIM + SUM_ROWS, tq), F32)] * 2
    alpha = {}
    qa_tile, qa = -1, None
    m_run, acc_run = m_fresh, acc_fresh
    for t in range(-2, n_items):
        matmuls, vector = [], []
        if t + 2 < n_items:
            nxt = items[t + 2]
            if nxt[0] != qa_tile:
                qa_tile, qa = nxt[0], query_blocks(nxt[0])
            matmuls += scores(nxt, qa, (t + 2) % 3)
        acc_new = [None, None]
        if t >= 0:
            matmuls += values(items[t], t, alpha.pop(t), acc_run, acc_new)
        res = [None, None]
        if 0 <= t + 1 < n_items:
            vector = probabilities(items[t + 1], t + 1, m_run, res)
        per = -(-len(vector) // max(len(matmuls), 1))
        for n, mm in enumerate(matmuls):
            mm()
            for piece in vector[n * per:(n + 1) * per]:
                piece()
        for piece in vector[len(matmuls) * per:]:
            piece()
        if t >= 0:
            acc_run = acc_new
            if items[t][3]:
                i = items[t][0]
                ot = jnp.concatenate(
                    [acc[:HEAD_DIM] / acc[HEAD_DIM:HEAD_DIM + 1] for acc in acc_run],
                    axis=0)
                o_ref[i * tq:(i + 1) * tq, :] = ot.T.astype(BF16)
                acc_run = acc_fresh
        if 0 <= t + 1 < n_items:
            m_run = m_fresh if items[t + 1][3] else [r[0] for r in res]
            alpha[t + 1] = [r[1] for r in res]


def _fox_attn(qt, k, vt, caug, bsz, seq, tq, tk):
    tok = pl.BlockSpec((seq, LANES), lambda b, p: (b, p))
    feat = pl.BlockSpec((LANES, seq), lambda b, p: (p, b))
    return pl.pallas_call(
        functools.partial(_fox_attn_kernel, tq=tq, tk=tk, seq=seq),
        grid=(bsz, N_PAIRS),
        in_specs=[feat, tok, feat,
                  pl.BlockSpec((seq, LANES), lambda b, p: (b, 0))],
        out_specs=tok,
        out_shape=jax.ShapeDtypeStruct(k.shape, BF16),
        scratch_shapes=[pltpu.VMEM((3, 2, tk, tq), F32),
                        pltpu.VMEM((2, 2, tk, tq), BF16)],
        compiler_params=pltpu.CompilerParams(
            dimension_semantics=("parallel", "parallel")),
        name="fox_attn",
    )(qt, k, vt, caug)


def _post_mlp_kernel(*refs, emit_next, ff_chunk):
    if emit_next:
        (h_ref, o_ref, wo_ref, gm_ref, wup_ref, wdn_ref,
         ga_ref, gkv_ref, wq_ref, wkv_ref, gq_ref, gk_ref,
         out_ref, qb_ref, kd_ref, vt_ref) = refs
    else:
        h_ref, o_ref, wo_ref, gm_ref, wup_ref, wdn_ref, out_ref = refs
    h1 = h_ref[...] + jnp.dot(o_ref[...], wo_ref[...], preferred_element_type=F32)
    m = ((h1 * _rms_scale(h1)) * gm_ref[...]).astype(BF16)
    acc = h1
    for c in range(D_FF // ff_chunk):
        u = jnp.dot(m, wup_ref[:, c * ff_chunk:(c + 1) * ff_chunk],
                    preferred_element_type=F32)
        u = jnp.square(jnp.maximum(u, 0.0)).astype(BF16)
        acc = acc + jnp.dot(u, wdn_ref[c * ff_chunk:(c + 1) * ff_chunk, :],
                            preferred_element_type=F32)
    out_ref[...] = acc
    if emit_next:
        hn = acc * _rms_scale(acc)
        a = (hn * ga_ref[...]).astype(BF16)
        for c in range(D_MODEL // MXU_N):
            y = jnp.dot(a, wq_ref[:, c * MXU_N:(c + 1) * MXU_N],
                        preferred_element_type=F32)
            for half in range(MXU_N // LANES):
                yy = y[:, half * LANES:(half + 1) * LANES]
                dst = pl.ds(c * MXU_N + half * LANES, LANES)
                qb_ref[:, dst] = (_head_pair_norm(yy, gq_ref[...])
                                  * (QK_SCALE * LOG2_E)).astype(BF16)
        kvn = (hn * gkv_ref[...]).astype(BF16)
        kv = jnp.dot(kvn, wkv_ref[...], preferred_element_type=F32)
        for j in range(KV_HEADS):
            sl = slice(j * LANES, (j + 1) * LANES)
            kd_ref[:, sl] = _head_pair_norm(kv[:, sl], gk_ref[...]).astype(BF16)
        vt_ref[...] = kv[:, KV_HEADS * LANES:].T.astype(BF16)


def _post_mlp(h, o, wo, gm, wup, wdn, layer, nxt, tm, ff_chunk=1024):
    t = h.shape[0]
    tok = lambda w: pl.BlockSpec((tm, w), lambda i: (i, 0))
    in_specs = [tok(D_MODEL), tok(D_MODEL), _layer((D_MODEL, D_MODEL), 0),
                _resident((1, D_MODEL)), _layer((D_MODEL, D_FF), layer),
                _layer((D_FF, D_MODEL), layer)]
    out_specs = [tok(D_MODEL)]
    out_shape = [jax.ShapeDtypeStruct((t, D_MODEL), F32)]
    args = [h, o, wo, gm, wup, wdn]
    if nxt is not None:
        kvw = KV_HEADS * LANES + KV_HEADS * HEAD_DIM
        in_specs += [_resident((1, D_MODEL)), _resident((1, D_MODEL)),
                     _layer((D_MODEL, D_MODEL), 0), _resident((D_MODEL, kvw)),
                     _resident((1, LANES)), _resident((1, LANES))]
        out_specs += [tok(D_MODEL), tok(KV_HEADS * LANES),
                      pl.BlockSpec((KV_HEADS * HEAD_DIM, tm), lambda i: (0, i))]
        out_shape += [jax.ShapeDtypeStruct((t, D_MODEL), BF16),
                      jax.ShapeDtypeStruct((t, KV_HEADS * LANES), BF16),
                      jax.ShapeDtypeStruct((KV_HEADS * HEAD_DIM, t), BF16)]
        args += list(nxt)
    return pl.pallas_call(
        functools.partial(_post_mlp_kernel, emit_next=nxt is not None,
                          ff_chunk=ff_chunk),
        grid=(t // tm,),
        in_specs=in_specs, out_specs=out_specs, out_shape=out_shape,
        compiler_params=pltpu.CompilerParams(
            dimension_semantics=("parallel",), vmem_limit_bytes=VMEM_LIMIT),
        name="post_mlp_next" if nxt is not None else "post_mlp",
    )(*args)


def _t5_causal_bucket(dist):
    n = np.maximum(dist, 0)
    max_exact = N_BUCKETS // 2
    large = max_exact + (np.log(np.maximum(n, 1) / max_exact)
                         / np.log(REL_MAX_DIST / max_exact)
                         * (N_BUCKETS - max_exact)).astype(np.int32)
    large = np.minimum(large, N_BUCKETS - 1)
    return np.where(n < max_exact, n, large).astype(np.int32)


def _bucket_ranges():
    buckets = _t5_causal_bucket(np.arange(WINDOW))
    assert np.all(np.diff(buckets) >= 0)
    out = []
    for k in np.unique(buckets):
        idx = np.nonzero(buckets == k)[0]
        out.append((int(k), int(idx[0]), int(idx[-1]) + 1))
    return out


def _swa_attn_kernel(sink_ref, rb_ref, q_ref, kp_ref, kc_ref, vtp_ref, vtc_ref,
                     o_ref, bias_sc, st_sc, e_sc, *, n_blocks):
    n = pl.program_id(1)
    band = 2 * WINDOW

    @pl.when((pl.program_id(0) == 0) & (n == 0))
    def _():
        krow = lax.broadcasted_iota(jnp.int32, (band, WINDOW), 0)
        qcol = lax.broadcasted_iota(jnp.int32, (band, WINDOW), 1)
        dist = qcol + WINDOW - krow
        for h in range(N_HEADS):
            tile = jnp.full((band, WINDOW), NEG_INF, F32)
            for k, lo, hi in _bucket_ranges():
                tile = jnp.where((dist >= lo) & (dist < hi),
                                 rb_ref[k, h] * LOG2_E, tile)
            cols = slice((h % 2) * WINDOW, (h % 2 + 1) * WINDOW)
            bias_sc[0, h // 2, :, cols] = tile
            bias_sc[1, h // 2, :, cols] = jnp.where(krow < WINDOW, NEG_INF, tile)

    first = (n == 0).astype(jnp.int32)
    keys = jnp.concatenate([kp_ref[...], kc_ref[...]], axis=0)
    vts = jnp.concatenate([vtp_ref[...], vtc_ref[...]], axis=1)
    lane = lax.broadcasted_iota(jnp.int32, (1, LANES), 1)
    col = lax.broadcasted_iota(jnp.int32, (1, band), 1)
    pairs_per_kv = N_PAIRS // KV_HEADS
    items = [(j, p) for j in range(n_blocks) for p in range(N_PAIRS)]

    def scores(item, slot):
        j, p = item
        g = p // pairs_per_kv
        q2 = q_ref[j * WINDOW:(j + 1) * WINDOW, p * LANES:(p + 1) * LANES]
        zero = jnp.zeros_like(q2)
        qs = jnp.concatenate([jnp.where(lane < HEAD_DIM, q2, zero),
                              jnp.where(lane >= HEAD_DIM, q2, zero)], axis=0)
        st_sc[slot] = lax.dot_general(
            keys[j * WINDOW:j * WINDOW + band, g * LANES:(g + 1) * LANES], qs,
            (((1,), (1,)), ((), ())), preferred_element_type=F32)

    def softmax(item, st_slot, slot):
        j, p = item
        st = st_sc[st_slot] + bias_sc[first if j == 0 else 0, p]
        sink = jnp.where(col < WINDOW, sink_ref[2 * p], sink_ref[2 * p + 1]) * LOG2_E
        m = jnp.maximum(_reduce_rows(st, jnp.maximum), sink)
        e_sc[slot] = jnp.exp2(st - m).astype(BF16)
        return jnp.exp2(sink - m)

    def values(item, slot, sink_term):
        j, p = item
        g = p // pairs_per_kv
        vt1 = jnp.concatenate(
            [vts[g * HEAD_DIM:(g + 1) * HEAD_DIM, j * WINDOW:j * WINDOW + band],
             jnp.ones((SUM_ROWS, band), BF16)], axis=0)
        acc = jnp.dot(vt1, e_sc[slot], preferred_element_type=F32)
        ot = acc[:HEAD_DIM] / (acc[HEAD_DIM:HEAD_DIM + 1] + sink_term)
        o2 = jnp.concatenate([ot[:, :WINDOW], ot[:, WINDOW:]], axis=0).T
        o_ref[j * WINDOW:(j + 1) * WINDOW, p * LANES:(p + 1) * LANES] = o2.astype(BF16)

    for t in range(3):
        scores(items[t], t)
    sink_term = softmax(items[0], 0, 0)
    for t, item in enumerate(items):
        if t + 3 < len(items):
            scores(items[t + 3], t % 3)
        values(item, t % 2, sink_term)
        if t + 1 < len(items):
            sink_term = softmax(items[t + 1], (t + 1) % 3, (t + 1) % 2)


def _swa_attn(sinks, rel_bias, q, kd, vt, bsz, seq, n_blocks):
    nblk = seq // WINDOW
    steps = nblk // n_blocks
    kvw = KV_HEADS * LANES
    vtw = KV_HEADS * HEAD_DIM
    tq = n_blocks * WINDOW
    prev_block = lambda b, n: b * nblk + jnp.maximum(n * n_blocks - 1, 0)
    smem = pl.BlockSpec(memory_space=pltpu.SMEM)
    return pl.pallas_call(
        functools.partial(_swa_attn_kernel, n_blocks=n_blocks),
        grid=(bsz, steps),
        in_specs=[
            smem, smem,
            pl.BlockSpec((tq, D_MODEL), lambda b, n: (b * steps + n, 0)),
            pl.BlockSpec((WINDOW, kvw), lambda b, n: (prev_block(b, n), 0)),
            pl.BlockSpec((tq, kvw), lambda b, n: (b * steps + n, 0)),
            pl.BlockSpec((vtw, WINDOW), lambda b, n: (0, prev_block(b, n))),
            pl.BlockSpec((vtw, tq), lambda b, n: (0, b * steps + n)),
        ],
        out_specs=pl.BlockSpec((tq, D_MODEL), lambda b, n: (b * steps + n, 0)),
        out_shape=jax.ShapeDtypeStruct((bsz * seq, D_MODEL), BF16),
        scratch_shapes=[pltpu.VMEM((2, N_PAIRS, 2 * WINDOW, 2 * WINDOW), F32),
                        pltpu.VMEM((3, 2 * WINDOW, 2 * WINDOW), F32),
                        pltpu.VMEM((2, 2 * WINDOW, 2 * WINDOW), BF16)],
        compiler_params=pltpu.CompilerParams(
            dimension_semantics=("arbitrary", "arbitrary")),
        name="swa_attn",
    )(sinks, rel_bias, q, kd, kd, vt, vt)


def _dup_heads(w):
    d = w.shape[0]
    w = w.reshape(d, KV_HEADS, 1, HEAD_DIM)
    return jnp.broadcast_to(w, (d, KV_HEADS, 2, HEAD_DIM)).reshape(d, KV_HEADS * LANES)


def _lane_groups(v):
    return jnp.pad(jnp.tile(v, N_SPLIT), (0, LANES - N_SPLIT * N_HEADS)).reshape(1, LANES)


def kernel(x, g_attn, g_mlp, w_in_a, b_f, gq_a, gk_a, w_out_a, g_kv, w_kv, gk_b,
           w_q_b, gq_b, sinks, rel_bias, w_out_b, w_up, w_down):
    bsz, seq, d = x.shape
    t = bsz * seq
    hw = N_HEADS * HEAD_DIM
    kvw = KV_HEADS * HEAD_DIM
    row = lambda g: g.reshape(1, -1).astype(F32)
    pair = lambda g: jnp.tile(g.astype(F32), 2).reshape(1, LANES)

    x2 = x.reshape(t, d)
    wf = jnp.pad(jnp.tile(w_in_a[0, :, 3 * hw:], (1, N_SPLIT)),
                 ((0, 0), (0, LANES - N_SPLIT * N_HEADS))).astype(BF16)
    w_up_b, w_down_b = w_up.astype(BF16), w_down.astype(BF16)

    qt, k, vt, f3 = _fox_pre(x2, row(g_attn[0]), w_in_a.astype(BF16), wf,
                            pair(gq_a[0]), pair(gk_a[0]), tm=512)
    caug = _fox_decay(f3, _lane_groups(b_f[0].astype(F32)), bsz, seq)
    o = _fox_attn(qt, k, vt, caug, bsz, seq, tq=256, tk=512)

    wkv = jnp.concatenate([_dup_heads(w_kv[:, :kvw]), w_kv[:, kvw:]],
                          axis=1).astype(BF16)
    nxt = (row(g_attn[1]), row(g_kv), w_q_b.astype(BF16), wkv,
           pair(gq_b[0]), pair(gk_b))
    h, qb, kd, vt_b = _post_mlp(x2, o, w_out_a.astype(BF16), row(g_mlp[0]),
                                w_up_b, w_down_b, 0, nxt, tm=512)

    o2 = _swa_attn(sinks[0].astype(F32), rel_bias.astype(F32), qb, kd, vt_b,
                   bsz, seq, n_blocks=8)
    (out,) = _post_mlp(h, o2, w_out_b.astype(BF16), row(g_mlp[1]),
                       w_up_b, w_down_b, 1, None, tm=512)
    return out.reshape(bsz, seq, d)
```

```python
import functools

import numpy as np
import jax
import jax.numpy as jnp
from jax import lax
from jax.experimental import pallas as pl
from jax.experimental.pallas import tpu as pltpu

D_MODEL = 1024
HEAD_DIM = 64
N_HEADS = 16
N_PAIRS = N_HEADS // 2
KV_HEADS = 2
WINDOW = 128
D_FF = 4 * D_MODEL
N_BUCKETS = 32
REL_MAX_DIST = 128
NORM_EPS = 1e-6
LANES = 128
MXU_N = 256
QK_SCALE = HEAD_DIM ** -0.5
LOG2_E = 1.4426950408889634
N_SPLIT = 3
SUM_ROWS = 16
ROW_CHUNK = 32

F32 = jnp.float32
BF16 = jnp.bfloat16
NEG_INF = float("-inf")
VMEM_LIMIT = 56 * 1024 * 1024


def _rms_scale(x):
    return lax.rsqrt(jnp.mean(x * x, axis=-1, keepdims=True) + NORM_EPS)


def _head_pair_norm(y, g2):
    lo = lax.broadcasted_iota(jnp.int32, (1, LANES), 1) < HEAD_DIM
    sq = y * y
    s_lo = jnp.sum(jnp.where(lo, sq, 0.0), axis=-1, keepdims=True)
    s_hi = jnp.sum(jnp.where(lo, 0.0, sq), axis=-1, keepdims=True)
    ms = jnp.where(lo, s_lo, s_hi) * (1.0 / HEAD_DIM)
    return (y * lax.rsqrt(ms + NORM_EPS)) * g2


def _split_bf16(x):
    terms = []
    for _ in range(N_SPLIT):
        t = x.astype(BF16)
        terms.append(t)
        x = x - t.astype(F32)
    return terms


def _reduce_rows(x, op, group=32):
    acc = x[0:group]
    for r in range(group, x.shape[0], group):
        acc = op(acc, x[r:r + group])
    red = jnp.max if op is jnp.maximum else jnp.sum
    return red(acc, axis=0, keepdims=True)


def _resident(shape):
    zeros = (0,) * len(shape)
    return pl.BlockSpec(shape, lambda *_: zeros, pipeline_mode=pl.Buffered(1))


def _layer(shape, layer):
    index = (layer,) + (0,) * len(shape)
    return pl.BlockSpec((None,) + tuple(shape), lambda *_: index,
                        pipeline_mode=pl.Buffered(1))


def _fox_pre_kernel(x_ref, g_ref, wqkv_ref, wf_ref, gq_ref, gk_ref,
                    qt_ref, k_ref, vt_ref, f_ref):
    x = x_ref[...]
    hn = ((x * _rms_scale(x)) * g_ref[...]).astype(BF16)
    n_chunks = D_MODEL // MXU_N
    for c in range(3 * n_chunks):
        y = jnp.dot(hn, wqkv_ref[:, c * MXU_N:(c + 1) * MXU_N],
                    preferred_element_type=F32)
        part, cc = divmod(c, n_chunks)
        rows = slice(cc * MXU_N, (cc + 1) * MXU_N)
        if part == 2:
            vt_ref[rows, :] = y.T.astype(BF16)
            continue
        g2 = gq_ref[...] if part == 0 else gk_ref[...]
        yn = jnp.concatenate(
            [_head_pair_norm(y[:, h * LANES:(h + 1) * LANES], g2)
             for h in range(MXU_N // LANES)], axis=1)
        if part == 0:
            qt_ref[rows, :] = (yn * QK_SCALE).T.astype(BF16)
        else:
            k_ref[:, rows] = yn.astype(BF16)
    f_ref[...] = jnp.dot(hn, wf_ref[...], preferred_element_type=F32)


def _fox_pre(x2, g, w_in, wf, gq2, gk2, tm):
    t = x2.shape[0]
    tok = lambda w: pl.BlockSpec((tm, w), lambda i: (i, 0))
    return pl.pallas_call(
        _fox_pre_kernel,
        grid=(t // tm,),
        in_specs=[tok(D_MODEL), _resident((1, D_MODEL)),
                  _resident(w_in.shape), _resident((D_MODEL, LANES)),
                  _resident((1, LANES)), _resident((1, LANES))],
        out_specs=[pl.BlockSpec((D_MODEL, tm), lambda i: (0, i)), tok(D_MODEL),
                   pl.BlockSpec((D_MODEL, tm), lambda i: (0, i)), tok(LANES)],
        out_shape=[jax.ShapeDtypeStruct((D_MODEL, t), BF16),
                   jax.ShapeDtypeStruct((t, D_MODEL), BF16),
                   jax.ShapeDtypeStruct((D_MODEL, t), BF16),
                   jax.ShapeDtypeStruct((t, LANES), F32)],
        compiler_params=pltpu.CompilerParams(
            dimension_semantics=("parallel",), vmem_limit_bytes=VMEM_LIMIT),
        name="fox_pre",
    )(x2, g, w_in, wf, gq2, gk2)


def _fox_decay_kernel(f_ref, b_ref, c_ref, *, blk):
    seq = f_ref.shape[0]
    lane = lax.broadcasted_iota(jnp.int32, (1, LANES), 1)
    r = lax.broadcasted_iota(jnp.int32, (blk, blk), 0)
    c = lax.broadcasted_iota(jnp.int32, (blk, blk), 1)
    lower = jnp.where(r >= c, 1.0, 0.0).astype(BF16)
    carry = jnp.zeros((1, LANES), F32)
    for b in range(seq // blk):
        rows = slice(b * blk, (b + 1) * blk)
        x = f_ref[rows, :] + b_ref[...]
        log_f = -(jnp.maximum(-x, 0.0) + jnp.log1p(jnp.exp(-jnp.abs(x))))
        cb = carry
        for term in _split_bf16(log_f):
            cb = cb + jnp.dot(lower, term, preferred_element_type=F32)
        carry = cb[blk - 1:blk, :]
        out = jnp.zeros((blk, LANES), BF16)
        for t, term in reversed(list(enumerate(_split_bf16(cb)))):
            out = jnp.where(lane < (t + 1) * N_HEADS, term, out)
        c_ref[rows, :] = out


def _fox_decay(f3, b3, bsz, seq):
    spec = pl.BlockSpec((seq, LANES), lambda b: (b, 0))
    return pl.pallas_call(
        functools.partial(_fox_decay_kernel, blk=MXU_N),
        grid=(bsz,),
        in_specs=[spec, _resident((1, LANES))],
        out_specs=spec,
        out_shape=jax.ShapeDtypeStruct(f3.shape, BF16),
        compiler_params=pltpu.CompilerParams(dimension_semantics=("parallel",)),
        name="fox_decay",
    )(f3, b3)


def _fox_attn_kernel(qt_ref, k_ref, vt_ref, c_ref, o_ref, st_sc, pt_sc,
                     *, tq, tk, seq):
    p = pl.program_id(1)
    feat = lax.broadcasted_iota(jnp.int32, (LANES, 1), 0)
    own = [feat < HEAD_DIM, feat >= HEAD_DIM]
    neg = [jnp.broadcast_to(
        jnp.where((feat < N_SPLIT * N_HEADS)
                  & ((feat & (N_HEADS - 1)) == 2 * p + hh), -1.0, 0.0).astype(BF16),
        (LANES, tq)) for hh in range(2)]

    items = []
    for i in range(seq // tq):
        n_keys = (i + 1) * tq
        starts = list(range(0, n_keys, tk))
        for ks in starts:
            items.append((i, ks, min(tk, n_keys - ks), ks == starts[-1]))

    def query_blocks(i):
        q2 = qt_ref[:, i * tq:(i + 1) * tq]
        return [jnp.concatenate(
            [jnp.where(own[hh], q2, jnp.zeros_like(q2)), neg[hh]], axis=0)
            for hh in range(2)]

    def scores(item, qa, slot):
        _, ks, width, _ = item
        ka = jnp.concatenate([k_ref[ks:ks + width, :], c_ref[ks:ks + width, :]],
                             axis=1)

        def one(hh):
            st_sc[slot, hh, 0:width, :] = jnp.dot(
                ka, qa[hh], preferred_element_type=F32)
        return [functools.partial(one, hh) for hh in range(2)]

    def probabilities(item, t, m_old, res):
        i, ks, width, is_last = item
        chunks = list(range(0, width, ROW_CHUNK))
        if is_last:
            krow = lax.broadcasted_iota(jnp.int32, (ROW_CHUNK, tq), 0)
            qcol = lax.broadcasted_iota(jnp.int32, (ROW_CHUNK, tq), 1)
            diag = qcol - krow + (i * tq - ks)
        pieces = []
        for hh in range(2):
            src = st_sc.at[t % 3, hh]

            def col_max(hh=hh, src=src):
                acc = None
                for r in chunks:
                    x = src[r:r + ROW_CHUNK, :]
                    if is_last:
                        x = jnp.where(diag >= r, x, NEG_INF)
                        src[r:r + ROW_CHUNK, :] = x
                    acc = x if acc is None else jnp.maximum(acc, x)
                m_new = jnp.maximum(m_old[hh], jnp.max(acc, axis=0, keepdims=True))
                res[hh] = (m_new, jnp.exp(m_old[hh] - m_new))

            def exps(rows, hh=hh, src=src):
                for r in rows:
                    pt_sc[t % 2, hh, r:r + ROW_CHUNK, :] = jnp.exp(
                        src[r:r + ROW_CHUNK, :] - res[hh][0]).astype(BF16)

            half = len(chunks) // 2
            pieces += [col_max, functools.partial(exps, chunks[:half]),
                       functools.partial(exps, chunks[half:])]
        return pieces

    def values(item, t, alpha, acc_old, new):
        _, ks, width, _ = item

        def one(hh):
            vt = jnp.concatenate(
                [vt_ref[hh * HEAD_DIM:(hh + 1) * HEAD_DIM, ks:ks + width],
                 jnp.ones((SUM_ROWS, width), BF16)], axis=0)
            new[hh] = alpha[hh] * acc_old[hh] + jnp.dot(
                vt, pt_sc[t % 2, hh, 0:width, :],
                preferred_element_type=F32)
        return [functools.partial(one, hh) for hh in range(2)]

    n_items = len(items)
    m_fresh = [jnp.full((1, tq), NEG_INF, F32)] * 2
    acc_fresh = [jnp.zeros((HEAD_DIM + SUM_ROWS, tq), F32)] * 2
    alpha = {}
    qa_tile, qa = -1, None
    m_run, acc_run = m_fresh, acc_fresh
    for t in range(-2, n_items):
        matmuls, vector = [], []
        if t + 2 < n_items:
            nxt = items[t + 2]
            if nxt[0] != qa_tile:
                qa_tile, qa = nxt[0], query_blocks(nxt[0])
            matmuls += scores(nxt, qa, (t + 2) % 3)
        acc_new = [None, None]
        if t >= 0:
            matmuls += values(items[t], t, alpha.pop(t), acc_run, acc_new)
        res = [None, None]
        if 0 <= t + 1 < n_items:
            vector = probabilities(items[t + 1], t + 1, m_run, res)
        per = -(-len(vector) // max(len(matmuls), 1))
        for n, mm in enumerate(matmuls):
            mm()
            for piece in vector[n * per:(n + 1) * per]:
                piece()
        for piece in vector[len(matmuls) * per:]:
            piece()
        if t >= 0:
            acc_run = acc_new
            if items[t][3]:
                i = items[t][0]
                ot = jnp.concatenate(
                    [acc[:HEAD_DIM] / acc[HEAD_DIM:HEAD_DIM + 1] for acc in acc_run],
                    axis=0)
                o_ref[i * tq:(i + 1) * tq, :] = ot.T.astype(BF16)
                acc_run = acc_fresh
        if 0 <= t + 1 < n_items:
            m_run = m_fresh if items[t + 1][3] else [r[0] for r in res]
            alpha[t + 1] = [r[1] for r in res]


def _fox_attn(qt, k, vt, caug, bsz, seq, tq, tk):
    tok = pl.BlockSpec((seq, LANES), lambda b, p: (b, p))
    feat = pl.BlockSpec((LANES, seq), lambda b, p: (p, b))
    return pl.pallas_call(
        functools.partial(_fox_attn_kernel, tq=tq, tk=tk, seq=seq),
        grid=(bsz, N_PAIRS),
        in_specs=[feat, tok, feat,
                  pl.BlockSpec((seq, LANES), lambda b, p: (b, 0))],
        out_specs=tok,
        out_shape=jax.ShapeDtypeStruct(k.shape, BF16),
        scratch_shapes=[pltpu.VMEM((3, 2, tk, tq), F32),
                        pltpu.VMEM((2, 2, tk, tq), BF16)],
        compiler_params=pltpu.CompilerParams(
            dimension_semantics=("parallel", "parallel")),
        name="fox_attn",
    )(qt, k, vt, caug)


def _post_mlp_kernel(*refs, emit_next, ff_chunk):
    if emit_next:
        (h_ref, o_ref, wo_ref, gm_ref, wup_ref, wdn_ref,
         ga_ref, gkv_ref, wq_ref, wkv_ref, gq_ref, gk_ref,
         out_ref, qb_ref, kd_ref, vt_ref) = refs
    else:
        h_ref, o_ref, wo_ref, gm_ref, wup_ref, wdn_ref, out_ref = refs
    h1 = h_ref[...] + jnp.dot(o_ref[...], wo_ref[...], preferred_element_type=F32)
    m = ((h1 * _rms_scale(h1)) * gm_ref[...]).astype(BF16)
    acc = h1
    for c in range(D_FF // ff_chunk):
        u = jnp.dot(m, wup_ref[:, c * ff_chunk:(c + 1) * ff_chunk],
                    preferred_element_type=F32)
        u = jnp.square(jnp.maximum(u, 0.0)).astype(BF16)
        acc = acc + jnp.dot(u, wdn_ref[c * ff_chunk:(c + 1) * ff_chunk, :],
                            preferred_element_type=F32)
    out_ref[...] = acc
    if emit_next:
        hn = acc * _rms_scale(acc)
        a = (hn * ga_ref[...]).astype(BF16)
        for c in range(D_MODEL // MXU_N):
            y = jnp.dot(a, wq_ref[:, c * MXU_N:(c + 1) * MXU_N],
                        preferred_element_type=F32)
            for half in range(MXU_N // LANES):
                yy = y[:, half * LANES:(half + 1) * LANES]
                dst = pl.ds(c * MXU_N + half * LANES, LANES)
                qb_ref[:, dst] = (_head_pair_norm(yy, gq_ref[...])
                                  * (QK_SCALE * LOG2_E)).astype(BF16)
        kvn = (hn * gkv_ref[...]).astype(BF16)
        kv = jnp.dot(kvn, wkv_ref[...], preferred_element_type=F32)
        for j in range(KV_HEADS):
            sl = slice(j * LANES, (j + 1) * LANES)
            kd_ref[:, sl] = _head_pair_norm(kv[:, sl], gk_ref[...]).astype(BF16)
        vt_ref[...] = kv[:, KV_HEADS * LANES:].T.astype(BF16)


def _post_mlp(h, o, wo, gm, wup, wdn, layer, nxt, tm, ff_chunk=1024):
    t = h.shape[0]
    tok = lambda w: pl.BlockSpec((tm, w), lambda i: (i, 0))
    in_specs = [tok(D_MODEL), tok(D_MODEL), _layer((D_MODEL, D_MODEL), 0),
                _resident((1, D_MODEL)), _layer((D_MODEL, D_FF), layer),
                _layer((D_FF, D_MODEL), layer)]
    out_specs = [tok(D_MODEL)]
    out_shape = [jax.ShapeDtypeStruct((t, D_MODEL), F32)]
    args = [h, o, wo, gm, wup, wdn]
    if nxt is not None:
        kvw = KV_HEADS * LANES + KV_HEADS * HEAD_DIM
        in_specs += [_resident((1, D_MODEL)), _resident((1, D_MODEL)),
                     _layer((D_MODEL, D_MODEL), 0), _resident((D_MODEL, kvw)),
                     _resident((1, LANES)), _resident((1, LANES))]
        out_specs += [tok(D_MODEL), tok(KV_HEADS * LANES),
                      pl.BlockSpec((KV_HEADS * HEAD_DIM, tm), lambda i: (0, i))]
        out_shape += [jax.ShapeDtypeStruct((t, D_MODEL), BF16),
                      jax.ShapeDtypeStruct((t, KV_HEADS * LANES), BF16),
                      jax.ShapeDtypeStruct((KV_HEADS * HEAD_DIM, t), BF16)]
        args += list(nxt)
    return pl.pallas_call(
        functools.partial(_post_mlp_kernel, emit_next=nxt is not None,
                          ff_chunk=ff_chunk),
        grid=(t // tm,),
        in_specs=in_specs, out_specs=out_specs, out_shape=out_shape,
        compiler_params=pltpu.CompilerParams(
            dimension_semantics=("parallel",), vmem_limit_bytes=VMEM_LIMIT),
        name="post_mlp_next" if nxt is not None else "post_mlp",
    )(*args)


def _t5_causal_bucket(dist):
    n = np.maximum(dist, 0)
    max_exact = N_BUCKETS // 2
    large = max_exact + (np.log(np.maximum(n, 1) / max_exact)
                         / np.log(REL_MAX_DIST / max_exact)
                         * (N_BUCKETS - max_exact)).astype(np.int32)
    large = np.minimum(large, N_BUCKETS - 1)
    return np.where(n < max_exact, n, large).astype(np.int32)


def _bucket_ranges():
    buckets = _t5_causal_bucket(np.arange(WINDOW))
    assert np.all(np.diff(buckets) >= 0)
    out = []
    for k in np.unique(buckets):
        idx = np.nonzero(buckets == k)[0]
        out.append((int(k), int(idx[0]), int(idx[-1]) + 1))
    return out


def _swa_attn_kernel(sink_ref, rb_ref, q_ref, kp_ref, kc_ref, vtp_ref, vtc_ref,
                     o_ref, bias_sc, st_sc, e_sc, *, n_blocks):
    n = pl.program_id(1)
    band = 2 * WINDOW

    @pl.when((pl.program_id(0) == 0) & (n == 0))
    def _():
        krow = lax.broadcasted_iota(jnp.int32, (band, WINDOW), 0)
        qcol = lax.broadcasted_iota(jnp.int32, (band, WINDOW), 1)
        dist = qcol + WINDOW - krow
        for h in range(N_HEADS):
            tile = jnp.full((band, WINDOW), NEG_INF, F32)
            for k, lo, hi in _bucket_ranges():
                tile = jnp.where((dist >= lo) & (dist < hi),
                                 rb_ref[k, h] * LOG2_E, tile)
            cols = slice((h % 2) * WINDOW, (h % 2 + 1) * WINDOW)
            bias_sc[0, h // 2, :, cols] = tile
            bias_sc[1, h // 2, :, cols] = jnp.where(krow < WINDOW, NEG_INF, tile)

    first = (n == 0).astype(jnp.int32)
    keys = jnp.concatenate([kp_ref[...], kc_ref[...]], axis=0)
    vts = jnp.concatenate([vtp_ref[...], vtc_ref[...]], axis=1)
    lane = lax.broadcasted_iota(jnp.int32, (1, LANES), 1)
    col = lax.broadcasted_iota(jnp.int32, (1, band), 1)
    pairs_per_kv = N_PAIRS // KV_HEADS
    items = [(j, p) for j in range(n_blocks) for p in range(N_PAIRS)]

    def scores(item, slot):
        j, p = item
        g = p // pairs_per_kv
        q2 = q_ref[j * WINDOW:(j + 1) * WINDOW, p * LANES:(p + 1) * LANES]
        zero = jnp.zeros_like(q2)
        qs = jnp.concatenate([jnp.where(lane < HEAD_DIM, q2, zero),
                              jnp.where(lane >= HEAD_DIM, q2, zero)], axis=0)
        st_sc[slot] = lax.dot_general(
            keys[j * WINDOW:j * WINDOW + band, g * LANES:(g + 1) * LANES], qs,
            (((1,), (1,)), ((), ())), preferred_element_type=F32)

    def softmax(item, st_slot, slot):
        j, p = item
        st = st_sc[st_slot] + bias_sc[first if j == 0 else 0, p]
        sink = jnp.where(col < WINDOW, sink_ref[2 * p], sink_ref[2 * p + 1]) * LOG2_E
        m = jnp.maximum(_reduce_rows(st, jnp.maximum), sink)
        e_sc[slot] = jnp.exp2(st - m).astype(BF16)
        return jnp.exp2(sink - m)

    def values(item, slot, sink_term):
        j, p = item
        g = p // pairs_per_kv
        vt1 = jnp.concatenate(
            [vts[g * HEAD_DIM:(g + 1) * HEAD_DIM, j * WINDOW:j * WINDOW + band],
             jnp.ones((SUM_ROWS, band), BF16)], axis=0)
        acc = jnp.dot(vt1, e_sc[slot], preferred_element_type=F32)
        ot = acc[:HEAD_DIM] / (acc[HEAD_DIM:HEAD_DIM + 1] + sink_term)
        o2 = jnp.concatenate([ot[:, :WINDOW], ot[:, WINDOW:]], axis=0).T
        o_ref[j * WINDOW:(j + 1) * WINDOW, p * LANES:(p + 1) * LANES] = o2.astype(BF16)

    for t in range(3):
        scores(items[t], t)
    sink_term = softmax(items[0], 0, 0)
    for t, item in enumerate(items):
        if t + 3 < len(items):
            scores(items[t + 3], t % 3)
        values(item, t % 2, sink_term)
        if t + 1 < len(items):
            sink_term = softmax(items[t + 1], (t + 1) % 3, (t + 1) % 2)


def _swa_attn(sinks, rel_bias, q, kd, vt, bsz, seq, n_blocks):
    nblk = seq // WINDOW
    steps = nblk // n_blocks
    kvw = KV_HEADS * LANES
    vtw = KV_HEADS * HEAD_DIM
    tq = n_blocks * WINDOW
    prev_block = lambda b, n: b * nblk + jnp.maximum(n * n_blocks - 1, 0)
    smem = pl.BlockSpec(memory_space=pltpu.SMEM)
    return pl.pallas_call(
        functools.partial(_swa_attn_kernel, n_blocks=n_blocks),
        grid=(bsz, steps),
        in_specs=[
            smem, smem,
            pl.BlockSpec((tq, D_MODEL), lambda b, n: (b * steps + n, 0)),
            pl.BlockSpec((WINDOW, kvw), lambda b, n: (prev_block(b, n), 0)),
            pl.BlockSpec((tq, kvw), lambda b, n: (b * steps + n, 0)),
            pl.BlockSpec((vtw, WINDOW), lambda b, n: (0, prev_block(b, n))),
            pl.BlockSpec((vtw, tq), lambda b, n: (0, b * steps + n)),
        ],
        out_specs=pl.BlockSpec((tq, D_MODEL), lambda b, n: (b * steps + n, 0)),
        out_shape=jax.ShapeDtypeStruct((bsz * seq, D_MODEL), BF16),
        scratch_shapes=[pltpu.VMEM((2, N_PAIRS, 2 * WINDOW, 2 * WINDOW), F32),
                        pltpu.VMEM((3, 2 * WINDOW, 2 * WINDOW), F32),
                        pltpu.VMEM((2, 2 * WINDOW, 2 * WINDOW), BF16)],
        compiler_params=pltpu.CompilerParams(
            dimension_semantics=("arbitrary", "arbitrary")),
        name="swa_attn",
    )(sinks, rel_bias, q, kd, kd, vt, vt)


def _dup_heads(w):
    d = w.shape[0]
    w = w.reshape(d, KV_HEADS, 1, HEAD_DIM)
    return jnp.broadcast_to(w, (d, KV_HEADS, 2, HEAD_DIM)).reshape(d, KV_HEADS * LANES)


def _lane_groups(v):
    return jnp.pad(jnp.tile(v, N_SPLIT), (0, LANES - N_SPLIT * N_HEADS)).reshape(1, LANES)


def kernel(x, g_attn, g_mlp, w_in_a, b_f, gq_a, gk_a, w_out_a, g_kv, w_kv, gk_b,
           w_q_b, gq_b, sinks, rel_bias, w_out_b, w_up, w_down):
    bsz, seq, d = x.shape
    t = bsz * seq
    hw = N_HEADS * HEAD_DIM
    kvw = KV_HEADS * HEAD_DIM
    row = lambda g: g.reshape(1, -1).astype(F32)
    pair = lambda g: jnp.tile(g.astype(F32), 2).reshape(1, LANES)

    x2 = x.reshape(t, d)
    wf = jnp.pad(jnp.tile(w_in_a[0, :, 3 * hw:], (1, N_SPLIT)),
                 ((0, 0), (0, LANES - N_SPLIT * N_HEADS))).astype(BF16)
    w_up_b, w_down_b = w_up.astype(BF16), w_down.astype(BF16)

    qt, k, vt, f3 = _fox_pre(x2, row(g_attn[0]), w_in_a[0, :, :3 * hw].astype(BF16), wf,
                            pair(gq_a[0]), pair(gk_a[0]), tm=512)
    caug = _fox_decay(f3, _lane_groups(b_f[0].astype(F32)), bsz, seq)
    o = _fox_attn(qt, k, vt, caug, bsz, seq, tq=256, tk=1024)

    wkv = jnp.concatenate([_dup_heads(w_kv[:, :kvw]), w_kv[:, kvw:]],
                          axis=1).astype(BF16)
    nxt = (row(g_attn[1]), row(g_kv), w_q_b.astype(BF16), wkv,
           pair(gq_b[0]), pair(gk_b))
    h, qb, kd, vt_b = _post_mlp(x2, o, w_out_a.astype(BF16), row(g_mlp[0]),
                                w_up_b, w_down_b, 0, nxt, tm=512)

    o2 = _swa_attn(sinks[0].astype(F32), rel_bias.astype(F32), qb, kd, vt_b,
                   bsz, seq, n_blocks=8)
    (out,) = _post_mlp(h, o2, w_out_b.astype(BF16), row(g_mlp[1]),
                       w_up_b, w_down_b, 1, None, tm=512)
    return out.reshape(bsz, seq, d)
```

```python
import functools

import numpy as np
import jax
import jax.numpy as jnp
from jax import lax
from jax.experimental import pallas as pl
from jax.experimental.pallas import tpu as pltpu

D_MODEL = 1024
HEAD_DIM = 64
N_HEADS = 16
N_PAIRS = N_HEADS // 2
KV_HEADS = 2
WINDOW = 128
D_FF = 4 * D_MODEL
N_BUCKETS = 32
REL_MAX_DIST = 128
NORM_EPS = 1e-6
LANES = 128
MXU_N = 256
QK_SCALE = HEAD_DIM ** -0.5
LOG2_E = 1.4426950408889634
N_SPLIT = 3
SUM_ROWS = 16
ROW_CHUNK = 32

F32 = jnp.float32
BF16 = jnp.bfloat16
NEG_INF = float("-inf")
VMEM_LIMIT = 56 * 1024 * 1024


def _rms_scale(x):
    return lax.rsqrt(jnp.mean(x * x, axis=-1, keepdims=True) + NORM_EPS)


def _head_pair_norm(y, g2):
    lo = lax.broadcasted_iota(jnp.int32, (1, LANES), 1) < HEAD_DIM
    sq = y * y
    s_lo = jnp.sum(jnp.where(lo, sq, 0.0), axis=-1, keepdims=True)
    s_hi = jnp.sum(jnp.where(lo, 0.0, sq), axis=-1, keepdims=True)
    ms = jnp.where(lo, s_lo, s_hi) * (1.0 / HEAD_DIM)
    return (y * lax.rsqrt(ms + NORM_EPS)) * g2


def _split_bf16(x):
    terms = []
    for _ in range(N_SPLIT):
        t = x.astype(BF16)
        terms.append(t)
        x = x - t.astype(F32)
    return terms


def _reduce_rows(x, op, group=32):
    acc = x[0:group]
    for r in range(group, x.shape[0], group):
        acc = op(acc, x[r:r + group])
    red = jnp.max if op is jnp.maximum else jnp.sum
    return red(acc, axis=0, keepdims=True)


def _resident(shape):
    zeros = (0,) * len(shape)
    return pl.BlockSpec(shape, lambda *_: zeros, pipeline_mode=pl.Buffered(1))


def _layer(shape, layer):
    index = (layer,) + (0,) * len(shape)
    return pl.BlockSpec((None,) + tuple(shape), lambda *_: index,
                        pipeline_mode=pl.Buffered(1))


def _fox_pre_kernel(x_ref, g_ref, wqkv_ref, wf_ref, gq_ref, gk_ref,
                    qt_ref, k_ref, vt_ref, f_ref):
    x = x_ref[...]
    hn = ((x * _rms_scale(x)) * g_ref[...]).astype(BF16)
    n_chunks = D_MODEL // MXU_N
    for c in range(3 * n_chunks):
        y = jnp.dot(hn, wqkv_ref[:, c * MXU_N:(c + 1) * MXU_N],
                    preferred_element_type=F32)
        part, cc = divmod(c, n_chunks)
        rows = slice(cc * MXU_N, (cc + 1) * MXU_N)
        if part == 2:
            vt_ref[rows, :] = y.T.astype(BF16)
            continue
        g2 = gq_ref[...] if part == 0 else gk_ref[...]
        yn = jnp.concatenate(
            [_head_pair_norm(y[:, h * LANES:(h + 1) * LANES], g2)
             for h in range(MXU_N // LANES)], axis=1)
        if part == 0:
            qt_ref[rows, :] = (yn * QK_SCALE).T.astype(BF16)
        else:
            k_ref[:, rows] = yn.astype(BF16)
    f_ref[...] = jnp.dot(hn, wf_ref[...], preferred_element_type=F32)


def _fox_pre(x2, g, w_in, wf, gq2, gk2, tm):
    t = x2.shape[0]
    tok = lambda w: pl.BlockSpec((tm, w), lambda i: (i, 0))
    return pl.pallas_call(
        _fox_pre_kernel,
        grid=(t // tm,),
        in_specs=[tok(D_MODEL), _resident((1, D_MODEL)),
                  _resident(w_in.shape), _resident((D_MODEL, LANES)),
                  _resident((1, LANES)), _resident((1, LANES))],
        out_specs=[pl.BlockSpec((D_MODEL, tm), lambda i: (0, i)), tok(D_MODEL),
                   pl.BlockSpec((D_MODEL, tm), lambda i: (0, i)), tok(LANES)],
        out_shape=[jax.ShapeDtypeStruct((D_MODEL, t), BF16),
                   jax.ShapeDtypeStruct((t, D_MODEL), BF16),
                   jax.ShapeDtypeStruct((D_MODEL, t), BF16),
                   jax.ShapeDtypeStruct((t, LANES), F32)],
        compiler_params=pltpu.CompilerParams(
            dimension_semantics=("parallel",), vmem_limit_bytes=VMEM_LIMIT),
        name="fox_pre",
    )(x2, g, w_in, wf, gq2, gk2)


def _fox_decay_kernel(f_ref, b_ref, c_ref, *, blk):
    seq = f_ref.shape[0]
    lane = lax.broadcasted_iota(jnp.int32, (1, LANES), 1)
    r = lax.broadcasted_iota(jnp.int32, (blk, blk), 0)
    c = lax.broadcasted_iota(jnp.int32, (blk, blk), 1)
    lower = jnp.where(r >= c, 1.0, 0.0).astype(BF16)
    carry = jnp.zeros((1, LANES), F32)
    for b in range(seq // blk):
        rows = slice(b * blk, (b + 1) * blk)
        x = f_ref[rows, :] + b_ref[...]
        log_f = -(jnp.maximum(-x, 0.0) + jnp.log1p(jnp.exp(-jnp.abs(x))))
        cb = carry
        for term in _split_bf16(log_f):
            cb = cb + jnp.dot(lower, term, preferred_element_type=F32)
        carry = cb[blk - 1:blk, :]
        out = jnp.zeros((blk, LANES), BF16)
        for t, term in reversed(list(enumerate(_split_bf16(cb)))):
            out = jnp.where(lane < (t + 1) * N_HEADS, term, out)
        c_ref[rows, :] = out


def _fox_decay(f3, b3, bsz, seq):
    spec = pl.BlockSpec((seq, LANES), lambda b: (b, 0))
    return pl.pallas_call(
        functools.partial(_fox_decay_kernel, blk=MXU_N),
        grid=(bsz,),
        in_specs=[spec, _resident((1, LANES))],
        out_specs=spec,
        out_shape=jax.ShapeDtypeStruct(f3.shape, BF16),
        compiler_params=pltpu.CompilerParams(dimension_semantics=("parallel",)),
        name="fox_decay",
    )(f3, b3)


def _fox_attn_kernel(qt_ref, k_ref, vt_ref, c_ref, *rest, tq, tk, seq, n_cast):
    w_refs, o_ref = rest[:n_cast], rest[n_cast]
    wb_refs = rest[n_cast + 1:2 * n_cast + 1]
    st_sc, pt_sc = rest[2 * n_cast + 1:]
    for w_ref, wb_ref in zip(w_refs, wb_refs):
        wb_ref[...] = w_ref[...].astype(BF16)
    p = pl.program_id(1)
    feat = lax.broadcasted_iota(jnp.int32, (LANES, 1), 0)
    own = [feat < HEAD_DIM, feat >= HEAD_DIM]
    neg = [jnp.broadcast_to(
        jnp.where((feat < N_SPLIT * N_HEADS)
                  & ((feat & (N_HEADS - 1)) == 2 * p + hh), -1.0, 0.0).astype(BF16),
        (LANES, tq)) for hh in range(2)]

    items = []
    for i in range(seq // tq):
        n_keys = (i + 1) * tq
        starts = list(range(0, n_keys, tk))
        for ks in starts:
            items.append((i, ks, min(tk, n_keys - ks), ks == starts[-1]))

    def query_blocks(i):
        q2 = qt_ref[:, i * tq:(i + 1) * tq]
        return [jnp.concatenate(
            [jnp.where(own[hh], q2, jnp.zeros_like(q2)), neg[hh]], axis=0)
            for hh in range(2)]

    def scores(item, qa, slot):
        _, ks, width, _ = item
        ka = jnp.concatenate([k_ref[ks:ks + width, :], c_ref[ks:ks + width, :]],
                             axis=1)

        def one(hh):
            st_sc[slot, hh, 0:width, :] = jnp.dot(
                ka, qa[hh], preferred_element_type=F32)
        return [functools.partial(one, hh) for hh in range(2)]

    def probabilities(item, t, m_old, res):
        i, ks, width, is_last = item
        chunks = list(range(0, width, ROW_CHUNK))
        if is_last:
            krow = lax.broadcasted_iota(jnp.int32, (ROW_CHUNK, tq), 0)
            qcol = lax.broadcasted_iota(jnp.int32, (ROW_CHUNK, tq), 1)
            diag = qcol - krow + (i * tq - ks)
        pieces = []
        for hh in range(2):
            src = st_sc.at[t % 3, hh]

            def col_max(hh=hh, src=src):
                acc = None
                for r in chunks:
                    x = src[r:r + ROW_CHUNK, :]
                    if is_last:
                        x = jnp.where(diag >= r, x, NEG_INF)
                        src[r:r + ROW_CHUNK, :] = x
                    acc = x if acc is None else jnp.maximum(acc, x)
                m_new = jnp.maximum(m_old[hh], jnp.max(acc, axis=0, keepdims=True))
                res[hh] = (m_new, jnp.exp(m_old[hh] - m_new))

            def exps(rows, hh=hh, src=src):
                for r in rows:
                    pt_sc[t % 2, hh, r:r + ROW_CHUNK, :] = jnp.exp(
                        src[r:r + ROW_CHUNK, :] - res[hh][0]).astype(BF16)

            half = len(chunks) // 2
            pieces += [col_max, functools.partial(exps, chunks[:half]),
                       functools.partial(exps, chunks[half:])]
        return pieces

    def values(item, t, alpha, acc_old, new):
        _, ks, width, _ = item

        def one(hh):
            vt = jnp.concatenate(
                [vt_ref[hh * HEAD_DIM:(hh + 1) * HEAD_DIM, ks:ks + width],
                 jnp.ones((SUM_ROWS, width), BF16)], axis=0)
            new[hh] = alpha[hh] * acc_old[hh] + jnp.dot(
                vt, pt_sc[t % 2, hh, 0:width, :],
                preferred_element_type=F32)
        return [functools.partial(one, hh) for hh in range(2)]

    n_items = len(items)
    m_fresh = [jnp.full((1, tq), NEG_INF, F32)] * 2
    acc_fresh = [jnp.zeros((HEAD_DIM + SUM_ROWS, tq), F32)] * 2
    alpha = {}
    qa_tile, qa = -1, None
    m_run, acc_run = m_fresh, acc_fresh
    for t in range(-2, n_items):
        matmuls, vector = [], []
        if t + 2 < n_items:
            nxt = items[t + 2]
            if nxt[0] != qa_tile:
                qa_tile, qa = nxt[0], query_blocks(nxt[0])
            matmuls += scores(nxt, qa, (t + 2) % 3)
        acc_new = [None, None]
        if t >= 0:
            matmuls += values(items[t], t, alpha.pop(t), acc_run, acc_new)
        res = [None, None]
        if 0 <= t + 1 < n_items:
            vector = probabilities(items[t + 1], t + 1, m_run, res)
        per = -(-len(vector) // max(len(matmuls), 1))
        for n, mm in enumerate(matmuls):
            mm()
            for piece in vector[n * per:(n + 1) * per]:
                piece()
        for piece in vector[len(matmuls) * per:]:
            piece()
        if t >= 0:
            acc_run = acc_new
            if items[t][3]:
                i = items[t][0]
                ot = jnp.concatenate(
                    [acc[:HEAD_DIM] / acc[HEAD_DIM:HEAD_DIM + 1] for acc in acc_run],
                    axis=0)
                o_ref[i * tq:(i + 1) * tq, :] = ot.T.astype(BF16)
                acc_run = acc_fresh
        if 0 <= t + 1 < n_items:
            m_run = m_fresh if items[t + 1][3] else [r[0] for r in res]
            alpha[t + 1] = [r[1] for r in res]


def _fox_attn(qt, k, vt, caug, weights, bsz, seq, tq, tk):
    tok = pl.BlockSpec((seq, LANES), lambda b, p: (b, p))
    feat = pl.BlockSpec((LANES, seq), lambda b, p: (p, b))
    steps = bsz * N_PAIRS
    flat = [w.reshape(-1, w.shape[-1]) for w in weights]
    slabs = [pl.BlockSpec((w.shape[0] // steps, w.shape[1]),
                          lambda b, p: (b * N_PAIRS + p, 0)) for w in flat]
    outs = pl.pallas_call(
        functools.partial(_fox_attn_kernel, tq=tq, tk=tk, seq=seq,
                          n_cast=len(flat)),
        grid=(bsz, N_PAIRS),
        in_specs=[feat, tok, feat,
                  pl.BlockSpec((seq, LANES), lambda b, p: (b, 0))] + slabs,
        out_specs=[tok] + slabs,
        out_shape=[jax.ShapeDtypeStruct(k.shape, BF16)]
        + [jax.ShapeDtypeStruct(w.shape, BF16) for w in flat],
        scratch_shapes=[pltpu.VMEM((3, 2, tk, tq), F32),
                        pltpu.VMEM((2, 2, tk, tq), BF16)],
        compiler_params=pltpu.CompilerParams(
            dimension_semantics=("parallel", "parallel")),
        name="fox_attn",
    )(qt, k, vt, caug, *flat)
    return outs[0], [wb.reshape(w.shape) for wb, w in zip(outs[1:], weights)]


def _post_mlp_kernel(*refs, emit_next, ff_chunk):
    if emit_next:
        (h_ref, o_ref, wo_ref, gm_ref, wup_ref, wdn_ref,
         ga_ref, gkv_ref, wq_ref, wkv_ref, gq_ref, gk_ref,
         out_ref, qb_ref, kd_ref, vt_ref) = refs
    else:
        h_ref, o_ref, wo_ref, gm_ref, wup_ref, wdn_ref, out_ref = refs
    h1 = h_ref[...] + jnp.dot(o_ref[...], wo_ref[...], preferred_element_type=F32)
    m = ((h1 * _rms_scale(h1)) * gm_ref[...]).astype(BF16)
    acc = h1
    for c in range(D_FF // ff_chunk):
        u = jnp.dot(m, wup_ref[:, c * ff_chunk:(c + 1) * ff_chunk],
                    preferred_element_type=F32)
        u = jnp.square(jnp.maximum(u, 0.0)).astype(BF16)
        acc = acc + jnp.dot(u, wdn_ref[c * ff_chunk:(c + 1) * ff_chunk, :],
                            preferred_element_type=F32)
    out_ref[...] = acc
    if emit_next:
        hn = acc * _rms_scale(acc)
        a = (hn * ga_ref[...]).astype(BF16)
        for c in range(D_MODEL // MXU_N):
            y = jnp.dot(a, wq_ref[:, c * MXU_N:(c + 1) * MXU_N],
                        preferred_element_type=F32)
            for half in range(MXU_N // LANES):
                yy = y[:, half * LANES:(half + 1) * LANES]
                dst = pl.ds(c * MXU_N + half * LANES, LANES)
                qb_ref[:, dst] = (_head_pair_norm(yy, gq_ref[...])
                                  * (QK_SCALE * LOG2_E)).astype(BF16)
        kvn = (hn * gkv_ref[...]).astype(BF16)
        kv = jnp.dot(kvn, wkv_ref[...], preferred_element_type=F32)
        for j in range(KV_HEADS):
            sl = slice(j * LANES, (j + 1) * LANES)
            kd_ref[:, sl] = _head_pair_norm(kv[:, sl], gk_ref[...]).astype(BF16)
        vt_ref[...] = kv[:, KV_HEADS * LANES:].T.astype(BF16)


def _post_mlp(h, o, wo, gm, wup, wdn, layer, nxt, tm, ff_chunk=1024):
    t = h.shape[0]
    tok = lambda w: pl.BlockSpec((tm, w), lambda i: (i, 0))
    in_specs = [tok(D_MODEL), tok(D_MODEL), _layer((D_MODEL, D_MODEL), 0),
                _resident((1, D_MODEL)), _layer((D_MODEL, D_FF), layer),
                _layer((D_FF, D_MODEL), layer)]
    out_specs = [tok(D_MODEL)]
    out_shape = [jax.ShapeDtypeStruct((t, D_MODEL), F32)]
    args = [h, o, wo, gm, wup, wdn]
    if nxt is not None:
        kvw = KV_HEADS * LANES + KV_HEADS * HEAD_DIM
        in_specs += [_resident((1, D_MODEL)), _resident((1, D_MODEL)),
                     _layer((D_MODEL, D_MODEL), 0), _resident((D_MODEL, kvw)),
                     _resident((1, LANES)), _resident((1, LANES))]
        out_specs += [tok(D_MODEL), tok(KV_HEADS * LANES),
                      pl.BlockSpec((KV_HEADS * HEAD_DIM, tm), lambda i: (0, i))]
        out_shape += [jax.ShapeDtypeStruct((t, D_MODEL), BF16),
                      jax.ShapeDtypeStruct((t, KV_HEADS * LANES), BF16),
                      jax.ShapeDtypeStruct((KV_HEADS * HEAD_DIM, t), BF16)]
        args += list(nxt)
    return pl.pallas_call(
        functools.partial(_post_mlp_kernel, emit_next=nxt is not None,
                          ff_chunk=ff_chunk),
        grid=(t // tm,),
        in_specs=in_specs, out_specs=out_specs, out_shape=out_shape,
        compiler_params=pltpu.CompilerParams(
            dimension_semantics=("parallel",), vmem_limit_bytes=VMEM_LIMIT),
        name="post_mlp_next" if nxt is not None else "post_mlp",
    )(*args)


def _t5_causal_bucket(dist):
    n = np.maximum(dist, 0)
    max_exact = N_BUCKETS // 2
    large = max_exact + (np.log(np.maximum(n, 1) / max_exact)
                         / np.log(REL_MAX_DIST / max_exact)
                         * (N_BUCKETS - max_exact)).astype(np.int32)
    large = np.minimum(large, N_BUCKETS - 1)
    return np.where(n < max_exact, n, large).astype(np.int32)


def _bucket_ranges():
    buckets = _t5_causal_bucket(np.arange(WINDOW))
    assert np.all(np.diff(buckets) >= 0)
    out = []
    for k in np.unique(buckets):
        idx = np.nonzero(buckets == k)[0]
        out.append((int(k), int(idx[0]), int(idx[-1]) + 1))
    return out


def _swa_attn_kernel(sink_ref, rb_ref, q_ref, kp_ref, kc_ref, vtp_ref, vtc_ref,
                     o_ref, bias_sc, st_sc, e_sc, *, n_blocks):
    n = pl.program_id(1)
    band = 2 * WINDOW

    @pl.when((pl.program_id(0) == 0) & (n == 0))
    def _():
        krow = lax.broadcasted_iota(jnp.int32, (band, WINDOW), 0)
        qcol = lax.broadcasted_iota(jnp.int32, (band, WINDOW), 1)
        dist = qcol + WINDOW - krow
        for h in range(N_HEADS):
            tile = jnp.full((band, WINDOW), NEG_INF, F32)
            for k, lo, hi in _bucket_ranges():
                tile = jnp.where((dist >= lo) & (dist < hi),
                                 rb_ref[k, h] * LOG2_E, tile)
            cols = slice((h % 2) * WINDOW, (h % 2 + 1) * WINDOW)
            bias_sc[0, h // 2, :, cols] = tile
            bias_sc[1, h // 2, :, cols] = jnp.where(krow < WINDOW, NEG_INF, tile)

    first = (n == 0).astype(jnp.int32)
    keys = jnp.concatenate([kp_ref[...], kc_ref[...]], axis=0)
    vts = jnp.concatenate([vtp_ref[...], vtc_ref[...]], axis=1)
    lane = lax.broadcasted_iota(jnp.int32, (1, LANES), 1)
    col = lax.broadcasted_iota(jnp.int32, (1, band), 1)
    pairs_per_kv = N_PAIRS // KV_HEADS
    items = [(j, p) for j in range(n_blocks) for p in range(N_PAIRS)]

    def scores(item, slot):
        j, p = item
        g = p // pairs_per_kv
        q2 = q_ref[j * WINDOW:(j + 1) * WINDOW, p * LANES:(p + 1) * LANES]
        zero = jnp.zeros_like(q2)
        qs = jnp.concatenate([jnp.where(lane < HEAD_DIM, q2, zero),
                              jnp.where(lane >= HEAD_DIM, q2, zero)], axis=0)
        st_sc[slot] = lax.dot_general(
            keys[j * WINDOW:j * WINDOW + band, g * LANES:(g + 1) * LANES], qs,
            (((1,), (1,)), ((), ())), preferred_element_type=F32)

    def softmax(item, st_slot, slot):
        j, p = item
        st = st_sc[st_slot] + bias_sc[first if j == 0 else 0, p]
        sink = jnp.where(col < WINDOW, sink_ref[2 * p], sink_ref[2 * p + 1]) * LOG2_E
        m = jnp.maximum(_reduce_rows(st, jnp.maximum), sink)
        e_sc[slot] = jnp.exp2(st - m).astype(BF16)
        return jnp.exp2(sink - m)

    def values(item, slot, sink_term):
        j, p = item
        g = p // pairs_per_kv
        vt1 = jnp.concatenate(
            [vts[g * HEAD_DIM:(g + 1) * HEAD_DIM, j * WINDOW:j * WINDOW + band],
             jnp.ones((SUM_ROWS, band), BF16)], axis=0)
        acc = jnp.dot(vt1, e_sc[slot], preferred_element_type=F32)
        ot = acc[:HEAD_DIM] / (acc[HEAD_DIM:HEAD_DIM + 1] + sink_term)
        o2 = jnp.concatenate([ot[:, :WINDOW], ot[:, WINDOW:]], axis=0).T
        o_ref[j * WINDOW:(j + 1) * WINDOW, p * LANES:(p + 1) * LANES] = o2.astype(BF16)

    for t in range(3):
        scores(items[t], t)
    sink_term = softmax(items[0], 0, 0)
    for t, item in enumerate(items):
        if t + 3 < len(items):
            scores(items[t + 3], t % 3)
        values(item, t % 2, sink_term)
        if t + 1 < len(items):
            sink_term = softmax(items[t + 1], (t + 1) % 3, (t + 1) % 2)


def _swa_attn(sinks, rel_bias, q, kd, vt, bsz, seq, n_blocks):
    nblk = seq // WINDOW
    steps = nblk // n_blocks
    kvw = KV_HEADS * LANES
    vtw = KV_HEADS * HEAD_DIM
    tq = n_blocks * WINDOW
    prev_block = lambda b, n: b * nblk + jnp.maximum(n * n_blocks - 1, 0)
    smem = pl.BlockSpec(memory_space=pltpu.SMEM)
    return pl.pallas_call(
        functools.partial(_swa_attn_kernel, n_blocks=n_blocks),
        grid=(bsz, steps),
        in_specs=[
            smem, smem,
            pl.BlockSpec((tq, D_MODEL), lambda b, n: (b * steps + n, 0)),
            pl.BlockSpec((WINDOW, kvw), lambda b, n: (prev_block(b, n), 0)),
            pl.BlockSpec((tq, kvw), lambda b, n: (b * steps + n, 0)),
            pl.BlockSpec((vtw, WINDOW), lambda b, n: (0, prev_block(b, n))),
            pl.BlockSpec((vtw, tq), lambda b, n: (0, b * steps + n)),
        ],
        out_specs=pl.BlockSpec((tq, D_MODEL), lambda b, n: (b * steps + n, 0)),
        out_shape=jax.ShapeDtypeStruct((bsz * seq, D_MODEL), BF16),
        scratch_shapes=[pltpu.VMEM((2, N_PAIRS, 2 * WINDOW, 2 * WINDOW), F32),
                        pltpu.VMEM((3, 2 * WINDOW, 2 * WINDOW), F32),
                        pltpu.VMEM((2, 2 * WINDOW, 2 * WINDOW), BF16)],
        compiler_params=pltpu.CompilerParams(
            dimension_semantics=("arbitrary", "arbitrary")),
        name="swa_attn",
    )(sinks, rel_bias, q, kd, kd, vt, vt)


def _dup_heads(w):
    d = w.shape[0]
    w = w.reshape(d, KV_HEADS, 1, HEAD_DIM)
    return jnp.broadcast_to(w, (d, KV_HEADS, 2, HEAD_DIM)).reshape(d, KV_HEADS * LANES)


def _lane_groups(v):
    return jnp.pad(jnp.tile(v, N_SPLIT), (0, LANES - N_SPLIT * N_HEADS)).reshape(1, LANES)


def kernel(x, g_attn, g_mlp, w_in_a, b_f, gq_a, gk_a, w_out_a, g_kv, w_kv, gk_b,
           w_q_b, gq_b, sinks, rel_bias, w_out_b, w_up, w_down):
    bsz, seq, d = x.shape
    t = bsz * seq
    hw = N_HEADS * HEAD_DIM
    kvw = KV_HEADS * HEAD_DIM
    row = lambda g: g.reshape(1, -1).astype(F32)
    pair = lambda g: jnp.tile(g.astype(F32), 2).reshape(1, LANES)

    x2 = x.reshape(t, d)
    wf = jnp.pad(jnp.tile(w_in_a[0, :, 3 * hw:], (1, N_SPLIT)),
                 ((0, 0), (0, LANES - N_SPLIT * N_HEADS))).astype(BF16)
    qt, k, vt, f3 = _fox_pre(x2, row(g_attn[0]), w_in_a[0, :, :3 * hw].astype(BF16), wf,
                            pair(gq_a[0]), pair(gk_a[0]), tm=512)
    caug = _fox_decay(f3, _lane_groups(b_f[0].astype(F32)), bsz, seq)
    o, (w_up_b, w_down_b, w_out_a_b, w_out_b_b, w_q_b_b) = _fox_attn(
        qt, k, vt, caug, [w_up, w_down, w_out_a, w_out_b, w_q_b], bsz, seq,
        tq=256, tk=512)

    wkv = jnp.concatenate([_dup_heads(w_kv[:, :kvw]), w_kv[:, kvw:]],
                          axis=1).astype(BF16)
    nxt = (row(g_attn[1]), row(g_kv), w_q_b_b, wkv, pair(gq_b[0]), pair(gk_b))
    h, qb, kd, vt_b = _post_mlp(x2, o, w_out_a_b, row(g_mlp[0]),
                                w_up_b, w_down_b, 0, nxt, tm=512)

    o2 = _swa_attn(sinks[0].astype(F32), rel_bias.astype(F32), qb, kd, vt_b,
                   bsz, seq, n_blocks=8)
    (out,) = _post_mlp(h, o2, w_out_b_b, row(g_mlp[1]),
                       w_up_b, w_down_b, 1, None, tm=512)
    return out.reshape(bsz, seq, d)
```

```python
import functools

import numpy as np
import jax
import jax.numpy as jnp
from jax import lax
from jax.experimental import pallas as pl
from jax.experimental.pallas import tpu as pltpu

D_MODEL = 1024
HEAD_DIM = 64
N_HEADS = 16
N_PAIRS = N_HEADS // 2
KV_HEADS = 2
WINDOW = 128
D_FF = 4 * D_MODEL
N_BUCKETS = 32
REL_MAX_DIST = 128
NORM_EPS = 1e-6
LANES = 128
MXU_N = 256
QK_SCALE = HEAD_DIM ** -0.5
LOG2_E = 1.4426950408889634
N_SPLIT = 3
SUM_ROWS = 16
ROW_CHUNK = 32

F32 = jnp.float32
BF16 = jnp.bfloat16
NEG_INF = float("-inf")
VMEM_LIMIT = 56 * 1024 * 1024


def _rms_scale(x):
    return lax.rsqrt(jnp.mean(x * x, axis=-1, keepdims=True) + NORM_EPS)


def _head_pair_norm(y, g2):
    lo = lax.broadcasted_iota(jnp.int32, (1, LANES), 1) < HEAD_DIM
    sq = y * y
    s_lo = jnp.sum(jnp.where(lo, sq, 0.0), axis=-1, keepdims=True)
    s_hi = jnp.sum(jnp.where(lo, 0.0, sq), axis=-1, keepdims=True)
    ms = jnp.where(lo, s_lo, s_hi) * (1.0 / HEAD_DIM)
    return (y * lax.rsqrt(ms + NORM_EPS)) * g2


def _split_bf16(x):
    terms = []
    for _ in range(N_SPLIT):
        t = x.astype(BF16)
        terms.append(t)
        x = x - t.astype(F32)
    return terms


def _reduce_rows(x, op, group=32):
    acc = x[0:group]
    for r in range(group, x.shape[0], group):
        acc = op(acc, x[r:r + group])
    red = jnp.max if op is jnp.maximum else jnp.sum
    return red(acc, axis=0, keepdims=True)


def _resident(shape):
    zeros = (0,) * len(shape)
    return pl.BlockSpec(shape, lambda *_: zeros, pipeline_mode=pl.Buffered(1))


def _layer(shape, layer):
    index = (layer,) + (0,) * len(shape)
    return pl.BlockSpec((None,) + tuple(shape), lambda *_: index,
                        pipeline_mode=pl.Buffered(1))


def _fox_pre_kernel(x_ref, g_ref, wqkv_ref, wf_ref, gq_ref, gk_ref,
                    qt_ref, k_ref, vt_ref, f_ref):
    x = x_ref[...]
    hn = ((x * _rms_scale(x)) * g_ref[...]).astype(BF16)
    n_chunks = D_MODEL // MXU_N
    for c in range(3 * n_chunks):
        y = jnp.dot(hn, wqkv_ref[:, c * MXU_N:(c + 1) * MXU_N],
                    preferred_element_type=F32)
        part, cc = divmod(c, n_chunks)
        rows = slice(cc * MXU_N, (cc + 1) * MXU_N)
        if part == 2:
            vt_ref[rows, :] = y.T.astype(BF16)
            continue
        g2 = gq_ref[...] if part == 0 else gk_ref[...]
        yn = jnp.concatenate(
            [_head_pair_norm(y[:, h * LANES:(h + 1) * LANES], g2)
             for h in range(MXU_N // LANES)], axis=1)
        if part == 0:
            qt_ref[rows, :] = (yn * QK_SCALE).T.astype(BF16)
        else:
            for h in range(MXU_N // LANES):
                k_ref[cc * (MXU_N // LANES) + h] = yn[:, h * LANES:(h + 1) * LANES].astype(BF16)
    f_ref[...] = jnp.dot(hn, wf_ref[...], preferred_element_type=F32)


def _fox_pre(x2, g, w_in, wf, gq2, gk2, tm):
    t = x2.shape[0]
    tok = lambda w: pl.BlockSpec((tm, w), lambda i: (i, 0))
    return pl.pallas_call(
        _fox_pre_kernel,
        grid=(t // tm,),
        in_specs=[tok(D_MODEL), _resident((1, D_MODEL)),
                  _resident(w_in.shape), _resident((D_MODEL, LANES)),
                  _resident((1, LANES)), _resident((1, LANES))],
        out_specs=[pl.BlockSpec((D_MODEL, tm), lambda i: (0, i)),
                   pl.BlockSpec((N_PAIRS, tm, LANES), lambda i: (0, i, 0)),
                   pl.BlockSpec((D_MODEL, tm), lambda i: (0, i)), tok(LANES)],
        out_shape=[jax.ShapeDtypeStruct((D_MODEL, t), BF16),
                   jax.ShapeDtypeStruct((N_PAIRS, t, LANES), BF16),
                   jax.ShapeDtypeStruct((D_MODEL, t), BF16),
                   jax.ShapeDtypeStruct((t, LANES), F32)],
        compiler_params=pltpu.CompilerParams(
            dimension_semantics=("parallel",), vmem_limit_bytes=VMEM_LIMIT),
        name="fox_pre",
    )(x2, g, w_in, wf, gq2, gk2)


def _fox_decay_kernel(f_ref, b_ref, c_ref, *, blk):
    seq = f_ref.shape[0]
    lane = lax.broadcasted_iota(jnp.int32, (1, LANES), 1)
    r = lax.broadcasted_iota(jnp.int32, (blk, blk), 0)
    c = lax.broadcasted_iota(jnp.int32, (blk, blk), 1)
    lower = jnp.where(r >= c, 1.0, 0.0).astype(BF16)
    carry = jnp.zeros((1, LANES), F32)
    for b in range(seq // blk):
        rows = slice(b * blk, (b + 1) * blk)
        x = f_ref[rows, :] + b_ref[...]
        log_f = -(jnp.maximum(-x, 0.0) + jnp.log1p(jnp.exp(-jnp.abs(x))))
        cb = carry
        for term in _split_bf16(log_f):
            cb = cb + jnp.dot(lower, term, preferred_element_type=F32)
        carry = cb[blk - 1:blk, :]
        out = jnp.zeros((blk, LANES), BF16)
        for t, term in reversed(list(enumerate(_split_bf16(cb)))):
            out = jnp.where(lane < (t + 1) * N_HEADS, term, out)
        c_ref[rows, :] = out


def _fox_decay(f3, b3, bsz, seq):
    spec = pl.BlockSpec((seq, LANES), lambda b: (b, 0))
    return pl.pallas_call(
        functools.partial(_fox_decay_kernel, blk=MXU_N),
        grid=(bsz,),
        in_specs=[spec, _resident((1, LANES))],
        out_specs=spec,
        out_shape=jax.ShapeDtypeStruct(f3.shape, BF16),
        compiler_params=pltpu.CompilerParams(dimension_semantics=("parallel",)),
        name="fox_decay",
    )(f3, b3)


def _fox_attn_kernel(qt_ref, k_ref, vt_ref, c_ref, *rest, tq, tk, seq, n_cast):
    w_refs, o_ref = rest[:n_cast], rest[n_cast]
    wb_refs = rest[n_cast + 1:2 * n_cast + 1]
    st_sc, pt_sc = rest[2 * n_cast + 1:]
    for w_ref, wb_ref in zip(w_refs, wb_refs):
        wb_ref[...] = w_ref[...].astype(BF16)
    p = pl.program_id(1)
    feat = lax.broadcasted_iota(jnp.int32, (LANES, 1), 0)
    own = [feat < HEAD_DIM, feat >= HEAD_DIM]
    neg = [jnp.broadcast_to(
        jnp.where((feat < N_SPLIT * N_HEADS)
                  & ((feat & (N_HEADS - 1)) == 2 * p + hh), -1.0, 0.0).astype(BF16),
        (LANES, tq)) for hh in range(2)]

    items = []
    for i in range(seq // tq):
        n_keys = (i + 1) * tq
        starts = list(range(0, n_keys, tk))
        for ks in starts:
            items.append((i, ks, min(tk, n_keys - ks), ks == starts[-1]))

    def query_blocks(i):
        q2 = qt_ref[:, i * tq:(i + 1) * tq]
        return [jnp.concatenate(
            [jnp.where(own[hh], q2, jnp.zeros_like(q2)), neg[hh]], axis=0)
            for hh in range(2)]

    def scores(item, qa, slot):
        _, ks, width, _ = item
        ka = jnp.concatenate([k_ref[ks:ks + width, :], c_ref[ks:ks + width, :]],
                             axis=1)

        def one(hh):
            st_sc[slot, hh, 0:width, :] = jnp.dot(
                ka, qa[hh], preferred_element_type=F32)
        return [functools.partial(one, hh) for hh in range(2)]

    def probabilities(item, t, m_old, res):
        i, ks, width, is_last = item
        chunks = list(range(0, width, ROW_CHUNK))
        if is_last:
            krow = lax.broadcasted_iota(jnp.int32, (ROW_CHUNK, tq), 0)
            qcol = lax.broadcasted_iota(jnp.int32, (ROW_CHUNK, tq), 1)
            diag = qcol - krow + (i * tq - ks)
        pieces = []
        for hh in range(2):
            src = st_sc.at[t % 3, hh]

            def col_max(hh=hh, src=src):
                acc = None
                for r in chunks:
                    x = src[r:r + ROW_CHUNK, :]
                    if is_last:
                        x = jnp.where(diag >= r, x, NEG_INF)
                        src[r:r + ROW_CHUNK, :] = x
                    acc = x if acc is None else jnp.maximum(acc, x)
                m_new = jnp.maximum(m_old[hh], jnp.max(acc, axis=0, keepdims=True))
                res[hh] = (m_new, jnp.exp(m_old[hh] - m_new))

            def exps(rows, hh=hh, src=src):
                for r in rows:
                    pt_sc[t % 2, hh, r:r + ROW_CHUNK, :] = jnp.exp(
                        src[r:r + ROW_CHUNK, :] - res[hh][0]).astype(BF16)

            half = len(chunks) // 2
            pieces += [col_max, functools.partial(exps, chunks[:half]),
                       functools.partial(exps, chunks[half:])]
        return pieces

    def values(item, t, alpha, acc_old, new):
        _, ks, width, _ = item

        def one(hh):
            vt = jnp.concatenate(
                [vt_ref[hh * HEAD_DIM:(hh + 1) * HEAD_DIM, ks:ks + width],
                 jnp.ones((SUM_ROWS, width), BF16)], axis=0)
            new[hh] = alpha[hh] * acc_old[hh] + jnp.dot(
                vt, pt_sc[t % 2, hh, 0:width, :],
                preferred_element_type=F32)
        return [functools.partial(one, hh) for hh in range(2)]

    n_items = len(items)
    m_fresh = [jnp.full((1, tq), NEG_INF, F32)] * 2
    acc_fresh = [jnp.zeros((HEAD_DIM + SUM_ROWS, tq), F32)] * 2
    alpha = {}
    qa_tile, qa = -1, None
    m_run, acc_run = m_fresh, acc_fresh
    for t in range(-2, n_items):
        matmuls, vector = [], []
        if t + 2 < n_items:
            nxt = items[t + 2]
            if nxt[0] != qa_tile:
                qa_tile, qa = nxt[0], query_blocks(nxt[0])
            matmuls += scores(nxt, qa, (t + 2) % 3)
        acc_new = [None, None]
        if t >= 0:
            matmuls += values(items[t], t, alpha.pop(t), acc_run, acc_new)
        res = [None, None]
        if 0 <= t + 1 < n_items:
            vector = probabilities(items[t + 1], t + 1, m_run, res)
        per = -(-len(vector) // max(len(matmuls), 1))
        for n, mm in enumerate(matmuls):
            mm()
            for piece in vector[n * per:(n + 1) * per]:
                piece()
        for piece in vector[len(matmuls) * per:]:
            piece()
        if t >= 0:
            acc_run = acc_new
            if items[t][3]:
                i = items[t][0]
                ot = jnp.concatenate(
                    [acc[:HEAD_DIM] / acc[HEAD_DIM:HEAD_DIM + 1] for acc in acc_run],
                    axis=0)
                o_ref[i * tq:(i + 1) * tq, :] = ot.T.astype(BF16)
                acc_run = acc_fresh
        if 0 <= t + 1 < n_items:
            m_run = m_fresh if items[t + 1][3] else [r[0] for r in res]
            alpha[t + 1] = [r[1] for r in res]


def _fox_attn(qt, k, vt, caug, weights, bsz, seq, tq, tk):
    tok = pl.BlockSpec((None, seq, LANES), lambda b, p: (p, b, 0))
    feat = pl.BlockSpec((LANES, seq), lambda b, p: (p, b))
    steps = bsz * N_PAIRS
    flat = [w.reshape(-1, w.shape[-1]) for w in weights]
    slabs = [pl.BlockSpec((w.shape[0] // steps, w.shape[1]),
                          lambda b, p: (b * N_PAIRS + p, 0)) for w in flat]
    outs = pl.pallas_call(
        functools.partial(_fox_attn_kernel, tq=tq, tk=tk, seq=seq,
                          n_cast=len(flat)),
        grid=(bsz, N_PAIRS),
        in_specs=[feat, tok, feat,
                  pl.BlockSpec((seq, LANES), lambda b, p: (b, 0))] + slabs,
        out_specs=[tok] + slabs,
        out_shape=[jax.ShapeDtypeStruct(k.shape, BF16)]
        + [jax.ShapeDtypeStruct(w.shape, BF16) for w in flat],
        scratch_shapes=[pltpu.VMEM((3, 2, tk, tq), F32),
                        pltpu.VMEM((2, 2, tk, tq), BF16)],
        compiler_params=pltpu.CompilerParams(
            dimension_semantics=("parallel", "parallel")),
        name="fox_attn",
    )(qt, k, vt, caug, *flat)
    return outs[0], [wb.reshape(w.shape) for wb, w in zip(outs[1:], weights)]


def _post_mlp_kernel(*refs, emit_next, o_by_pair, ff_chunk):
    if emit_next:
        (h_ref, o_ref, wo_ref, gm_ref, wup_ref, wdn_ref,
         ga_ref, gkv_ref, wq_ref, wkv_ref, gq_ref, gk_ref,
         out_ref, qb_ref, kd_ref, vt_ref) = refs
    else:
        h_ref, o_ref, wo_ref, gm_ref, wup_ref, wdn_ref, out_ref = refs
    if o_by_pair:
        o = jnp.concatenate([o_ref[p] for p in range(N_PAIRS)], axis=1)
    else:
        o = o_ref[...]
    h1 = h_ref[...] + jnp.dot(o, wo_ref[...], preferred_element_type=F32)
    m = ((h1 * _rms_scale(h1)) * gm_ref[...]).astype(BF16)
    acc = h1
    for c in range(D_FF // ff_chunk):
        u = jnp.dot(m, wup_ref[:, c * ff_chunk:(c + 1) * ff_chunk],
                    preferred_element_type=F32)
        u = jnp.square(jnp.maximum(u, 0.0)).astype(BF16)
        acc = acc + jnp.dot(u, wdn_ref[c * ff_chunk:(c + 1) * ff_chunk, :],
                            preferred_element_type=F32)
    out_ref[...] = acc
    if emit_next:
        hn = acc * _rms_scale(acc)
        a = (hn * ga_ref[...]).astype(BF16)
        for c in range(D_MODEL // MXU_N):
            y = jnp.dot(a, wq_ref[:, c * MXU_N:(c + 1) * MXU_N],
                        preferred_element_type=F32)
            for half in range(MXU_N // LANES):
                yy = y[:, half * LANES:(half + 1) * LANES]
                dst = pl.ds(c * MXU_N + half * LANES, LANES)
                qb_ref[:, dst] = (_head_pair_norm(yy, gq_ref[...])
                                  * (QK_SCALE * LOG2_E)).astype(BF16)
        kvn = (hn * gkv_ref[...]).astype(BF16)
        kv = jnp.dot(kvn, wkv_ref[...], preferred_element_type=F32)
        for j in range(KV_HEADS):
            sl = slice(j * LANES, (j + 1) * LANES)
            kd_ref[:, sl] = _head_pair_norm(kv[:, sl], gk_ref[...]).astype(BF16)
        vt_ref[...] = kv[:, KV_HEADS * LANES:].T.astype(BF16)


def _post_mlp(h, o, wo, gm, wup, wdn, layer, nxt, tm, ff_chunk=1024):
    t = h.shape[0]
    tok = lambda w: pl.BlockSpec((tm, w), lambda i: (i, 0))
    o_by_pair = o.ndim == 3
    o_spec = (pl.BlockSpec((N_PAIRS, tm, LANES), lambda i: (0, i, 0))
              if o_by_pair else tok(D_MODEL))
    in_specs = [tok(D_MODEL), o_spec, _layer((D_MODEL, D_MODEL), 0),
                _resident((1, D_MODEL)), _layer((D_MODEL, D_FF), layer),
                _layer((D_FF, D_MODEL), layer)]
    out_specs = [tok(D_MODEL)]
    out_shape = [jax.ShapeDtypeStruct((t, D_MODEL), F32)]
    args = [h, o, wo, gm, wup, wdn]
    if nxt is not None:
        kvw = KV_HEADS * LANES + KV_HEADS * HEAD_DIM
        in_specs += [_resident((1, D_MODEL)), _resident((1, D_MODEL)),
                     _layer((D_MODEL, D_MODEL), 0), _resident((D_MODEL, kvw)),
                     _resident((1, LANES)), _resident((1, LANES))]
        out_specs += [tok(D_MODEL), tok(KV_HEADS * LANES),
                      pl.BlockSpec((KV_HEADS * HEAD_DIM, tm), lambda i: (0, i))]
        out_shape += [jax.ShapeDtypeStruct((t, D_MODEL), BF16),
                      jax.ShapeDtypeStruct((t, KV_HEADS * LANES), BF16),
                      jax.ShapeDtypeStruct((KV_HEADS * HEAD_DIM, t), BF16)]
        args += list(nxt)
    return pl.pallas_call(
        functools.partial(_post_mlp_kernel, emit_next=nxt is not None,
                          o_by_pair=o_by_pair, ff_chunk=ff_chunk),
        grid=(t // tm,),
        in_specs=in_specs, out_specs=out_specs, out_shape=out_shape,
        compiler_params=pltpu.CompilerParams(
            dimension_semantics=("parallel",), vmem_limit_bytes=VMEM_LIMIT),
        name="post_mlp_next" if nxt is not None else "post_mlp",
    )(*args)


def _t5_causal_bucket(dist):
    n = np.maximum(dist, 0)
    max_exact = N_BUCKETS // 2
    large = max_exact + (np.log(np.maximum(n, 1) / max_exact)
                         / np.log(REL_MAX_DIST / max_exact)
                         * (N_BUCKETS - max_exact)).astype(np.int32)
    large = np.minimum(large, N_BUCKETS - 1)
    return np.where(n < max_exact, n, large).astype(np.int32)


def _bucket_ranges():
    buckets = _t5_causal_bucket(np.arange(WINDOW))
    assert np.all(np.diff(buckets) >= 0)
    out = []
    for k in np.unique(buckets):
        idx = np.nonzero(buckets == k)[0]
        out.append((int(k), int(idx[0]), int(idx[-1]) + 1))
    return out


def _swa_attn_kernel(sink_ref, rb_ref, q_ref, kp_ref, kc_ref, vtp_ref, vtc_ref,
                     o_ref, bias_sc, st_sc, e_sc, *, n_blocks):
    n = pl.program_id(1)
    band = 2 * WINDOW

    @pl.when((pl.program_id(0) == 0) & (n == 0))
    def _():
        krow = lax.broadcasted_iota(jnp.int32, (band, WINDOW), 0)
        qcol = lax.broadcasted_iota(jnp.int32, (band, WINDOW), 1)
        dist = qcol + WINDOW - krow
        for h in range(N_HEADS):
            tile = jnp.full((band, WINDOW), NEG_INF, F32)
            for k, lo, hi in _bucket_ranges():
                tile = jnp.where((dist >= lo) & (dist < hi),
                                 rb_ref[k, h] * LOG2_E, tile)
            cols = slice((h % 2) * WINDOW, (h % 2 + 1) * WINDOW)
            bias_sc[0, h // 2, :, cols] = tile
            bias_sc[1, h // 2, :, cols] = jnp.where(krow < WINDOW, NEG_INF, tile)

    first = (n == 0).astype(jnp.int32)
    keys = jnp.concatenate([kp_ref[...], kc_ref[...]], axis=0)
    vts = jnp.concatenate([vtp_ref[...], vtc_ref[...]], axis=1)
    lane = lax.broadcasted_iota(jnp.int32, (1, LANES), 1)
    col = lax.broadcasted_iota(jnp.int32, (1, band), 1)
    pairs_per_kv = N_PAIRS // KV_HEADS
    items = [(j, p) for j in range(n_blocks) for p in range(N_PAIRS)]

    def scores(item, slot):
        j, p = item
        g = p // pairs_per_kv
        q2 = q_ref[j * WINDOW:(j + 1) * WINDOW, p * LANES:(p + 1) * LANES]
        zero = jnp.zeros_like(q2)
        qs = jnp.concatenate([jnp.where(lane < HEAD_DIM, q2, zero),
                              jnp.where(lane >= HEAD_DIM, q2, zero)], axis=0)
        st_sc[slot] = lax.dot_general(
            keys[j * WINDOW:j * WINDOW + band, g * LANES:(g + 1) * LANES], qs,
            (((1,), (1,)), ((), ())), preferred_element_type=F32)

    def softmax(item, st_slot, slot):
        j, p = item
        st = st_sc[st_slot] + bias_sc[first if j == 0 else 0, p]
        sink = jnp.where(col < WINDOW, sink_ref[2 * p], sink_ref[2 * p + 1]) * LOG2_E
        m = jnp.maximum(_reduce_rows(st, jnp.maximum), sink)
        e_sc[slot] = jnp.exp2(st - m).astype(BF16)
        return jnp.exp2(sink - m)

    def values(item, slot, sink_term):
        j, p = item
        g = p // pairs_per_kv
        vt1 = jnp.concatenate(
            [vts[g * HEAD_DIM:(g + 1) * HEAD_DIM, j * WINDOW:j * WINDOW + band],
             jnp.ones((SUM_ROWS, band), BF16)], axis=0)
        acc = jnp.dot(vt1, e_sc[slot], preferred_element_type=F32)
        ot = acc[:HEAD_DIM] / (acc[HEAD_DIM:HEAD_DIM + 1] + sink_term)
        o2 = jnp.concatenate([ot[:, :WINDOW], ot[:, WINDOW:]], axis=0).T
        o_ref[j * WINDOW:(j + 1) * WINDOW, p * LANES:(p + 1) * LANES] = o2.astype(BF16)

    for t in range(3):
        scores(items[t], t)
    sink_term = softmax(items[0], 0, 0)
    for t, item in enumerate(items):
        if t + 3 < len(items):
            scores(items[t + 3], t % 3)
        values(item, t % 2, sink_term)
        if t + 1 < len(items):
            sink_term = softmax(items[t + 1], (t + 1) % 3, (t + 1) % 2)


def _swa_attn(sinks, rel_bias, q, kd, vt, bsz, seq, n_blocks):
    nblk = seq // WINDOW
    steps = nblk // n_blocks
    kvw = KV_HEADS * LANES
    vtw = KV_HEADS * HEAD_DIM
    tq = n_blocks * WINDOW
    prev_block = lambda b, n: b * nblk + jnp.maximum(n * n_blocks - 1, 0)
    smem = pl.BlockSpec(memory_space=pltpu.SMEM)
    return pl.pallas_call(
        functools.partial(_swa_attn_kernel, n_blocks=n_blocks),
        grid=(bsz, steps),
        in_specs=[
            smem, smem,
            pl.BlockSpec((tq, D_MODEL), lambda b, n: (b * steps + n, 0)),
            pl.BlockSpec((WINDOW, kvw), lambda b, n: (prev_block(b, n), 0)),
            pl.BlockSpec((tq, kvw), lambda b, n: (b * steps + n, 0)),
            pl.BlockSpec((vtw, WINDOW), lambda b, n: (0, prev_block(b, n))),
            pl.BlockSpec((vtw, tq), lambda b, n: (0, b * steps + n)),
        ],
        out_specs=pl.BlockSpec((tq, D_MODEL), lambda b, n: (b * steps + n, 0)),
        out_shape=jax.ShapeDtypeStruct((bsz * seq, D_MODEL), BF16),
        scratch_shapes=[pltpu.VMEM((2, N_PAIRS, 2 * WINDOW, 2 * WINDOW), F32),
                        pltpu.VMEM((3, 2 * WINDOW, 2 * WINDOW), F32),
                        pltpu.VMEM((2, 2 * WINDOW, 2 * WINDOW), BF16)],
        compiler_params=pltpu.CompilerParams(
            dimension_semantics=("arbitrary", "arbitrary")),
        name="swa_attn",
    )(sinks, rel_bias, q, kd, kd, vt, vt)


def _dup_heads(w):
    d = w.shape[0]
    w = w.reshape(d, KV_HEADS, 1, HEAD_DIM)
    return jnp.broadcast_to(w, (d, KV_HEADS, 2, HEAD_DIM)).reshape(d, KV_HEADS * LANES)


def _lane_groups(v):
    return jnp.pad(jnp.tile(v, N_SPLIT), (0, LANES - N_SPLIT * N_HEADS)).reshape(1, LANES)


def kernel(x, g_attn, g_mlp, w_in_a, b_f, gq_a, gk_a, w_out_a, g_kv, w_kv, gk_b,
           w_q_b, gq_b, sinks, rel_bias, w_out_b, w_up, w_down):
    bsz, seq, d = x.shape
    t = bsz * seq
    hw = N_HEADS * HEAD_DIM
    kvw = KV_HEADS * HEAD_DIM
    row = lambda g: g.reshape(1, -1).astype(F32)
    pair = lambda g: jnp.tile(g.astype(F32), 2).reshape(1, LANES)

    x2 = x.reshape(t, d)
    wf = jnp.pad(jnp.tile(w_in_a[0, :, 3 * hw:], (1, N_SPLIT)),
                 ((0, 0), (0, LANES - N_SPLIT * N_HEADS))).astype(BF16)
    qt, k, vt, f3 = _fox_pre(x2, row(g_attn[0]), w_in_a[0, :, :3 * hw].astype(BF16), wf,
                            pair(gq_a[0]), pair(gk_a[0]), tm=512)
    caug = _fox_decay(f3, _lane_groups(b_f[0].astype(F32)), bsz, seq)
    o, (w_up_b, w_down_b, w_out_a_b, w_out_b_b, w_q_b_b) = _fox_attn(
        qt, k, vt, caug, [w_up, w_down, w_out_a, w_out_b, w_q_b], bsz, seq,
        tq=256, tk=512)

    wkv = jnp.concatenate([_dup_heads(w_kv[:, :kvw]), w_kv[:, kvw:]],
                          axis=1).astype(BF16)
    nxt = (row(g_attn[1]), row(g_kv), w_q_b_b, wkv, pair(gq_b[0]), pair(gk_b))
    h, qb, kd, vt_b = _post_mlp(x2, o, w_out_a_b, row(g_mlp[0]),
                                w_up_b, w_down_b, 0, nxt, tm=512)

    o2 = _swa_attn(sinks[0].astype(F32), rel_bias.astype(F32), qb, kd, vt_b,
                   bsz, seq, n_blocks=8)
    (out,) = _post_mlp(h, o2, w_out_b_b, row(g_mlp[1]),
                       w_up_b, w_down_b, 1, None, tm=512)
    return out.reshape(bsz, seq, d)
```

```python
import functools

import numpy as np
import jax
import jax.numpy as jnp
from jax import lax
from jax.experimental import pallas as pl
from jax.experimental.pallas import tpu as pltpu

D_MODEL = 1024
HEAD_DIM = 64
N_HEADS = 16
N_PAIRS = N_HEADS // 2
KV_HEADS = 2
WINDOW = 128
D_FF = 4 * D_MODEL
N_BUCKETS = 32
REL_MAX_DIST = 128
NORM_EPS = 1e-6
LANES = 128
MXU_N = 256
QK_SCALE = HEAD_DIM ** -0.5
LOG2_E = 1.4426950408889634
N_SPLIT = 3
SUM_ROWS = 16
ROW_CHUNK = 32

F32 = jnp.float32
BF16 = jnp.bfloat16
NEG_INF = float("-inf")
VMEM_LIMIT = 56 * 1024 * 1024


def _rms_scale(x):
    return lax.rsqrt(jnp.mean(x * x, axis=-1, keepdims=True) + NORM_EPS)


def _head_pair_norm(y, g2):
    lo = lax.broadcasted_iota(jnp.int32, (1, LANES), 1) < HEAD_DIM
    sq = y * y
    s_lo = jnp.sum(jnp.where(lo, sq, 0.0), axis=-1, keepdims=True)
    s_hi = jnp.sum(jnp.where(lo, 0.0, sq), axis=-1, keepdims=True)
    ms = jnp.where(lo, s_lo, s_hi) * (1.0 / HEAD_DIM)
    return (y * lax.rsqrt(ms + NORM_EPS)) * g2


def _split_bf16(x):
    terms = []
    for _ in range(N_SPLIT):
        t = x.astype(BF16)
        terms.append(t)
        x = x - t.astype(F32)
    return terms


def _reduce_rows(x, op, group=32):
    acc = x[0:group]
    for r in range(group, x.shape[0], group):
        acc = op(acc, x[r:r + group])
    red = jnp.max if op is jnp.maximum else jnp.sum
    return red(acc, axis=0, keepdims=True)


def _resident(shape):
    zeros = (0,) * len(shape)
    return pl.BlockSpec(shape, lambda *_: zeros, pipeline_mode=pl.Buffered(1))


def _layer(shape, layer):
    index = (layer,) + (0,) * len(shape)
    return pl.BlockSpec((None,) + tuple(shape), lambda *_: index,
                        pipeline_mode=pl.Buffered(1))


def _fox_pre_kernel(x_ref, g_ref, wqkv_ref, wf_ref, gq_ref, gk_ref,
                    qt_ref, k_ref, vt_ref, f_ref):
    x = x_ref[...]
    hn = ((x * _rms_scale(x)) * g_ref[...]).astype(BF16)
    n_chunks = D_MODEL // MXU_N
    for c in range(3 * n_chunks):
        y = jnp.dot(hn, wqkv_ref[:, c * MXU_N:(c + 1) * MXU_N].astype(BF16),
                    preferred_element_type=F32)
        part, cc = divmod(c, n_chunks)
        rows = slice(cc * MXU_N, (cc + 1) * MXU_N)
        if part == 2:
            vt_ref[rows, :] = y.T.astype(BF16)
            continue
        g2 = gq_ref[...] if part == 0 else gk_ref[...]
        yn = jnp.concatenate(
            [_head_pair_norm(y[:, h * LANES:(h + 1) * LANES], g2)
             for h in range(MXU_N // LANES)], axis=1)
        if part == 0:
            qt_ref[rows, :] = (yn * (QK_SCALE * LOG2_E)).T.astype(BF16)
        else:
            for h in range(MXU_N // LANES):
                k_ref[cc * (MXU_N // LANES) + h] = yn[:, h * LANES:(h + 1) * LANES].astype(BF16)
    f_ref[...] = jnp.dot(hn, wf_ref[...], preferred_element_type=F32)


def _fox_pre(x2, g, w_in, wf, gq2, gk2, tm):
    t = x2.shape[0]
    tok = lambda w: pl.BlockSpec((tm, w), lambda i: (i, 0))
    return pl.pallas_call(
        _fox_pre_kernel,
        grid=(t // tm,),
        in_specs=[tok(D_MODEL), _resident((1, D_MODEL)),
                  _layer(w_in.shape[1:], 0), _resident((D_MODEL, LANES)),
                  _resident((1, LANES)), _resident((1, LANES))],
        out_specs=[pl.BlockSpec((D_MODEL, tm), lambda i: (0, i)),
                   pl.BlockSpec((N_PAIRS, tm, LANES), lambda i: (0, i, 0)),
                   pl.BlockSpec((D_MODEL, tm), lambda i: (0, i)), tok(LANES)],
        out_shape=[jax.ShapeDtypeStruct((D_MODEL, t), BF16),
                   jax.ShapeDtypeStruct((N_PAIRS, t, LANES), BF16),
                   jax.ShapeDtypeStruct((D_MODEL, t), BF16),
                   jax.ShapeDtypeStruct((t, LANES), F32)],
        compiler_params=pltpu.CompilerParams(
            dimension_semantics=("parallel",), vmem_limit_bytes=VMEM_LIMIT),
        name="fox_pre",
    )(x2, g, w_in, wf, gq2, gk2)


def _fox_decay_kernel(f_ref, b_ref, c_ref, *, blk):
    seq = f_ref.shape[0]
    lane = lax.broadcasted_iota(jnp.int32, (1, LANES), 1)
    r = lax.broadcasted_iota(jnp.int32, (blk, blk), 0)
    c = lax.broadcasted_iota(jnp.int32, (blk, blk), 1)
    lower = jnp.where(r >= c, 1.0, 0.0).astype(BF16)
    carry = jnp.zeros((1, LANES), F32)
    for b in range(seq // blk):
        rows = slice(b * blk, (b + 1) * blk)
        x = f_ref[rows, :] + b_ref[...]
        log_f = -(jnp.maximum(-x, 0.0) + jnp.log1p(jnp.exp(-jnp.abs(x))))
        cb = carry
        for term in _split_bf16(log_f):
            cb = cb + jnp.dot(lower, term, preferred_element_type=F32)
        carry = cb[blk - 1:blk, :]
        out = jnp.zeros((blk, LANES), BF16)
        for t, term in reversed(list(enumerate(_split_bf16(cb * LOG2_E)))):
            out = jnp.where(lane < (t + 1) * N_HEADS, term, out)
        c_ref[rows, :] = out


def _fox_decay(f3, b3, bsz, seq):
    spec = pl.BlockSpec((seq, LANES), lambda b: (b, 0))
    return pl.pallas_call(
        functools.partial(_fox_decay_kernel, blk=MXU_N),
        grid=(bsz,),
        in_specs=[spec, _resident((1, LANES))],
        out_specs=spec,
        out_shape=jax.ShapeDtypeStruct(f3.shape, BF16),
        compiler_params=pltpu.CompilerParams(dimension_semantics=("parallel",)),
        name="fox_decay",
    )(f3, b3)


def _fox_attn_kernel(qt_ref, k_ref, vt_ref, c_ref, *rest, tq, tk, seq, n_cast):
    w_refs, o_ref = rest[:n_cast], rest[n_cast]
    wb_refs = rest[n_cast + 1:2 * n_cast + 1]
    st_sc, pt_sc = rest[2 * n_cast + 1:]
    for w_ref, wb_ref in zip(w_refs, wb_refs):
        wb_ref[...] = w_ref[...].astype(BF16)
    p = pl.program_id(1)
    feat = lax.broadcasted_iota(jnp.int32, (LANES, 1), 0)
    own = [feat < HEAD_DIM, feat >= HEAD_DIM]
    neg = [jnp.broadcast_to(
        jnp.where((feat < N_SPLIT * N_HEADS)
                  & ((feat & (N_HEADS - 1)) == 2 * p + hh), -1.0, 0.0).astype(BF16),
        (LANES, tq)) for hh in range(2)]

    items = []
    for i in range(seq // tq):
        n_keys = (i + 1) * tq
        starts = list(range(0, n_keys, tk))
        for ks in starts:
            items.append((i, ks, min(tk, n_keys - ks), ks == starts[-1]))

    def query_blocks(i):
        q2 = qt_ref[:, i * tq:(i + 1) * tq]
        return [jnp.concatenate(
            [jnp.where(own[hh], q2, jnp.zeros_like(q2)), neg[hh]], axis=0)
            for hh in range(2)]

    def scores(item, qa, slot):
        _, ks, width, _ = item
        ka = jnp.concatenate([k_ref[ks:ks + width, :], c_ref[ks:ks + width, :]],
                             axis=1)

        def one(hh):
            st_sc[slot, hh, 0:width, :] = jnp.dot(
                ka, qa[hh], preferred_element_type=F32)
        return [functools.partial(one, hh) for hh in range(2)]

    def probabilities(item, t, m_old, res):
        i, ks, width, is_last = item
        chunks = list(range(0, width, ROW_CHUNK))
        if is_last:
            krow = lax.broadcasted_iota(jnp.int32, (ROW_CHUNK, tq), 0)
            qcol = lax.broadcasted_iota(jnp.int32, (ROW_CHUNK, tq), 1)
            diag = qcol - krow + (i * tq - ks)
        pieces = []
        for hh in range(2):
            src = st_sc.at[t % 3, hh]

            def col_max(hh=hh, src=src):
                acc = None
                for r in chunks:
                    x = src[r:r + ROW_CHUNK, :]
                    if is_last:
                        x = jnp.where(diag >= r, x, NEG_INF)
                        src[r:r + ROW_CHUNK, :] = x
                    acc = x if acc is None else jnp.maximum(acc, x)
                m_new = jnp.maximum(m_old[hh], jnp.max(acc, axis=0, keepdims=True))
                res[hh] = (m_new, jnp.exp2(m_old[hh] - m_new))

            def exps(rows, hh=hh, src=src):
                for r in rows:
                    pt_sc[t % 2, hh, r:r + ROW_CHUNK, :] = jnp.exp2(
                        src[r:r + ROW_CHUNK, :] - res[hh][0]).astype(BF16)

            half = len(chunks) // 2
            pieces += [col_max, functools.partial(exps, chunks[:half]),
                       functools.partial(exps, chunks[half:])]
        return pieces

    def values(item, t, alpha, acc_old, new):
        _, ks, width, _ = item

        def one(hh):
            vt = jnp.concatenate(
                [vt_ref[hh * HEAD_DIM:(hh + 1) * HEAD_DIM, ks:ks + width],
                 jnp.ones((SUM_ROWS, width), BF16)], axis=0)
            new[hh] = alpha[hh] * acc_old[hh] + jnp.dot(
                vt, pt_sc[t % 2, hh, 0:width, :],
                preferred_element_type=F32)
        return [functools.partial(one, hh) for hh in range(2)]

    n_items = len(items)
    m_fresh = [jnp.full((1, tq), NEG_INF, F32)] * 2
    acc_fresh = [jnp.zeros((HEAD_DIM + SUM_ROWS, tq), F32)] * 2
    alpha = {}
    qa_tile, qa = -1, None
    m_run, acc_run = m_fresh, acc_fresh
    for t in range(-2, n_items):
        matmuls, vector = [], []
        if t + 2 < n_items:
            nxt = items[t + 2]
            if nxt[0] != qa_tile:
                qa_tile, qa = nxt[0], query_blocks(nxt[0])
            matmuls += scores(nxt, qa, (t + 2) % 3)
        acc_new = [None, None]
        if t >= 0:
            matmuls += values(items[t], t, alpha.pop(t), acc_run, acc_new)
        res = [None, None]
        if 0 <= t + 1 < n_items:
            vector = probabilities(items[t + 1], t + 1, m_run, res)
        per = -(-len(vector) // max(len(matmuls), 1))
        for n, mm in enumerate(matmuls):
            mm()
            for piece in vector[n * per:(n + 1) * per]:
                piece()
        for piece in vector[len(matmuls) * per:]:
            piece()
        if t >= 0:
            acc_run = acc_new
            if items[t][3]:
                i = items[t][0]
                ot = jnp.concatenate(
                    [acc[:HEAD_DIM] / acc[HEAD_DIM:HEAD_DIM + 1] for acc in acc_run],
                    axis=0)
                o_ref[i * tq:(i + 1) * tq, :] = ot.T.astype(BF16)
                acc_run = acc_fresh
        if 0 <= t + 1 < n_items:
            m_run = m_fresh if items[t + 1][3] else [r[0] for r in res]
            alpha[t + 1] = [r[1] for r in res]


def _fox_attn(qt, k, vt, caug, weights, bsz, seq, tq, tk):
    tok = pl.BlockSpec((None, seq, LANES), lambda b, p: (p, b, 0))
    feat = pl.BlockSpec((LANES, seq), lambda b, p: (p, b))
    steps = bsz * N_PAIRS
    flat = [w.reshape(-1, w.shape[-1]) for w in weights]
    slabs = [pl.BlockSpec((w.shape[0] // steps, w.shape[1]),
                          lambda b, p: (b * N_PAIRS + p, 0)) for w in flat]
    outs = pl.pallas_call(
        functools.partial(_fox_attn_kernel, tq=tq, tk=tk, seq=seq,
                          n_cast=len(flat)),
        grid=(bsz, N_PAIRS),
        in_specs=[feat, tok, feat,
                  pl.BlockSpec((seq, LANES), lambda b, p: (b, 0))] + slabs,
        out_specs=[tok] + slabs,
        out_shape=[jax.ShapeDtypeStruct(k.shape, BF16)]
        + [jax.ShapeDtypeStruct(w.shape, BF16) for w in flat],
        scratch_shapes=[pltpu.VMEM((3, 2, tk, tq), F32),
                        pltpu.VMEM((2, 2, tk, tq), BF16)],
        compiler_params=pltpu.CompilerParams(
            dimension_semantics=("parallel", "parallel")),
        name="fox_attn",
    )(qt, k, vt, caug, *flat)
    return outs[0], [wb.reshape(w.shape) for wb, w in zip(outs[1:], weights)]


def _post_mlp_kernel(*refs, emit_next, o_by_pair, ff_chunk):
    if emit_next:
        (h_ref, o_ref, wo_ref, gm_ref, wup_ref, wdn_ref,
         ga_ref, gkv_ref, wq_ref, wkv_ref, gq_ref, gk_ref,
         out_ref, qb_ref, kd_ref, vt_ref) = refs
    else:
        h_ref, o_ref, wo_ref, gm_ref, wup_ref, wdn_ref, out_ref = refs
    if o_by_pair:
        o = jnp.concatenate([o_ref[p] for p in range(N_PAIRS)], axis=1)
    else:
        o = o_ref[...]
    h1 = h_ref[...] + jnp.dot(o, wo_ref[...], preferred_element_type=F32)
    m = ((h1 * _rms_scale(h1)) * gm_ref[...]).astype(BF16)
    acc = h1
    for c in range(D_FF // ff_chunk):
        u = jnp.dot(m, wup_ref[:, c * ff_chunk:(c + 1) * ff_chunk],
                    preferred_element_type=F32)
        u = jnp.square(jnp.maximum(u, 0.0)).astype(BF16)
        acc = acc + jnp.dot(u, wdn_ref[c * ff_chunk:(c + 1) * ff_chunk, :],
                            preferred_element_type=F32)
    out_ref[...] = acc
    if emit_next:
        hn = acc * _rms_scale(acc)
        a = (hn * ga_ref[...]).astype(BF16)
        for c in range(D_MODEL // MXU_N):
            y = jnp.dot(a, wq_ref[:, c * MXU_N:(c + 1) * MXU_N],
                        preferred_element_type=F32)
            for half in range(MXU_N // LANES):
                yy = y[:, half * LANES:(half + 1) * LANES]
                dst = pl.ds(c * MXU_N + half * LANES, LANES)
                qb_ref[:, dst] = (_head_pair_norm(yy, gq_ref[...])
                                  * (QK_SCALE * LOG2_E)).astype(BF16)
        kvn = (hn * gkv_ref[...]).astype(BF16)
        kv = jnp.dot(kvn, wkv_ref[...], preferred_element_type=F32)
        for j in range(KV_HEADS):
            sl = slice(j * LANES, (j + 1) * LANES)
            kd_ref[:, sl] = _head_pair_norm(kv[:, sl], gk_ref[...]).astype(BF16)
        vt_ref[...] = kv[:, KV_HEADS * LANES:].T.astype(BF16)


def _post_mlp(h, o, wo, gm, wup, wdn, layer, nxt, tm, ff_chunk=1024):
    t = h.shape[0]
    tok = lambda w: pl.BlockSpec((tm, w), lambda i: (i, 0))
    o_by_pair = o.ndim == 3
    o_spec = (pl.BlockSpec((N_PAIRS, tm, LANES), lambda i: (0, i, 0))
              if o_by_pair else tok(D_MODEL))
    in_specs = [tok(D_MODEL), o_spec, _layer((D_MODEL, D_MODEL), 0),
                _resident((1, D_MODEL)), _layer((D_MODEL, D_FF), layer),
                _layer((D_FF, D_MODEL), layer)]
    out_specs = [tok(D_MODEL)]
    out_shape = [jax.ShapeDtypeStruct((t, D_MODEL), F32)]
    args = [h, o, wo, gm, wup, wdn]
    if nxt is not None:
        kvw = KV_HEADS * LANES + KV_HEADS * HEAD_DIM
        in_specs += [_resident((1, D_MODEL)), _resident((1, D_MODEL)),
                     _layer((D_MODEL, D_MODEL), 0), _resident((D_MODEL, kvw)),
                     _resident((1, LANES)), _resident((1, LANES))]
        out_specs += [tok(D_MODEL), tok(KV_HEADS * LANES),
                      pl.BlockSpec((KV_HEADS * HEAD_DIM, tm), lambda i: (0, i))]
        out_shape += [jax.ShapeDtypeStruct((t, D_MODEL), BF16),
                      jax.ShapeDtypeStruct((t, KV_HEADS * LANES), BF16),
                      jax.ShapeDtypeStruct((KV_HEADS * HEAD_DIM, t), BF16)]
        args += list(nxt)
    return pl.pallas_call(
        functools.partial(_post_mlp_kernel, emit_next=nxt is not None,
                          o_by_pair=o_by_pair, ff_chunk=ff_chunk),
        grid=(t // tm,),
        in_specs=in_specs, out_specs=out_specs, out_shape=out_shape,
        compiler_params=pltpu.CompilerParams(
            dimension_semantics=("parallel",), vmem_limit_bytes=VMEM_LIMIT),
        name="post_mlp_next" if nxt is not None else "post_mlp",
    )(*args)


def _t5_causal_bucket(dist):
    n = np.maximum(dist, 0)
    max_exact = N_BUCKETS // 2
    large = max_exact + (np.log(np.maximum(n, 1) / max_exact)
                         / np.log(REL_MAX_DIST / max_exact)
                         * (N_BUCKETS - max_exact)).astype(np.int32)
    large = np.minimum(large, N_BUCKETS - 1)
    return np.where(n < max_exact, n, large).astype(np.int32)


def _bucket_ranges():
    buckets = _t5_causal_bucket(np.arange(WINDOW))
    assert np.all(np.diff(buckets) >= 0)
    out = []
    for k in np.unique(buckets):
        idx = np.nonzero(buckets == k)[0]
        out.append((int(k), int(idx[0]), int(idx[-1]) + 1))
    return out


def _swa_attn_kernel(sink_ref, rb_ref, q_ref, kp_ref, kc_ref, vtp_ref, vtc_ref,
                     o_ref, bias_sc, st_sc, e_sc, *, n_blocks):
    n = pl.program_id(1)
    band = 2 * WINDOW

    @pl.when((pl.program_id(0) == 0) & (n == 0))
    def _():
        krow = lax.broadcasted_iota(jnp.int32, (band, WINDOW), 0)
        qcol = lax.broadcasted_iota(jnp.int32, (band, WINDOW), 1)
        dist = qcol + WINDOW - krow
        for h in range(N_HEADS):
            tile = jnp.full((band, WINDOW), NEG_INF, F32)
            for k, lo, hi in _bucket_ranges():
                tile = jnp.where((dist >= lo) & (dist < hi),
                                 rb_ref[k, h] * LOG2_E, tile)
            cols = slice((h % 2) * WINDOW, (h % 2 + 1) * WINDOW)
            bias_sc[0, h // 2, :, cols] = tile
            bias_sc[1, h // 2, :, cols] = jnp.where(krow < WINDOW, NEG_INF, tile)

    first = (n == 0).astype(jnp.int32)
    keys = jnp.concatenate([kp_ref[...], kc_ref[...]], axis=0)
    vts = jnp.concatenate([vtp_ref[...], vtc_ref[...]], axis=1)
    lane = lax.broadcasted_iota(jnp.int32, (1, LANES), 1)
    col = lax.broadcasted_iota(jnp.int32, (1, band), 1)
    pairs_per_kv = N_PAIRS // KV_HEADS
    items = [(j, p) for j in range(n_blocks) for p in range(N_PAIRS)]

    def scores(item, slot):
        j, p = item
        g = p // pairs_per_kv
        q2 = q_ref[j * WINDOW:(j + 1) * WINDOW, p * LANES:(p + 1) * LANES]
        zero = jnp.zeros_like(q2)
        qs = jnp.concatenate([jnp.where(lane < HEAD_DIM, q2, zero),
                              jnp.where(lane >= HEAD_DIM, q2, zero)], axis=0)
        st_sc[slot] = lax.dot_general(
            keys[j * WINDOW:j * WINDOW + band, g * LANES:(g + 1) * LANES], qs,
            (((1,), (1,)), ((), ())), preferred_element_type=F32)

    def softmax(item, st_slot, slot):
        j, p = item
        st = st_sc[st_slot] + bias_sc[first if j == 0 else 0, p]
        sink = jnp.where(col < WINDOW, sink_ref[2 * p], sink_ref[2 * p + 1]) * LOG2_E
        m = jnp.maximum(_reduce_rows(st, jnp.maximum), sink)
        e_sc[slot] = jnp.exp2(st - m).astype(BF16)
        return jnp.exp2(sink - m)

    def values(item, slot, sink_term):
        j, p = item
        g = p // pairs_per_kv
        vt1 = jnp.concatenate(
            [vts[g * HEAD_DIM:(g + 1) * HEAD_DIM, j * WINDOW:j * WINDOW + band],
             jnp.ones((SUM_ROWS, band), BF16)], axis=0)
        acc = jnp.dot(vt1, e_sc[slot], preferred_element_type=F32)
        ot = acc[:HEAD_DIM] / (acc[HEAD_DIM:HEAD_DIM + 1] + sink_term)
        o2 = jnp.concatenate([ot[:, :WINDOW], ot[:, WINDOW:]], axis=0).T
        o_ref[j * WINDOW:(j + 1) * WINDOW, p * LANES:(p + 1) * LANES] = o2.astype(BF16)

    for t in range(3):
        scores(items[t], t)
    sink_term = softmax(items[0], 0, 0)
    for t, item in enumerate(items):
        if t + 3 < len(items):
            scores(items[t + 3], t % 3)
        values(item, t % 2, sink_term)
        if t + 1 < len(items):
            sink_term = softmax(items[t + 1], (t + 1) % 3, (t + 1) % 2)


def _swa_attn(sinks, rel_bias, q, kd, vt, bsz, seq, n_blocks):
    nblk = seq // WINDOW
    steps = nblk // n_blocks
    kvw = KV_HEADS * LANES
    vtw = KV_HEADS * HEAD_DIM
    tq = n_blocks * WINDOW
    prev_block = lambda b, n: b * nblk + jnp.maximum(n * n_blocks - 1, 0)
    smem = pl.BlockSpec(memory_space=pltpu.SMEM)
    return pl.pallas_call(
        functools.partial(_swa_attn_kernel, n_blocks=n_blocks),
        grid=(bsz, steps),
        in_specs=[
            smem, smem,
            pl.BlockSpec((tq, D_MODEL), lambda b, n: (b * steps + n, 0)),
            pl.BlockSpec((WINDOW, kvw), lambda b, n: (prev_block(b, n), 0)),
            pl.BlockSpec((tq, kvw), lambda b, n: (b * steps + n, 0)),
            pl.BlockSpec((vtw, WINDOW), lambda b, n: (0, prev_block(b, n))),
            pl.BlockSpec((vtw, tq), lambda b, n: (0, b * steps + n)),
        ],
        out_specs=pl.BlockSpec((tq, D_MODEL), lambda b, n: (b * steps + n, 0)),
        out_shape=jax.ShapeDtypeStruct((bsz * seq, D_MODEL), BF16),
        scratch_shapes=[pltpu.VMEM((2, N_PAIRS, 2 * WINDOW, 2 * WINDOW), F32),
                        pltpu.VMEM((3, 2 * WINDOW, 2 * WINDOW), F32),
                        pltpu.VMEM((2, 2 * WINDOW, 2 * WINDOW), BF16)],
        compiler_params=pltpu.CompilerParams(
            dimension_semantics=("arbitrary", "arbitrary")),
        name="swa_attn",
    )(sinks, rel_bias, q, kd, kd, vt, vt)


def _dup_heads(w):
    d = w.shape[0]
    w = w.reshape(d, KV_HEADS, 1, HEAD_DIM)
    return jnp.broadcast_to(w, (d, KV_HEADS, 2, HEAD_DIM)).reshape(d, KV_HEADS * LANES)


def _lane_groups(v):
    return jnp.pad(jnp.tile(v, N_SPLIT), (0, LANES - N_SPLIT * N_HEADS)).reshape(1, LANES)


def kernel(x, g_attn, g_mlp, w_in_a, b_f, gq_a, gk_a, w_out_a, g_kv, w_kv, gk_b,
           w_q_b, gq_b, sinks, rel_bias, w_out_b, w_up, w_down):
    bsz, seq, d = x.shape
    t = bsz * seq
    hw = N_HEADS * HEAD_DIM
    kvw = KV_HEADS * HEAD_DIM
    row = lambda g: g.reshape(1, -1).astype(F32)
    pair = lambda g: jnp.tile(g.astype(F32), 2).reshape(1, LANES)

    x2 = x.reshape(t, d)
    wf = jnp.pad(jnp.tile(w_in_a[0, :, 3 * hw:], (1, N_SPLIT)),
                 ((0, 0), (0, LANES - N_SPLIT * N_HEADS))).astype(BF16)
    qt, k, vt, f3 = _fox_pre(x2, row(g_attn[0]), w_in_a.astype(F32), wf,
                            pair(gq_a[0]), pair(gk_a[0]), tm=1024)
    caug = _fox_decay(f3, _lane_groups(b_f[0].astype(F32)), bsz, seq)
    o, (w_up_b, w_down_b, w_out_a_b, w_out_b_b, w_q_b_b) = _fox_attn(
        qt, k, vt, caug, [w_up, w_down, w_out_a, w_out_b, w_q_b], bsz, seq,
        tq=256, tk=512)

    wkv = jnp.concatenate([_dup_heads(w_kv[:, :kvw]), w_kv[:, kvw:]],
                          axis=1).astype(BF16)
    nxt = (row(g_attn[1]), row(g_kv), w_q_b_b, wkv, pair(gq_b[0]), pair(gk_b))
    h, qb, kd, vt_b = _post_mlp(x2, o, w_out_a_b, row(g_mlp[0]),
                                w_up_b, w_down_b, 0, nxt, tm=512)

    o2 = _swa_attn(sinks[0].astype(F32), rel_bias.astype(F32), qb, kd, vt_b,
                   bsz, seq, n_blocks=8)
    (out,) = _post_mlp(h, o2, w_out_b_b, row(g_mlp[1]),
                       w_up_b, w_down_b, 1, None, tm=512)
    return out.reshape(bsz, seq, d)
```

```python
import functools

import numpy as np
import jax
import jax.numpy as jnp
from jax import lax
from jax.experimental import pallas as pl
from jax.experimental.pallas import tpu as pltpu

D_MODEL = 1024
HEAD_DIM = 64
N_HEADS = 16
N_PAIRS = N_HEADS // 2
KV_HEADS = 2
WINDOW = 128
D_FF = 4 * D_MODEL
N_BUCKETS = 32
REL_MAX_DIST = 128
NORM_EPS = 1e-6
LANES = 128
MXU_N = 256
QK_SCALE = HEAD_DIM ** -0.5
LOG2_E = 1.4426950408889634
N_SPLIT = 3
SUM_ROWS = 16
ROW_CHUNK = 32

F32 = jnp.float32
BF16 = jnp.bfloat16
NEG_INF = float("-inf")
VMEM_LIMIT = 60 * 1024 * 1024


def _rms_scale(x):
    return lax.rsqrt(jnp.mean(x * x, axis=-1, keepdims=True) + NORM_EPS)


def _head_pair_norm(y, g2):
    lo = lax.broadcasted_iota(jnp.int32, (1, LANES), 1) < HEAD_DIM
    sq = y * y
    s_lo = jnp.sum(jnp.where(lo, sq, 0.0), axis=-1, keepdims=True)
    s_hi = jnp.sum(jnp.where(lo, 0.0, sq), axis=-1, keepdims=True)
    ms = jnp.where(lo, s_lo, s_hi) * (1.0 / HEAD_DIM)
    return (y * lax.rsqrt(ms + NORM_EPS)) * g2


def _split_bf16(x):
    terms = []
    for _ in range(N_SPLIT):
        t = x.astype(BF16)
        terms.append(t)
        x = x - t.astype(F32)
    return terms


def _reduce_rows(x, op, group=32):
    acc = x[0:group]
    for r in range(group, x.shape[0], group):
        acc = op(acc, x[r:r + group])
    red = jnp.max if op is jnp.maximum else jnp.sum
    return red(acc, axis=0, keepdims=True)


def _resident(shape):
    zeros = (0,) * len(shape)
    return pl.BlockSpec(shape, lambda *_: zeros, pipeline_mode=pl.Buffered(1))


def _layer(shape, layer):
    index = (layer,) + (0,) * len(shape)
    return pl.BlockSpec((None,) + tuple(shape), lambda *_: index,
                        pipeline_mode=pl.Buffered(1))


def _fox_pre_kernel(x_ref, g_ref, wqkv_ref, wf_ref, gq_ref, gk_ref,
                    qt_ref, k_ref, vt_ref, f_ref):
    x = x_ref[...]
    hn = ((x * _rms_scale(x)) * g_ref[...]).astype(BF16)
    n_chunks = D_MODEL // MXU_N
    for c in range(3 * n_chunks):
        y = lax.dot_general(
            hn, wqkv_ref[c * MXU_N:(c + 1) * MXU_N, :].astype(BF16),
            (((1,), (1,)), ((), ())), preferred_element_type=F32)
        part, cc = divmod(c, n_chunks)
        rows = slice(cc * MXU_N, (cc + 1) * MXU_N)
        if part == 2:
            vt_ref[rows, :] = y.T.astype(BF16)
            continue
        g2 = gq_ref[...] if part == 0 else gk_ref[...]
        yn = jnp.concatenate(
            [_head_pair_norm(y[:, h * LANES:(h + 1) * LANES], g2)
             for h in range(MXU_N // LANES)], axis=1)
        if part == 0:
            qt_ref[rows, :] = (yn * (QK_SCALE * LOG2_E)).T.astype(BF16)
        else:
            for h in range(MXU_N // LANES):
                k_ref[cc * (MXU_N // LANES) + h] = yn[:, h * LANES:(h + 1) * LANES].astype(BF16)
    f_ref[...] = jnp.dot(hn, wf_ref[...], preferred_element_type=F32)


def _fox_pre(x2, g, w_in, wf, gq2, gk2, tm):
    t = x2.shape[0]
    tok = lambda w: pl.BlockSpec((tm, w), lambda i: (i, 0))
    return pl.pallas_call(
        _fox_pre_kernel,
        grid=(t // tm,),
        in_specs=[tok(D_MODEL), _resident((1, D_MODEL)),
                  _resident(w_in.shape), _resident((D_MODEL, LANES)),
                  _resident((1, LANES)), _resident((1, LANES))],
        out_specs=[pl.BlockSpec((D_MODEL, tm), lambda i: (0, i)),
                   pl.BlockSpec((N_PAIRS, tm, LANES), lambda i: (0, i, 0)),
                   pl.BlockSpec((D_MODEL, tm), lambda i: (0, i)), tok(LANES)],
        out_shape=[jax.ShapeDtypeStruct((D_MODEL, t), BF16),
                   jax.ShapeDtypeStruct((N_PAIRS, t, LANES), BF16),
                   jax.ShapeDtypeStruct((D_MODEL, t), BF16),
                   jax.ShapeDtypeStruct((t, LANES), F32)],
        compiler_params=pltpu.CompilerParams(
            dimension_semantics=("parallel",), vmem_limit_bytes=VMEM_LIMIT),
        name="fox_pre",
    )(x2, g, w_in, wf, gq2, gk2)


def _fox_decay_kernel(f_ref, b_ref, c_ref, *, blk):
    seq = f_ref.shape[0]
    lane = lax.broadcasted_iota(jnp.int32, (1, LANES), 1)
    r = lax.broadcasted_iota(jnp.int32, (blk, blk), 0)
    c = lax.broadcasted_iota(jnp.int32, (blk, blk), 1)
    lower = jnp.where(r >= c, 1.0, 0.0).astype(BF16)
    carry = jnp.zeros((1, LANES), F32)
    for b in range(seq // blk):
        rows = slice(b * blk, (b + 1) * blk)
        x = f_ref[rows, :] + b_ref[...]
        log_f = -(jnp.maximum(-x, 0.0) + jnp.log1p(jnp.exp(-jnp.abs(x))))
        cb = carry
        for term in _split_bf16(log_f):
            cb = cb + jnp.dot(lower, term, preferred_element_type=F32)
        carry = cb[blk - 1:blk, :]
        out = jnp.zeros((blk, LANES), BF16)
        for t, term in reversed(list(enumerate(_split_bf16(cb * LOG2_E)))):
            out = jnp.where(lane < (t + 1) * N_HEADS, term, out)
        c_ref[rows, :] = out


def _fox_decay(f3, b3, bsz, seq):
    spec = pl.BlockSpec((seq, LANES), lambda b: (b, 0))
    return pl.pallas_call(
        functools.partial(_fox_decay_kernel, blk=MXU_N),
        grid=(bsz,),
        in_specs=[spec, _resident((1, LANES))],
        out_specs=spec,
        out_shape=jax.ShapeDtypeStruct(f3.shape, BF16),
        compiler_params=pltpu.CompilerParams(dimension_semantics=("parallel",)),
        name="fox_decay",
    )(f3, b3)


def _fox_attn_kernel(qt_ref, k_ref, vt_ref, c_ref, *rest, tq, tk, seq, n_cast,
                     s_ahead, e_ahead):
    w_refs, o_ref = rest[:n_cast], rest[n_cast]
    wb_refs = rest[n_cast + 1:2 * n_cast + 1]
    st_sc, pt_sc = rest[2 * n_cast + 1:]
    for w_ref, wb_ref in zip(w_refs, wb_refs):
        wb_ref[...] = w_ref[...].astype(BF16)
    p = pl.program_id(1)
    feat = lax.broadcasted_iota(jnp.int32, (LANES, 1), 0)
    own = [feat < HEAD_DIM, feat >= HEAD_DIM]
    neg = [jnp.broadcast_to(
        jnp.where((feat < N_SPLIT * N_HEADS)
                  & ((feat & (N_HEADS - 1)) == 2 * p + hh), -1.0, 0.0).astype(BF16),
        (LANES, tq)) for hh in range(2)]

    items = []
    for i in range(seq // tq):
        n_keys = (i + 1) * tq
        starts = list(range(0, n_keys, tk))
        for ks in starts:
            items.append((i, ks, min(tk, n_keys - ks), ks == starts[-1]))

    def query_blocks(i):
        q2 = qt_ref[:, i * tq:(i + 1) * tq]
        return [jnp.concatenate(
            [jnp.where(own[hh], q2, jnp.zeros_like(q2)), neg[hh]], axis=0)
            for hh in range(2)]

    def scores(item, qa, slot, col_max):
        i, ks, width, is_last = item
        ka = jnp.concatenate([k_ref[ks:ks + width, :], c_ref[ks:ks + width, :]],
                             axis=1)
        if is_last:
            krow = lax.broadcasted_iota(jnp.int32, (width, tq), 0)
            qcol = lax.broadcasted_iota(jnp.int32, (width, tq), 1)
            visible = krow <= qcol + (i * tq - ks)

        def one(hh):
            st = jnp.dot(ka, qa[hh], preferred_element_type=F32)
            if is_last:
                st = jnp.where(visible, st, NEG_INF)
            st_sc[slot, hh, 0:width, :] = st
            col_max[hh] = _reduce_rows(st, jnp.maximum)
        return [functools.partial(one, hh) for hh in range(2)]

    def probabilities(item, t, col_max, m_old, res):
        width = item[2]
        chunks = list(range(0, width, ROW_CHUNK))
        pieces = []
        for hh in range(2):
            src = st_sc.at[t % n_st, hh]

            def new_max(hh=hh):
                m_new = jnp.maximum(m_old[hh], col_max[hh])
                res[hh] = (m_new, jnp.exp2(m_old[hh] - m_new))

            def exps(rows, hh=hh, src=src):
                for r in rows:
                    pt_sc[t % n_pt, hh, r:r + ROW_CHUNK, :] = jnp.exp2(
                        src[r:r + ROW_CHUNK, :] - res[hh][0]).astype(BF16)

            half = len(chunks) // 2
            pieces += [new_max, functools.partial(exps, chunks[:half]),
                       functools.partial(exps, chunks[half:])]
        return pieces

    def values(item, t, alpha, acc_old, new):
        _, ks, width, _ = item

        def one(hh):
            vt = jnp.concatenate(
                [vt_ref[hh * HEAD_DIM:(hh + 1) * HEAD_DIM, ks:ks + width],
                 jnp.ones((SUM_ROWS, width), BF16)], axis=0)
            new[hh] = alpha[hh] * acc_old[hh] + jnp.dot(
                vt, pt_sc[t % n_pt, hh, 0:width, :],
                preferred_element_type=F32)
        return [functools.partial(one, hh) for hh in range(2)]

    n_items = len(items)
    n_st, n_pt = st_sc.shape[0], pt_sc.shape[0]
    assert n_st > s_ahead - e_ahead and n_pt > e_ahead
    m_fresh = [jnp.full((1, tq), NEG_INF, F32)] * 2
    acc_fresh = [jnp.zeros((HEAD_DIM + SUM_ROWS, tq), F32)] * 2
    alpha, col_max = {}, {}
    qa_tile, qa = -1, None
    m_run, acc_run = m_fresh, acc_fresh
    for t in range(-s_ahead, n_items):
        matmuls, vector = [], []
        u, e = t + s_ahead, t + e_ahead
        if u < n_items:
            if items[u][0] != qa_tile:
                qa_tile, qa = items[u][0], query_blocks(items[u][0])
            col_max[u] = [None, None]
            matmuls += scores(items[u], qa, u % n_st, col_max[u])
        acc_new = [None, None]
        if t >= 0:
            matmuls += values(items[t], t, alpha.pop(t), acc_run, acc_new)
        res = [None, None]
        if 0 <= e < n_items:
            vector = probabilities(items[e], e, col_max.pop(e), m_run, res)
        per = -(-len(vector) // max(len(matmuls), 1))
        for n, mm in enumerate(matmuls):
            mm()
            for piece in vector[n * per:(n + 1) * per]:
                piece()
        for piece in vector[len(matmuls) * per:]:
            piece()
        if t >= 0:
            acc_run = acc_new
            if items[t][3]:
                i = items[t][0]
                ot = jnp.concatenate(
                    [acc[:HEAD_DIM] / acc[HEAD_DIM:HEAD_DIM + 1] for acc in acc_run],
                    axis=0)
                o_ref[i * tq:(i + 1) * tq, :] = ot.T.astype(BF16)
                acc_run = acc_fresh
        if 0 <= e < n_items:
            m_run = m_fresh if items[e][3] else [r[0] for r in res]
            alpha[e] = [r[1] for r in res]


def _fox_attn(qt, k, vt, caug, weights, bsz, seq, tq, tk, s_ahead=4, e_ahead=2):
    tok = pl.BlockSpec((None, seq, LANES), lambda b, p: (p, b, 0))
    feat = pl.BlockSpec((LANES, seq), lambda b, p: (p, b))
    steps = bsz * N_PAIRS
    flat = [w.reshape(-1, w.shape[-1]) for w in weights]
    slabs = [pl.BlockSpec((w.shape[0] // steps, w.shape[1]),
                          lambda b, p: (b * N_PAIRS + p, 0)) for w in flat]
    outs = pl.pallas_call(
        functools.partial(_fox_attn_kernel, tq=tq, tk=tk, seq=seq,
                          n_cast=len(flat), s_ahead=s_ahead, e_ahead=e_ahead),
        grid=(bsz, N_PAIRS),
        in_specs=[feat, tok, feat,
                  pl.BlockSpec((seq, LANES), lambda b, p: (b, 0))] + slabs,
        out_specs=[tok] + slabs,
        out_shape=[jax.ShapeDtypeStruct(k.shape, BF16)]
        + [jax.ShapeDtypeStruct(w.shape, BF16) for w in flat],
        scratch_shapes=[pltpu.VMEM((s_ahead - e_ahead + 2, 2, tk, tq), F32),
                        pltpu.VMEM((e_ahead + 1, 2, tk, tq), BF16)],
        compiler_params=pltpu.CompilerParams(
            dimension_semantics=("parallel", "parallel")),
        name="fox_attn",
    )(qt, k, vt, caug, *flat)
    return outs[0], [wb.reshape(w.shape) for wb, w in zip(outs[1:], weights)]


def _post_mlp_kernel(*refs, emit_next, o_by_pair, ff_chunk):
    if emit_next:
        (h_ref, o_ref, wo_ref, gm_ref, wup_ref, wdn_ref,
         ga_ref, gkv_ref, wq_ref, wkv_ref, gq_ref, gk_ref,
         out_ref, qb_ref, kd_ref, vt_ref) = refs
    else:
        h_ref, o_ref, wo_ref, gm_ref, wup_ref, wdn_ref, out_ref = refs
    if o_by_pair:
        o = jnp.concatenate([o_ref[p] for p in range(N_PAIRS)], axis=1)
    else:
        o = o_ref[...]
    h1 = h_ref[...] + jnp.dot(o, wo_ref[...], preferred_element_type=F32)
    m = ((h1 * _rms_scale(h1)) * gm_ref[...]).astype(BF16)
    acc = h1
    for c in range(D_FF // ff_chunk):
        u = jnp.dot(m, wup_ref[:, c * ff_chunk:(c + 1) * ff_chunk],
                    preferred_element_type=F32)
        u = jnp.square(jnp.maximum(u, 0.0)).astype(BF16)
        acc = acc + jnp.dot(u, wdn_ref[c * ff_chunk:(c + 1) * ff_chunk, :],
                            preferred_element_type=F32)
    out_ref[...] = acc
    if emit_next:
        hn = acc * _rms_scale(acc)
        a = (hn * ga_ref[...]).astype(BF16)
        for c in range(D_MODEL // MXU_N):
            y = jnp.dot(a, wq_ref[:, c * MXU_N:(c + 1) * MXU_N],
                        preferred_element_type=F32)
            for half in range(MXU_N // LANES):
                yy = y[:, half * LANES:(half + 1) * LANES]
                dst = pl.ds(c * MXU_N + half * LANES, LANES)
                qb_ref[:, dst] = (_head_pair_norm(yy, gq_ref[...])
                                  * (QK_SCALE * LOG2_E)).astype(BF16)
        kvn = (hn * gkv_ref[...]).astype(BF16)
        kv = jnp.dot(kvn, wkv_ref[...], preferred_element_type=F32)
        for j in range(KV_HEADS):
            sl = slice(j * LANES, (j + 1) * LANES)
            kd_ref[:, sl] = _head_pair_norm(kv[:, sl], gk_ref[...]).astype(BF16)
        vt_ref[...] = kv[:, KV_HEADS * LANES:].T.astype(BF16)


def _post_mlp(h, o, wo, gm, wup, wdn, layer, nxt, tm, ff_chunk=1024):
    t = h.shape[0]
    tok = lambda w: pl.BlockSpec((tm, w), lambda i: (i, 0))
    o_by_pair = o.ndim == 3
    o_spec = (pl.BlockSpec((N_PAIRS, tm, LANES), lambda i: (0, i, 0))
              if o_by_pair else tok(D_MODEL))
    in_specs = [tok(D_MODEL), o_spec, _layer((D_MODEL, D_MODEL), 0),
                _resident((1, D_MODEL)), _layer((D_MODEL, D_FF), layer),
                _layer((D_FF, D_MODEL), layer)]
    out_specs = [tok(D_MODEL)]
    out_shape = [jax.ShapeDtypeStruct((t, D_MODEL), F32)]
    args = [h, o, wo, gm, wup, wdn]
    if nxt is not None:
        kvw = KV_HEADS * LANES + KV_HEADS * HEAD_DIM
        in_specs += [_resident((1, D_MODEL)), _resident((1, D_MODEL)),
                     _layer((D_MODEL, D_MODEL), 0), _resident((D_MODEL, kvw)),
                     _resident((1, LANES)), _resident((1, LANES))]
        out_specs += [tok(D_MODEL), tok(KV_HEADS * LANES),
                      pl.BlockSpec((KV_HEADS * HEAD_DIM, tm), lambda i: (0, i))]
        out_shape += [jax.ShapeDtypeStruct((t, D_MODEL), BF16),
                      jax.ShapeDtypeStruct((t, KV_HEADS * LANES), BF16),
                      jax.ShapeDtypeStruct((KV_HEADS * HEAD_DIM, t), BF16)]
        args += list(nxt)
    return pl.pallas_call(
        functools.partial(_post_mlp_kernel, emit_next=nxt is not None,
                          o_by_pair=o_by_pair, ff_chunk=ff_chunk),
        grid=(t // tm,),
        in_specs=in_specs, out_specs=out_specs, out_shape=out_shape,
        compiler_params=pltpu.CompilerParams(
            dimension_semantics=("parallel",), vmem_limit_bytes=VMEM_LIMIT),
        name="post_mlp_next" if nxt is not None else "post_mlp",
    )(*args)


def _t5_causal_bucket(dist):
    n = np.maximum(dist, 0)
    max_exact = N_BUCKETS // 2
    large = max_exact + (np.log(np.maximum(n, 1) / max_exact)
                         / np.log(REL_MAX_DIST / max_exact)
                         * (N_BUCKETS - max_exact)).astype(np.int32)
    large = np.minimum(large, N_BUCKETS - 1)
    return np.where(n < max_exact, n, large).astype(np.int32)


def _bucket_ranges():
    buckets = _t5_causal_bucket(np.arange(WINDOW))
    assert np.all(np.diff(buckets) >= 0)
    out = []
    for k in np.unique(buckets):
        idx = np.nonzero(buckets == k)[0]
        out.append((int(k), int(idx[0]), int(idx[-1]) + 1))
    return out


def _swa_attn_kernel(sink_ref, rb_ref, q_ref, kp_ref, kc_ref, vtp_ref, vtc_ref,
                     o_ref, bias_sc, st_sc, e_sc, *, n_blocks):
    n = pl.program_id(1)
    band = 2 * WINDOW

    @pl.when((pl.program_id(0) == 0) & (n == 0))
    def _():
        krow = lax.broadcasted_iota(jnp.int32, (band, WINDOW), 0)
        qcol = lax.broadcasted_iota(jnp.int32, (band, WINDOW), 1)
        dist = qcol + WINDOW - krow
        for h in range(N_HEADS):
            tile = jnp.full((band, WINDOW), NEG_INF, F32)
            for k, lo, hi in _bucket_ranges():
                tile = jnp.where((dist >= lo) & (dist < hi),
                                 rb_ref[k, h] * LOG2_E, tile)
            cols = slice((h % 2) * WINDOW, (h % 2 + 1) * WINDOW)
            bias_sc[0, h // 2, :, cols] = tile
            bias_sc[1, h // 2, :, cols] = jnp.where(krow < WINDOW, NEG_INF, tile)

    first = (n == 0).astype(jnp.int32)
    keys = jnp.concatenate([kp_ref[...], kc_ref[...]], axis=0)
    vts = jnp.concatenate([vtp_ref[...], vtc_ref[...]], axis=1)
    lane = lax.broadcasted_iota(jnp.int32, (1, LANES), 1)
    col = lax.broadcasted_iota(jnp.int32, (1, band), 1)
    pairs_per_kv = N_PAIRS // KV_HEADS
    items = [(j, p) for j in range(n_blocks) for p in range(N_PAIRS)]

    def scores(item, slot):
        j, p = item
        g = p // pairs_per_kv
        q2 = q_ref[j * WINDOW:(j + 1) * WINDOW, p * LANES:(p + 1) * LANES]
        zero = jnp.zeros_like(q2)
        qs = jnp.concatenate([jnp.where(lane < HEAD_DIM, q2, zero),
                              jnp.where(lane >= HEAD_DIM, q2, zero)], axis=0)
        st_sc[slot] = lax.dot_general(
            keys[j * WINDOW:j * WINDOW + band, g * LANES:(g + 1) * LANES], qs,
            (((1,), (1,)), ((), ())), preferred_element_type=F32)

    def softmax(item, st_slot, slot):
        j, p = item
        st = st_sc[st_slot] + bias_sc[first if j == 0 else 0, p]
        sink = jnp.where(col < WINDOW, sink_ref[2 * p], sink_ref[2 * p + 1]) * LOG2_E
        m = jnp.maximum(_reduce_rows(st, jnp.maximum), sink)
        e_sc[slot] = jnp.exp2(st - m).astype(BF16)
        return jnp.exp2(sink - m)

    def values(item, slot, sink_term):
        j, p = item
        g = p // pairs_per_kv
        vt1 = jnp.concatenate(
            [vts[g * HEAD_DIM:(g + 1) * HEAD_DIM, j * WINDOW:j * WINDOW + band],
             jnp.ones((SUM_ROWS, band), BF16)], axis=0)
        acc = jnp.dot(vt1, e_sc[slot], preferred_element_type=F32)
        ot = acc[:HEAD_DIM] / (acc[HEAD_DIM:HEAD_DIM + 1] + sink_term)
        o2 = jnp.concatenate([ot[:, :WINDOW], ot[:, WINDOW:]], axis=0).T
        o_ref[j * WINDOW:(j + 1) * WINDOW, p * LANES:(p + 1) * LANES] = o2.astype(BF16)

    for t in range(3):
        scores(items[t], t)
    sink_term = softmax(items[0], 0, 0)
    for t, item in enumerate(items):
        if t + 3 < len(items):
            scores(items[t + 3], t % 3)
        values(item, t % 2, sink_term)
        if t + 1 < len(items):
            sink_term = softmax(items[t + 1], (t + 1) % 3, (t + 1) % 2)


def _swa_attn(sinks, rel_bias, q, kd, vt, bsz, seq, n_blocks):
    nblk = seq // WINDOW
    steps = nblk // n_blocks
    kvw = KV_HEADS * LANES
    vtw = KV_HEADS * HEAD_DIM
    tq = n_blocks * WINDOW
    prev_block = lambda b, n: b * nblk + jnp.maximum(n * n_blocks - 1, 0)
    smem = pl.BlockSpec(memory_space=pltpu.SMEM)
    return pl.pallas_call(
        functools.partial(_swa_attn_kernel, n_blocks=n_blocks),
        grid=(bsz, steps),
        in_specs=[
            smem, smem,
            pl.BlockSpec((tq, D_MODEL), lambda b, n: (b * steps + n, 0)),
            pl.BlockSpec((WINDOW, kvw), lambda b, n: (prev_block(b, n), 0)),
            pl.BlockSpec((tq, kvw), lambda b, n: (b * steps + n, 0)),
            pl.BlockSpec((vtw, WINDOW), lambda b, n: (0, prev_block(b, n))),
            pl.BlockSpec((vtw, tq), lambda b, n: (0, b * steps + n)),
        ],
        out_specs=pl.BlockSpec((tq, D_MODEL), lambda b, n: (b * steps + n, 0)),
        out_shape=jax.ShapeDtypeStruct((bsz * seq, D_MODEL), BF16),
        scratch_shapes=[pltpu.VMEM((2, N_PAIRS, 2 * WINDOW, 2 * WINDOW), F32),
                        pltpu.VMEM((3, 2 * WINDOW, 2 * WINDOW), F32),
                        pltpu.VMEM((2, 2 * WINDOW, 2 * WINDOW), BF16)],
        compiler_params=pltpu.CompilerParams(
            dimension_semantics=("arbitrary", "arbitrary")),
        name="swa_attn",
    )(sinks, rel_bias, q, kd, kd, vt, vt)


def _dup_heads(w):
    d = w.shape[0]
    w = w.reshape(d, KV_HEADS, 1, HEAD_DIM)
    return jnp.broadcast_to(w, (d, KV_HEADS, 2, HEAD_DIM)).reshape(d, KV_HEADS * LANES)


def _lane_groups(v):
    return jnp.pad(jnp.tile(v, N_SPLIT), (0, LANES - N_SPLIT * N_HEADS)).reshape(1, LANES)


def kernel(x, g_attn, g_mlp, w_in_a, b_f, gq_a, gk_a, w_out_a, g_kv, w_kv, gk_b,
           w_q_b, gq_b, sinks, rel_bias, w_out_b, w_up, w_down):
    bsz, seq, d = x.shape
    t = bsz * seq
    hw = N_HEADS * HEAD_DIM
    kvw = KV_HEADS * HEAD_DIM
    row = lambda g: g.reshape(1, -1).astype(F32)
    pair = lambda g: jnp.tile(g.astype(F32), 2).reshape(1, LANES)

    x2 = x.reshape(t, d)
    wf = jnp.pad(jnp.tile(w_in_a[0, :, 3 * hw:], (1, N_SPLIT)),
                 ((0, 0), (0, LANES - N_SPLIT * N_HEADS))).astype(BF16)
    qt, k, vt, f3 = _fox_pre(x2, row(g_attn[0]), w_in_a[0].T.astype(F32), wf,
                            pair(gq_a[0]), pair(gk_a[0]), tm=1024)
    caug = _fox_decay(f3, _lane_groups(b_f[0].astype(F32)), bsz, seq)
    o, (w_up_b, w_down_b, w_out_a_b, w_out_b_b, w_q_b_b) = _fox_attn(
        qt, k, vt, caug, [w_up, w_down, w_out_a, w_out_b, w_q_b], bsz, seq,
        tq=256, tk=512)

    wkv = jnp.concatenate([_dup_heads(w_kv[:, :kvw]), w_kv[:, kvw:]],
                          axis=1).astype(BF16)
    nxt = (row(g_attn[1]), row(g_kv), w_q_b_b, wkv, pair(gq_b[0]), pair(gk_b))
    h, qb, kd, vt_b = _post_mlp(x2, o, w_out_a_b, row(g_mlp[0]),
                                w_up_b, w_down_b, 0, nxt, tm=1024)

    o2 = _swa_attn(sinks[0].astype(F32), rel_bias.astype(F32), qb, kd, vt_b,
                   bsz, seq, n_blocks=8)
    (out,) = _post_mlp(h, o2, w_out_b_b, row(g_mlp[1]),
                       w_up_b, w_down_b, 1, None, tm=1024)
    return out.reshape(bsz, seq, d)
```

```python
import functools

import numpy as np
import jax
import jax.numpy as jnp
from jax import lax
from jax.experimental import pallas as pl
from jax.experimental.pallas import tpu as pltpu

D_MODEL = 1024
HEAD_DIM = 64
N_HEADS = 16
N_PAIRS = N_HEADS // 2
KV_HEADS = 2
WINDOW = 128
D_FF = 4 * D_MODEL
N_BUCKETS = 32
REL_MAX_DIST = 128
NORM_EPS = 1e-6
LANES = 128
MXU_N = 256
QK_SCALE = HEAD_DIM ** -0.5
LOG2_E = 1.4426950408889634
N_SPLIT = 3
SUM_ROWS = 16
ROW_CHUNK = 32

F32 = jnp.float32
BF16 = jnp.bfloat16
NEG_INF = float("-inf")

PRE_TOKENS = 1024
MLP_TOKENS = 512
MLP_FF_CHUNK = 1024
FOX_TQ, FOX_TK = 256, 512
FOX_SCORES_AHEAD = 2
FOX_PROBS_AHEAD = 1
SWA_BLOCKS = 8
VMEM_LIMIT = 56 * 1024 * 1024


def _rms_scale(x):
    return lax.rsqrt(jnp.mean(x * x, axis=-1, keepdims=True) + NORM_EPS)


def _head_pair_norm(y, g):
    g2 = jnp.concatenate([g, g], axis=1)
    lo = lax.broadcasted_iota(jnp.int32, (1, LANES), 1) < HEAD_DIM
    sq = y * y
    s_lo = jnp.sum(jnp.where(lo, sq, 0.0), axis=-1, keepdims=True)
    s_hi = jnp.sum(jnp.where(lo, 0.0, sq), axis=-1, keepdims=True)
    ms = jnp.where(lo, s_lo, s_hi) * (1.0 / HEAD_DIM)
    return (y * lax.rsqrt(ms + NORM_EPS)) * g2


def _split_bf16(x):
    terms = []
    for _ in range(N_SPLIT):
        t = x.astype(BF16)
        terms.append(t)
        x = x - t.astype(F32)
    return terms


def _reduce_rows(x, op, group=32):
    acc = x[0:group]
    for r in range(group, x.shape[0], group):
        acc = op(acc, x[r:r + group])
    red = jnp.max if op is jnp.maximum else jnp.sum
    return red(acc, axis=0, keepdims=True)


def _resident(shape):
    zeros = (0,) * len(shape)
    return pl.BlockSpec(shape, lambda *_: zeros, pipeline_mode=pl.Buffered(1))


def _layer(shape, layer):
    index = (layer,) + (0,) * len(shape)
    return pl.BlockSpec((None,) + tuple(shape), lambda *_: index,
                        pipeline_mode=pl.Buffered(1))


def _fox_pre_kernel(x_ref, g_ref, w_ref, gq_ref, gk_ref,
                    qt_ref, k_ref, vt_ref, f_ref):
    x = x_ref[...]
    hn = ((x * _rms_scale(x)) * g_ref[...]).astype(BF16)
    n_chunks = D_MODEL // MXU_N
    for c in range(3 * n_chunks):
        y = lax.dot_general(
            hn, w_ref[c * MXU_N:(c + 1) * MXU_N, :].astype(BF16),
            (((1,), (1,)), ((), ())), preferred_element_type=F32)
        part, cc = divmod(c, n_chunks)
        rows = slice(cc * MXU_N, (cc + 1) * MXU_N)
        if part == 2:
            vt_ref[rows, :] = y.T.astype(BF16)
            continue
        g2 = gq_ref[...] if part == 0 else gk_ref[...]
        yn = jnp.concatenate(
            [_head_pair_norm(y[:, h * LANES:(h + 1) * LANES], g2)
             for h in range(MXU_N // LANES)], axis=1)
        if part == 0:
            qt_ref[rows, :] = (yn * (QK_SCALE * LOG2_E)).T.astype(BF16)
        else:
            for h in range(MXU_N // LANES):
                k_ref[cc * (MXU_N // LANES) + h] = yn[:, h * LANES:(h + 1) * LANES].astype(BF16)
    w_f = w_ref[3 * D_MODEL:3 * D_MODEL + N_HEADS, :]
    w_f3 = jnp.concatenate(
        [w_f] * N_SPLIT + [jnp.zeros((LANES - N_SPLIT * N_HEADS, D_MODEL), F32)],
        axis=0).astype(BF16)
    f_ref[...] = lax.dot_general(hn, w_f3, (((1,), (1,)), ((), ())),
                                 preferred_element_type=F32)


def _fox_pre(x2, g, w_in, gq2, gk2, tm):
    t = x2.shape[0]
    tok = lambda w: pl.BlockSpec((tm, w), lambda i: (i, 0))
    return pl.pallas_call(
        _fox_pre_kernel,
        grid=(t // tm,),
        in_specs=[tok(D_MODEL), _resident((1, D_MODEL)),
                  _resident(w_in.shape),
                  _resident((1, HEAD_DIM)), _resident((1, HEAD_DIM))],
        out_specs=[pl.BlockSpec((D_MODEL, tm), lambda i: (0, i)),
                   pl.BlockSpec((N_PAIRS, tm, LANES), lambda i: (0, i, 0)),
                   pl.BlockSpec((D_MODEL, tm), lambda i: (0, i)), tok(LANES)],
        out_shape=[jax.ShapeDtypeStruct((D_MODEL, t), BF16),
                   jax.ShapeDtypeStruct((N_PAIRS, t, LANES), BF16),
                   jax.ShapeDtypeStruct((D_MODEL, t), BF16),
                   jax.ShapeDtypeStruct((t, LANES), F32)],
        compiler_params=pltpu.CompilerParams(
            dimension_semantics=("parallel",), vmem_limit_bytes=VMEM_LIMIT),
        name="fox_pre",
    )(x2, g, w_in, gq2, gk2)


def _fox_decay_kernel(f_ref, b_ref, c_ref, *, blk):
    seq = f_ref.shape[0]
    lane = lax.broadcasted_iota(jnp.int32, (1, LANES), 1)
    r = lax.broadcasted_iota(jnp.int32, (blk, blk), 0)
    c = lax.broadcasted_iota(jnp.int32, (blk, blk), 1)
    lower = jnp.where(r >= c, 1.0, 0.0).astype(BF16)
    bias = jnp.concatenate(
        [b_ref[...]] * N_SPLIT + [jnp.zeros((1, LANES - N_SPLIT * N_HEADS), F32)], axis=1)
    carry = jnp.zeros((1, LANES), F32)
    for b in range(seq // blk):
        rows = slice(b * blk, (b + 1) * blk)
        x = f_ref[rows, :] + bias
        log_f = -(jnp.maximum(-x, 0.0) + jnp.log1p(jnp.exp(-jnp.abs(x))))
        cb = carry
        for term in _split_bf16(log_f):
            cb = cb + jnp.dot(lower, term, preferred_element_type=F32)
        carry = cb[blk - 1:blk, :]
        out = jnp.zeros((blk, LANES), BF16)
        for t, term in reversed(list(enumerate(_split_bf16(cb * LOG2_E)))):
            out = jnp.where(lane < (t + 1) * N_HEADS, term, out)
        c_ref[rows, :] = out


def _fox_decay(f3, b3, bsz, seq):
    spec = pl.BlockSpec((seq, LANES), lambda b: (b, 0))
    return pl.pallas_call(
        functools.partial(_fox_decay_kernel, blk=MXU_N),
        grid=(bsz,),
        in_specs=[spec, _resident((1, N_HEADS))],
        out_specs=spec,
        out_shape=jax.ShapeDtypeStruct(f3.shape, BF16),
        compiler_params=pltpu.CompilerParams(dimension_semantics=("parallel",)),
        name="fox_decay",
    )(f3, b3)


def _fox_attn_kernel(qt_ref, k_ref, vt_ref, c_ref, *rest, tq, tk, seq, n_cast,
                     s_ahead, e_ahead):
    w_refs, o_ref = rest[:n_cast], rest[n_cast]
    wb_refs = rest[n_cast + 1:2 * n_cast + 1]
    st_sc, pt_sc = rest[2 * n_cast + 1:]
    for w_ref, wb_ref in zip(w_refs, wb_refs):
        wb_ref[...] = w_ref[...].astype(BF16)
    p = pl.program_id(1)
    feat = lax.broadcasted_iota(jnp.int32, (LANES, 1), 0)
    own = [feat < HEAD_DIM, feat >= HEAD_DIM]
    neg = [jnp.broadcast_to(
        jnp.where((feat < N_SPLIT * N_HEADS)
                  & ((feat & (N_HEADS - 1)) == 2 * p + hh), -1.0, 0.0).astype(BF16),
        (LANES, tq)) for hh in range(2)]

    items = []
    for i in range(seq // tq):
        n_keys = (i + 1) * tq
        starts = list(range(0, n_keys, tk))
        for ks in starts:
            items.append((i, ks, min(tk, n_keys - ks), ks == starts[-1]))

    def query_blocks(i):
        q2 = qt_ref[:, i * tq:(i + 1) * tq]
        return [jnp.concatenate(
            [jnp.where(own[hh], q2, jnp.zeros_like(q2)), neg[hh]], axis=0)
            for hh in range(2)]

    def scores(item, qa, slot, col_max):
        i, ks, width, is_last = item
        ka = jnp.concatenate([k_ref[ks:ks + width, :], c_ref[ks:ks + width, :]],
                             axis=1)
        if is_last:
            krow = lax.broadcasted_iota(jnp.int32, (width, tq), 0)
            qcol = lax.broadcasted_iota(jnp.int32, (width, tq), 1)
            visible = krow <= qcol + (i * tq - ks)

        def one(hh):
            st = jnp.dot(ka, qa[hh], preferred_element_type=F32)
            if is_last:
                st = jnp.where(visible, st, NEG_INF)
            st_sc[slot, hh, 0:width, :] = st
            col_max[hh] = _reduce_rows(st, jnp.maximum)
        return [functools.partial(one, hh) for hh in range(2)]

    def probabilities(item, t, col_max, m_old, res):
        width = item[2]
        chunks = list(range(0, width, ROW_CHUNK))
        pieces = []
        for hh in range(2):
            src = st_sc.at[t % n_st, hh]

            def new_max(hh=hh):
                m_new = jnp.maximum(m_old[hh], col_max[hh])
                res[hh] = (m_new, jnp.exp2(m_old[hh] - m_new))

            def exps(rows, hh=hh, src=src):
                for r in rows:
                    pt_sc[t % n_pt, hh, r:r + ROW_CHUNK, :] = jnp.exp2(
                        src[r:r + ROW_CHUNK, :] - res[hh][0]).astype(BF16)

            half = len(chunks) // 2
            pieces += [new_max, functools.partial(exps, chunks[:half]),
                       functools.partial(exps, chunks[half:])]
        return pieces

    def values(item, t, alpha, acc_old, new):
        _, ks, width, _ = item

        def one(hh):
            vt = jnp.concatenate(
                [vt_ref[hh * HEAD_DIM:(hh + 1) * HEAD_DIM, ks:ks + width],
                 jnp.ones((SUM_ROWS, width), BF16)], axis=0)
            new[hh] = alpha[hh] * acc_old[hh] + jnp.dot(
                vt, pt_sc[t % n_pt, hh, 0:width, :],
                preferred_element_type=F32)
        return [functools.partial(one, hh) for hh in range(2)]

    n_items = len(items)
    n_st, n_pt = st_sc.shape[0], pt_sc.shape[0]
    assert n_st > s_ahead - e_ahead and n_pt > e_ahead
    m_fresh = [jnp.full((1, tq), NEG_INF, F32)] * 2
    acc_fresh = [jnp.zeros((HEAD_DIM + SUM_ROWS, tq), F32)] * 2
    alpha, col_max = {}, {}
    qa_tile, qa = -1, None
    m_run, acc_run = m_fresh, acc_fresh
    for t in range(-s_ahead, n_items):
        matmuls, vector = [], []
        u, e = t + s_ahead, t + e_ahead
        if u < n_items:
            if items[u][0] != qa_tile:
                qa_tile, qa = items[u][0], query_blocks(items[u][0])
            col_max[u] = [None, None]
            matmuls += scores(items[u], qa, u % n_st, col_max[u])
        acc_new = [None, None]
        if t >= 0:
            matmuls += values(items[t], t, alpha.pop(t), acc_run, acc_new)
        res = [None, None]
        if 0 <= e < n_items:
            vector = probabilities(items[e], e, col_max.pop(e), m_run, res)
        per = -(-len(vector) // max(len(matmuls), 1))
        for n, mm in enumerate(matmuls):
            mm()
            for piece in vector[n * per:(n + 1) * per]:
                piece()
        for piece in vector[len(matmuls) * per:]:
            piece()
        if t >= 0:
            acc_run = acc_new
            if items[t][3]:
                i = items[t][0]
                ot = jnp.concatenate(
                    [acc[:HEAD_DIM] / acc[HEAD_DIM:HEAD_DIM + 1] for acc in acc_run],
                    axis=0)
                o_ref[i * tq:(i + 1) * tq, :] = ot.T.astype(BF16)
                acc_run = acc_fresh
        if 0 <= e < n_items:
            m_run = m_fresh if items[e][3] else [r[0] for r in res]
            alpha[e] = [r[1] for r in res]


def _fox_attn(qt, k, vt, caug, weights, bsz, seq, tq, tk,
              s_ahead=FOX_SCORES_AHEAD, e_ahead=FOX_PROBS_AHEAD):
    tok = pl.BlockSpec((None, seq, LANES), lambda b, p: (p, b, 0))
    feat = pl.BlockSpec((LANES, seq), lambda b, p: (p, b))
    steps = bsz * N_PAIRS
    flat = [w.reshape(-1, w.shape[-1]) for w in weights]
    slabs = [pl.BlockSpec((w.shape[0] // steps, w.shape[1]),
                          lambda b, p: (b * N_PAIRS + p, 0)) for w in flat]
    outs = pl.pallas_call(
        functools.partial(_fox_attn_kernel, tq=tq, tk=tk, seq=seq,
                          n_cast=len(flat), s_ahead=s_ahead, e_ahead=e_ahead),
        grid=(bsz, N_PAIRS),
        in_specs=[feat, tok, feat,
                  pl.BlockSpec((seq, LANES), lambda b, p: (b, 0))] + slabs,
        out_specs=[tok] + slabs,
        out_shape=[jax.ShapeDtypeStruct(k.shape, BF16)]
        + [jax.ShapeDtypeStruct(w.shape, BF16) for w in flat],
        scratch_shapes=[pltpu.VMEM((s_ahead - e_ahead + 2, 2, tk, tq), F32),
                        pltpu.VMEM((e_ahead + 1, 2, tk, tq), BF16)],
        compiler_params=pltpu.CompilerParams(
            dimension_semantics=("parallel", "parallel")),
        name="fox_attn",
    )(qt, k, vt, caug, *flat)
    return outs[0], [wb.reshape(w.shape) for wb, w in zip(outs[1:], weights)]


def _post_mlp_kernel(*refs, emit_next, o_by_pair, ff_chunk):
    if emit_next:
        (h_ref, o_ref, wo_ref, gm_ref, wup_ref, wdn_ref,
         ga_ref, gkv_ref, wq_ref, wkv_ref, gq_ref, gk_ref,
         out_ref, qb_ref, kd_ref, vt_ref) = refs
    else:
        h_ref, o_ref, wo_ref, gm_ref, wup_ref, wdn_ref, out_ref = refs
    if o_by_pair:
        o = jnp.concatenate([o_ref[p] for p in range(N_PAIRS)], axis=1)
    else:
        o = o_ref[...]
    h1 = h_ref[...] + jnp.dot(o, wo_ref[...], preferred_element_type=F32)
    m = ((h1 * _rms_scale(h1)) * gm_ref[...]).astype(BF16)
    acc = h1
    for c in range(D_FF // ff_chunk):
        u = jnp.dot(m, wup_ref[:, c * ff_chunk:(c + 1) * ff_chunk],
                    preferred_element_type=F32)
        u = jnp.square(jnp.maximum(u, 0.0)).astype(BF16)
        acc = acc + jnp.dot(u, wdn_ref[c * ff_chunk:(c + 1) * ff_chunk, :],
                            preferred_element_type=F32)
    out_ref[...] = acc
    if emit_next:
        hn = acc * _rms_scale(acc)
        a = (hn * ga_ref[...]).astype(BF16)
        for c in range(D_MODEL // MXU_N):
            y = jnp.dot(a, wq_ref[:, c * MXU_N:(c + 1) * MXU_N],
                        preferred_element_type=F32)
            for half in range(MXU_N // LANES):
                yy = y[:, half * LANES:(half + 1) * LANES]
                dst = pl.ds(c * MXU_N + half * LANES, LANES)
                qb_ref[:, dst] = (_head_pair_norm(yy, gq_ref[...])
                                  * (QK_SCALE * LOG2_E)).astype(BF16)
        kvn = (hn * gkv_ref[...]).astype(BF16)
        kv = jnp.dot(kvn, wkv_ref[...], preferred_element_type=F32)
        for j in range(KV_HEADS):
            sl = slice(j * LANES, (j + 1) * LANES)
            kd_ref[:, sl] = _head_pair_norm(kv[:, sl], gk_ref[...]).astype(BF16)
        vt_ref[...] = kv[:, KV_HEADS * LANES:].T.astype(BF16)


def _post_mlp(h, o, wo, gm, wup, wdn, layer, nxt, tm, ff_chunk=MLP_FF_CHUNK):
    t = h.shape[0]
    tok = lambda w: pl.BlockSpec((tm, w), lambda i: (i, 0))
    o_by_pair = o.ndim == 3
    o_spec = (pl.BlockSpec((N_PAIRS, tm, LANES), lambda i: (0, i, 0))
              if o_by_pair else tok(D_MODEL))
    in_specs = [tok(D_MODEL), o_spec, _layer((D_MODEL, D_MODEL), 0),
                _resident((1, D_MODEL)), _layer((D_MODEL, D_FF), layer),
                _layer((D_FF, D_MODEL), layer)]
    out_specs = [tok(D_MODEL)]
    out_shape = [jax.ShapeDtypeStruct((t, D_MODEL), F32)]
    args = [h, o, wo, gm, wup, wdn]
    if nxt is not None:
        kvw = KV_HEADS * LANES + KV_HEADS * HEAD_DIM
        in_specs += [_resident((1, D_MODEL)), _resident((1, D_MODEL)),
                     _layer((D_MODEL, D_MODEL), 0), _resident((D_MODEL, kvw)),
                     _resident((1, HEAD_DIM)), _resident((1, HEAD_DIM))]
        out_specs += [tok(D_MODEL), tok(KV_HEADS * LANES),
                      pl.BlockSpec((KV_HEADS * HEAD_DIM, tm), lambda i: (0, i))]
        out_shape += [jax.ShapeDtypeStruct((t, D_MODEL), BF16),
                      jax.ShapeDtypeStruct((t, KV_HEADS * LANES), BF16),
                      jax.ShapeDtypeStruct((KV_HEADS * HEAD_DIM, t), BF16)]
        args += list(nxt)
    return pl.pallas_call(
        functools.partial(_post_mlp_kernel, emit_next=nxt is not None,
                          o_by_pair=o_by_pair, ff_chunk=ff_chunk),
        grid=(t // tm,),
        in_specs=in_specs, out_specs=out_specs, out_shape=out_shape,
        compiler_params=pltpu.CompilerParams(
            dimension_semantics=("parallel",), vmem_limit_bytes=VMEM_LIMIT),
        name="post_mlp_next" if nxt is not None else "post_mlp",
    )(*args)


def _t5_causal_bucket(dist):
    n = np.maximum(dist, 0)
    max_exact = N_BUCKETS // 2
    large = max_exact + (np.log(np.maximum(n, 1) / max_exact)
                         / np.log(REL_MAX_DIST / max_exact)
                         * (N_BUCKETS - max_exact)).astype(np.int32)
    large = np.minimum(large, N_BUCKETS - 1)
    return np.where(n < max_exact, n, large).astype(np.int32)


def _bucket_ranges():
    buckets = _t5_causal_bucket(np.arange(WINDOW))
    assert np.all(np.diff(buckets) >= 0)
    out = []
    for k in np.unique(buckets):
        idx = np.nonzero(buckets == k)[0]
        out.append((int(k), int(idx[0]), int(idx[-1]) + 1))
    return out


def _swa_attn_kernel(sink_ref, rb_ref, q_ref, kp_ref, kc_ref, vtp_ref, vtc_ref,
                     o_ref, bias_sc, st_sc, e_sc, *, n_blocks):
    n = pl.program_id(1)
    band = 2 * WINDOW

    @pl.when((pl.program_id(0) == 0) & (n == 0))
    def _():
        krow = lax.broadcasted_iota(jnp.int32, (band, WINDOW), 0)
        qcol = lax.broadcasted_iota(jnp.int32, (band, WINDOW), 1)
        dist = qcol + WINDOW - krow
        for h in range(N_HEADS):
            tile = jnp.full((band, WINDOW), NEG_INF, F32)
            for k, lo, hi in _bucket_ranges():
                tile = jnp.where((dist >= lo) & (dist < hi),
                                 rb_ref[k, h] * LOG2_E, tile)
            cols = slice((h % 2) * WINDOW, (h % 2 + 1) * WINDOW)
            bias_sc[0, h // 2, :, cols] = tile
            bias_sc[1, h // 2, :, cols] = jnp.where(krow < WINDOW, NEG_INF, tile)

    first = (n == 0).astype(jnp.int32)
    keys = jnp.concatenate([kp_ref[...], kc_ref[...]], axis=0)
    vts = jnp.concatenate([vtp_ref[...], vtc_ref[...]], axis=1)
    lane = lax.broadcasted_iota(jnp.int32, (1, LANES), 1)
    col = lax.broadcasted_iota(jnp.int32, (1, band), 1)
    pairs_per_kv = N_PAIRS // KV_HEADS
    items = [(j, p) for j in range(n_blocks) for p in range(N_PAIRS)]

    def scores(item, slot):
        j, p = item
        g = p // pairs_per_kv
        q2 = q_ref[j * WINDOW:(j + 1) * WINDOW, p * LANES:(p + 1) * LANES]
        zero = jnp.zeros_like(q2)
        qs = jnp.concatenate([jnp.where(lane < HEAD_DIM, q2, zero),
                              jnp.where(lane >= HEAD_DIM, q2, zero)], axis=0)
        st_sc[slot] = lax.dot_general(
            keys[j * WINDOW:j * WINDOW + band, g * LANES:(g + 1) * LANES], qs,
            (((1,), (1,)), ((), ())), preferred_element_type=F32)

    def softmax(item, st_slot, slot):
        j, p = item
        st = st_sc[st_slot] + bias_sc[first if j == 0 else 0, p]
        sink = jnp.where(col < WINDOW, sink_ref[2 * p], sink_ref[2 * p + 1]) * LOG2_E
        m = jnp.maximum(_reduce_rows(st, jnp.maximum), sink)
        e_sc[slot] = jnp.exp2(st - m).astype(BF16)
        return jnp.exp2(sink - m)

    def values(item, slot, sink_term):
        j, p = item
        g = p // pairs_per_kv
        vt1 = jnp.concatenate(
            [vts[g * HEAD_DIM:(g + 1) * HEAD_DIM, j * WINDOW:j * WINDOW + band],
             jnp.ones((SUM_ROWS, band), BF16)], axis=0)
        acc = jnp.dot(vt1, e_sc[slot], preferred_element_type=F32)
        ot = acc[:HEAD_DIM] / (acc[HEAD_DIM:HEAD_DIM + 1] + sink_term)
        o2 = jnp.concatenate([ot[:, :WINDOW], ot[:, WINDOW:]], axis=0).T
        o_ref[j * WINDOW:(j + 1) * WINDOW, p * LANES:(p + 1) * LANES] = o2.astype(BF16)

    for t in range(3):
        scores(items[t], t)
    sink_term = softmax(items[0], 0, 0)
    for t, item in enumerate(items):
        if t + 3 < len(items):
            scores(items[t + 3], t % 3)
        values(item, t % 2, sink_term)
        if t + 1 < len(items):
            sink_term = softmax(items[t + 1], (t + 1) % 3, (t + 1) % 2)


def _swa_attn(sinks, rel_bias, q, kd, vt, bsz, seq, n_blocks):
    nblk = seq // WINDOW
    steps = nblk // n_blocks
    kvw = KV_HEADS * LANES
    vtw = KV_HEADS * HEAD_DIM
    tq = n_blocks * WINDOW
    prev_block = lambda b, n: b * nblk + jnp.maximum(n * n_blocks - 1, 0)
    smem = pl.BlockSpec(memory_space=pltpu.SMEM)
    return pl.pallas_call(
        functools.partial(_swa_attn_kernel, n_blocks=n_blocks),
        grid=(bsz, steps),
        in_specs=[
            smem, smem,
            pl.BlockSpec((tq, D_MODEL), lambda b, n: (b * steps + n, 0)),
            pl.BlockSpec((WINDOW, kvw), lambda b, n: (prev_block(b, n), 0)),
            pl.BlockSpec((tq, kvw), lambda b, n: (b * steps + n, 0)),
            pl.BlockSpec((vtw, WINDOW), lambda b, n: (0, prev_block(b, n))),
            pl.BlockSpec((vtw, tq), lambda b, n: (0, b * steps + n)),
        ],
        out_specs=pl.BlockSpec((tq, D_MODEL), lambda b, n: (b * steps + n, 0)),
        out_shape=jax.ShapeDtypeStruct((bsz * seq, D_MODEL), BF16),
        scratch_shapes=[pltpu.VMEM((2, N_PAIRS, 2 * WINDOW, 2 * WINDOW), F32),
                        pltpu.VMEM((3, 2 * WINDOW, 2 * WINDOW), F32),
                        pltpu.VMEM((2, 2 * WINDOW, 2 * WINDOW), BF16)],
        compiler_params=pltpu.CompilerParams(
            dimension_semantics=("arbitrary", "arbitrary")),
        name="swa_attn",
    )(sinks, rel_bias, q, kd, kd, vt, vt)


def _dup_heads(w):
    d = w.shape[0]
    w = w.reshape(d, KV_HEADS, 1, HEAD_DIM)
    return jnp.broadcast_to(w, (d, KV_HEADS, 2, HEAD_DIM)).reshape(d, KV_HEADS * LANES)


def kernel(x, g_attn, g_mlp, w_in_a, b_f, gq_a, gk_a, w_out_a, g_kv, w_kv, gk_b,
           w_q_b, gq_b, sinks, rel_bias, w_out_b, w_up, w_down):
    bsz, seq, d = x.shape
    t = bsz * seq
    kvw = KV_HEADS * HEAD_DIM
    row = lambda g: g.reshape(1, -1).astype(F32)

    x2 = x.reshape(t, d)
    qt, k, vt, f3 = _fox_pre(x2, row(g_attn[0]), w_in_a[0].T.astype(F32),
                            row(gq_a[0]), row(gk_a[0]), tm=PRE_TOKENS)
    caug = _fox_decay(f3, row(b_f[0]), bsz, seq)
    o, (w_up_b, w_down_b, w_out_a_b, w_out_b_b, w_q_b_b) = _fox_attn(
        qt, k, vt, caug, [w_up, w_down, w_out_a, w_out_b, w_q_b], bsz, seq,
        tq=FOX_TQ, tk=FOX_TK)

    wkv = jnp.concatenate([_dup_heads(w_kv[:, :kvw]), w_kv[:, kvw:]],
                          axis=1).astype(BF16)
    nxt = (row(g_attn[1]), row(g_kv), w_q_b_b, wkv, row(gq_b[0]), row(gk_b))
    h, qb, kd, vt_b = _post_mlp(x2, o, w_out_a_b, row(g_mlp[0]),
                                w_up_b, w_down_b, 0, nxt, tm=MLP_TOKENS)

    o2 = _swa_attn(sinks[0].astype(F32), rel_bias.astype(F32), qb, kd, vt_b,
                   bsz, seq, n_blocks=SWA_BLOCKS)
    (out,) = _post_mlp(h, o2, w_out_b_b, row(g_mlp[1]),
                       w_up_b, w_down_b, 1, None, tm=MLP_TOKENS)
    return out.reshape(bsz, seq, d)
```

```python
import functools

import numpy as np
import jax
import jax.numpy as jnp
from jax import lax
from jax.experimental import pallas as pl
from jax.experimental.pallas import tpu as pltpu

D_MODEL = 1024
HEAD_DIM = 64
N_HEADS = 16
N_PAIRS = N_HEADS // 2
KV_HEADS = 2
WINDOW = 128
D_FF = 4 * D_MODEL
N_BUCKETS = 32
REL_MAX_DIST = 128
NORM_EPS = 1e-6
LANES = 128
MXU_N = 256
QK_SCALE = HEAD_DIM ** -0.5
LOG2_E = 1.4426950408889634
N_SPLIT = 3
SUM_ROWS = 16
ROW_CHUNK = 32

F32 = jnp.float32
BF16 = jnp.bfloat16
NEG_INF = float("-inf")

PRE_TOKENS = 1024
MLP_TOKENS = 512
MLP_FF_CHUNK = 1024
FOX_TQ, FOX_TK = 256, 512
FOX_SCORES_AHEAD = 2
FOX_PROBS_AHEAD = 1
SWA_BLOCKS = 8
VMEM_LIMIT = 56 * 1024 * 1024


def _rms_scale(x):
    return lax.rsqrt(jnp.mean(x * x, axis=-1, keepdims=True) + NORM_EPS)


def _head_pair_norm(y, g):
    g2 = jnp.concatenate([g, g], axis=1)
    lo = lax.broadcasted_iota(jnp.int32, (1, LANES), 1) < HEAD_DIM
    sq = y * y
    s_lo = jnp.sum(jnp.where(lo, sq, 0.0), axis=-1, keepdims=True)
    s_hi = jnp.sum(jnp.where(lo, 0.0, sq), axis=-1, keepdims=True)
    ms = jnp.where(lo, s_lo, s_hi) * (1.0 / HEAD_DIM)
    return (y * lax.rsqrt(ms + NORM_EPS)) * g2


def _split_bf16(x):
    terms = []
    for _ in range(N_SPLIT):
        t = x.astype(BF16)
        terms.append(t)
        x = x - t.astype(F32)
    return terms


def _reduce_rows(x, op, group=32):
    acc = x[0:group]
    for r in range(group, x.shape[0], group):
        acc = op(acc, x[r:r + group])
    red = jnp.max if op is jnp.maximum else jnp.sum
    return red(acc, axis=0, keepdims=True)


def _resident(shape):
    zeros = (0,) * len(shape)
    return pl.BlockSpec(shape, lambda *_: zeros, pipeline_mode=pl.Buffered(1))


def _layer(shape, layer):
    index = (layer,) + (0,) * len(shape)
    return pl.BlockSpec((None,) + tuple(shape), lambda *_: index,
                        pipeline_mode=pl.Buffered(1))


def _fox_pre_kernel(x_ref, g_ref, w_ref, gq_ref, gk_ref,
                    qt_ref, k_ref, vt_ref, f_ref):
    x = x_ref[...]
    hn = ((x * _rms_scale(x)) * g_ref[...]).astype(BF16)
    n_chunks = D_MODEL // MXU_N
    for c in range(3 * n_chunks):
        y = lax.dot_general(
            hn, w_ref[c * MXU_N:(c + 1) * MXU_N, :].astype(BF16),
            (((1,), (1,)), ((), ())), preferred_element_type=F32)
        part, cc = divmod(c, n_chunks)
        rows = slice(cc * MXU_N, (cc + 1) * MXU_N)
        if part == 2:
            vt_ref[rows, :] = y.T.astype(BF16)
            continue
        g2 = gq_ref[...] if part == 0 else gk_ref[...]
        yn = jnp.concatenate(
            [_head_pair_norm(y[:, h * LANES:(h + 1) * LANES], g2)
             for h in range(MXU_N // LANES)], axis=1)
        if part == 0:
            qt_ref[rows, :] = (yn * (QK_SCALE * LOG2_E)).T.astype(BF16)
        else:
            for h in range(MXU_N // LANES):
                k_ref[cc * (MXU_N // LANES) + h] = yn[:, h * LANES:(h + 1) * LANES].astype(BF16)
    w_f = w_ref[3 * D_MODEL:3 * D_MODEL + N_HEADS, :]
    w_f3 = jnp.concatenate(
        [w_f] * N_SPLIT + [jnp.zeros((LANES - N_SPLIT * N_HEADS, D_MODEL), F32)],
        axis=0).astype(BF16)
    f_ref[...] = lax.dot_general(hn, w_f3, (((1,), (1,)), ((), ())),
                                 preferred_element_type=F32)


def _fox_pre(x2, g, w_in, gq2, gk2, tm):
    t = x2.shape[0]
    tok = lambda w: pl.BlockSpec((tm, w), lambda i: (i, 0))
    return pl.pallas_call(
        _fox_pre_kernel,
        grid=(t // tm,),
        in_specs=[tok(D_MODEL), _resident((1, D_MODEL)),
                  _resident(w_in.shape),
                  _resident((1, HEAD_DIM)), _resident((1, HEAD_DIM))],
        out_specs=[pl.BlockSpec((D_MODEL, tm), lambda i: (0, i)),
                   pl.BlockSpec((N_PAIRS, tm, LANES), lambda i: (0, i, 0)),
                   pl.BlockSpec((D_MODEL, tm), lambda i: (0, i)), tok(LANES)],
        out_shape=[jax.ShapeDtypeStruct((D_MODEL, t), BF16),
                   jax.ShapeDtypeStruct((N_PAIRS, t, LANES), BF16),
                   jax.ShapeDtypeStruct((D_MODEL, t), BF16),
                   jax.ShapeDtypeStruct((t, LANES), F32)],
        compiler_params=pltpu.CompilerParams(
            dimension_semantics=("parallel",), vmem_limit_bytes=VMEM_LIMIT),
        name="fox_pre",
    )(x2, g, w_in, gq2, gk2)


def _fox_decay_kernel(f_ref, b_ref, c_ref, *, blk):
    seq = f_ref.shape[0]
    lane = lax.broadcasted_iota(jnp.int32, (1, LANES), 1)
    r = lax.broadcasted_iota(jnp.int32, (blk, blk), 0)
    c = lax.broadcasted_iota(jnp.int32, (blk, blk), 1)
    lower = jnp.where(r >= c, 1.0, 0.0).astype(BF16)
    bias = jnp.concatenate(
        [b_ref[...]] * N_SPLIT + [jnp.zeros((1, LANES - N_SPLIT * N_HEADS), F32)], axis=1)
    carry = jnp.zeros((1, LANES), F32)
    for b in range(seq // blk):
        rows = slice(b * blk, (b + 1) * blk)
        x = f_ref[rows, :] + bias
        log_f = -(jnp.maximum(-x, 0.0) + jnp.log1p(jnp.exp(-jnp.abs(x))))
        cb = carry
        for term in _split_bf16(log_f):
            cb = cb + jnp.dot(lower, term, preferred_element_type=F32)
        carry = cb[blk - 1:blk, :]
        out = jnp.zeros((blk, LANES), BF16)
        for t, term in reversed(list(enumerate(_split_bf16(cb * LOG2_E)))):
            out = jnp.where(lane < (t + 1) * N_HEADS, term, out)
        c_ref[rows, :] = out


def _fox_decay(f3, b3, bsz, seq):
    spec = pl.BlockSpec((seq, LANES), lambda b: (b, 0))
    return pl.pallas_call(
        functools.partial(_fox_decay_kernel, blk=MXU_N),
        grid=(bsz,),
        in_specs=[spec, _resident((1, N_HEADS))],
        out_specs=spec,
        out_shape=jax.ShapeDtypeStruct(f3.shape, BF16),
        compiler_params=pltpu.CompilerParams(dimension_semantics=("parallel",)),
        name="fox_decay",
    )(f3, b3)


def _fox_attn_kernel(qt_ref, k_ref, vt_ref, c_ref, *rest, tq, tk, seq, n_cast,
                     s_ahead, e_ahead):
    w_refs, wkv_ref, o_ref = rest[:n_cast], rest[n_cast], rest[n_cast + 1]
    wb_refs, wkvb_ref = rest[n_cast + 2:2 * n_cast + 2], rest[2 * n_cast + 2]
    st_sc, pt_sc = rest[2 * n_cast + 3:]
    for w_ref, wb_ref in zip(w_refs, wb_refs):
        wb_ref[...] = w_ref[...].astype(BF16)
    wkv = wkv_ref[...]
    k_heads = [wkv[:, j * HEAD_DIM:(j + 1) * HEAD_DIM] for j in range(KV_HEADS)]
    wkvb_ref[...] = jnp.concatenate(
        [kh for kh in k_heads for _ in range(2)] + [wkv[:, KV_HEADS * HEAD_DIM:]],
        axis=1).astype(BF16)
    p = pl.program_id(1)
    feat = lax.broadcasted_iota(jnp.int32, (LANES, 1), 0)
    own = [feat < HEAD_DIM, feat >= HEAD_DIM]
    neg = [jnp.broadcast_to(
        jnp.where((feat < N_SPLIT * N_HEADS)
                  & ((feat & (N_HEADS - 1)) == 2 * p + hh), -1.0, 0.0).astype(BF16),
        (LANES, tq)) for hh in range(2)]

    items = []
    for i in range(seq // tq):
        n_keys = (i + 1) * tq
        starts = list(range(0, n_keys, tk))
        for ks in starts:
            items.append((i, ks, min(tk, n_keys - ks), ks == starts[-1]))

    def query_blocks(i):
        q2 = qt_ref[:, i * tq:(i + 1) * tq]
        return [jnp.concatenate(
            [jnp.where(own[hh], q2, jnp.zeros_like(q2)), neg[hh]], axis=0)
            for hh in range(2)]

    def scores(item, qa, slot, col_max):
        i, ks, width, is_last = item
        ka = jnp.concatenate([k_ref[ks:ks + width, :], c_ref[ks:ks + width, :]],
                             axis=1)
        if is_last:
            krow = lax.broadcasted_iota(jnp.int32, (width, tq), 0)
            qcol = lax.broadcasted_iota(jnp.int32, (width, tq), 1)
            visible = krow <= qcol + (i * tq - ks)

        def one(hh):
            st = jnp.dot(ka, qa[hh], preferred_element_type=F32)
            if is_last:
                st = jnp.where(visible, st, NEG_INF)
            st_sc[slot, hh, 0:width, :] = st
            col_max[hh] = _reduce_rows(st, jnp.maximum)
        return [functools.partial(one, hh) for hh in range(2)]

    def probabilities(item, t, col_max, m_old, res):
        width = item[2]
        chunks = list(range(0, width, ROW_CHUNK))
        pieces = []
        for hh in range(2):
            src = st_sc.at[t % n_st, hh]

            def new_max(hh=hh):
                m_new = jnp.maximum(m_old[hh], col_max[hh])
                res[hh] = (m_new, jnp.exp2(m_old[hh] - m_new))

            def exps(rows, hh=hh, src=src):
                for r in rows:
                    pt_sc[t % n_pt, hh, r:r + ROW_CHUNK, :] = jnp.exp2(
                        src[r:r + ROW_CHUNK, :] - res[hh][0]).astype(BF16)

            half = len(chunks) // 2
            pieces += [new_max, functools.partial(exps, chunks[:half]),
                       functools.partial(exps, chunks[half:])]
        return pieces

    def values(item, t, alpha, acc_old, new):
        _, ks, width, _ = item

        def one(hh):
            vt = jnp.concatenate(
                [vt_ref[hh * HEAD_DIM:(hh + 1) * HEAD_DIM, ks:ks + width],
                 jnp.ones((SUM_ROWS, width), BF16)], axis=0)
            new[hh] = alpha[hh] * acc_old[hh] + jnp.dot(
                vt, pt_sc[t % n_pt, hh, 0:width, :],
                preferred_element_type=F32)
        return [functools.partial(one, hh) for hh in range(2)]

    n_items = len(items)
    n_st, n_pt = st_sc.shape[0], pt_sc.shape[0]
    assert n_st > s_ahead - e_ahead and n_pt > e_ahead
    m_fresh = [jnp.full((1, tq), NEG_INF, F32)] * 2
    acc_fresh = [jnp.zeros((HEAD_DIM + SUM_ROWS, tq), F32)] * 2
    alpha, col_max = {}, {}
    qa_tile, qa = -1, None
    m_run, acc_run = m_fresh, acc_fresh
    for t in range(-s_ahead, n_items):
        matmuls, vector = [], []
        u, e = t + s_ahead, t + e_ahead
        if u < n_items:
            if items[u][0] != qa_tile:
                qa_tile, qa = items[u][0], query_blocks(items[u][0])
            col_max[u] = [None, None]
            matmuls += scores(items[u], qa, u % n_st, col_max[u])
        acc_new = [None, None]
        if t >= 0:
            matmuls += values(items[t], t, alpha.pop(t), acc_run, acc_new)
        res = [None, None]
        if 0 <= e < n_items:
            vector = probabilities(items[e], e, col_max.pop(e), m_run, res)
        per = -(-len(vector) // max(len(matmuls), 1))
        for n, mm in enumerate(matmuls):
            mm()
            for piece in vector[n * per:(n + 1) * per]:
                piece()
        for piece in vector[len(matmuls) * per:]:
            piece()
        if t >= 0:
            acc_run = acc_new
            if items[t][3]:
                i = items[t][0]
                ot = jnp.concatenate(
                    [acc[:HEAD_DIM] / acc[HEAD_DIM:HEAD_DIM + 1] for acc in acc_run],
                    axis=0)
                o_ref[i * tq:(i + 1) * tq, :] = ot.T.astype(BF16)
                acc_run = acc_fresh
        if 0 <= e < n_items:
            m_run = m_fresh if items[e][3] else [r[0] for r in res]
            alpha[e] = [r[1] for r in res]


def _fox_attn(qt, k, vt, caug, weights, w_kv, bsz, seq, tq, tk,
              s_ahead=FOX_SCORES_AHEAD, e_ahead=FOX_PROBS_AHEAD):
    tok = pl.BlockSpec((None, seq, LANES), lambda b, p: (p, b, 0))
    feat = pl.BlockSpec((LANES, seq), lambda b, p: (p, b))
    steps = bsz * N_PAIRS
    flat = [w.reshape(-1, w.shape[-1]) for w in weights]
    slab = lambda rows, cols: pl.BlockSpec((rows // steps, cols),
                                           lambda b, p: (b * N_PAIRS + p, 0))
    slabs = [slab(*w.shape) for w in flat]
    kv_rows, kv_cols = w_kv.shape
    kv_out_cols = kv_cols + KV_HEADS * HEAD_DIM
    outs = pl.pallas_call(
        functools.partial(_fox_attn_kernel, tq=tq, tk=tk, seq=seq,
                          n_cast=len(flat), s_ahead=s_ahead, e_ahead=e_ahead),
        grid=(bsz, N_PAIRS),
        in_specs=[feat, tok, feat,
                  pl.BlockSpec((seq, LANES), lambda b, p: (b, 0))]
        + slabs + [slab(kv_rows, kv_cols)],
        out_specs=[tok] + slabs + [slab(kv_rows, kv_out_cols)],
        out_shape=[jax.ShapeDtypeStruct(k.shape, BF16)]
        + [jax.ShapeDtypeStruct(w.shape, BF16) for w in flat]
        + [jax.ShapeDtypeStruct((kv_rows, kv_out_cols), BF16)],
        scratch_shapes=[pltpu.VMEM((s_ahead - e_ahead + 2, 2, tk, tq), F32),
                        pltpu.VMEM((e_ahead + 1, 2, tk, tq), BF16)],
        compiler_params=pltpu.CompilerParams(
            dimension_semantics=("parallel", "parallel")),
        name="fox_attn",
    )(qt, k, vt, caug, *flat, w_kv)
    return (outs[0], [wb.reshape(w.shape) for wb, w in zip(outs[1:-1], weights)],
            outs[-1])


def _post_mlp_kernel(*refs, emit_next, o_by_pair, ff_chunk):
    if emit_next:
        (h_ref, o_ref, wo_ref, gm_ref, wup_ref, wdn_ref,
         ga_ref, gkv_ref, wq_ref, wkv_ref, gq_ref, gk_ref,
         out_ref, qb_ref, kd_ref, vt_ref) = refs
    else:
        h_ref, o_ref, wo_ref, gm_ref, wup_ref, wdn_ref, out_ref = refs
    if o_by_pair:
        o = jnp.concatenate([o_ref[p] for p in range(N_PAIRS)], axis=1)
    else:
        o = o_ref[...]
    h1 = h_ref[...] + jnp.dot(o, wo_ref[...], preferred_element_type=F32)
    m = ((h1 * _rms_scale(h1)) * gm_ref[...]).astype(BF16)
    acc = h1
    for c in range(D_FF // ff_chunk):
        u = jnp.dot(m, wup_ref[:, c * ff_chunk:(c + 1) * ff_chunk],
                    preferred_element_type=F32)
        u = jnp.square(jnp.maximum(u, 0.0)).astype(BF16)
        acc = acc + jnp.dot(u, wdn_ref[c * ff_chunk:(c + 1) * ff_chunk, :],
                            preferred_element_type=F32)
    out_ref[...] = acc
    if emit_next:
        hn = acc * _rms_scale(acc)
        a = (hn * ga_ref[...]).astype(BF16)
        for c in range(D_MODEL // MXU_N):
            y = jnp.dot(a, wq_ref[:, c * MXU_N:(c + 1) * MXU_N],
                        preferred_element_type=F32)
            for half in range(MXU_N // LANES):
                yy = y[:, half * LANES:(half + 1) * LANES]
                dst = pl.ds(c * MXU_N + half * LANES, LANES)
                qb_ref[:, dst] = (_head_pair_norm(yy, gq_ref[...])
                                  * (QK_SCALE * LOG2_E)).astype(BF16)
        kvn = (hn * gkv_ref[...]).astype(BF16)
        kv = jnp.dot(kvn, wkv_ref[...], preferred_element_type=F32)
        for j in range(KV_HEADS):
            sl = slice(j * LANES, (j + 1) * LANES)
            kd_ref[:, sl] = _head_pair_norm(kv[:, sl], gk_ref[...]).astype(BF16)
        vt_ref[...] = kv[:, KV_HEADS * LANES:].T.astype(BF16)


def _post_mlp(h, o, wo, gm, wup, wdn, layer, nxt, tm, ff_chunk=MLP_FF_CHUNK):
    t = h.shape[0]
    tok = lambda w: pl.BlockSpec((tm, w), lambda i: (i, 0))
    o_by_pair = o.ndim == 3
    o_spec = (pl.BlockSpec((N_PAIRS, tm, LANES), lambda i: (0, i, 0))
              if o_by_pair else tok(D_MODEL))
    in_specs = [tok(D_MODEL), o_spec, _layer((D_MODEL, D_MODEL), 0),
                _resident((1, D_MODEL)), _layer((D_MODEL, D_FF), layer),
                _layer((D_FF, D_MODEL), layer)]
    out_specs = [tok(D_MODEL)]
    out_shape = [jax.ShapeDtypeStruct((t, D_MODEL), F32)]
    args = [h, o, wo, gm, wup, wdn]
    if nxt is not None:
        kvw = KV_HEADS * LANES + KV_HEADS * HEAD_DIM
        in_specs += [_resident((1, D_MODEL)), _resident((1, D_MODEL)),
                     _layer((D_MODEL, D_MODEL), 0), _resident((D_MODEL, kvw)),
                     _resident((1, HEAD_DIM)), _resident((1, HEAD_DIM))]
        out_specs += [tok(D_MODEL), tok(KV_HEADS * LANES),
                      pl.BlockSpec((KV_HEADS * HEAD_DIM, tm), lambda i: (0, i))]
        out_shape += [jax.ShapeDtypeStruct((t, D_MODEL), BF16),
                      jax.ShapeDtypeStruct((t, KV_HEADS * LANES), BF16),
                      jax.ShapeDtypeStruct((KV_HEADS * HEAD_DIM, t), BF16)]
        args += list(nxt)
    return pl.pallas_call(
        functools.partial(_post_mlp_kernel, emit_next=nxt is not None,
                          o_by_pair=o_by_pair, ff_chunk=ff_chunk),
        grid=(t // tm,),
        in_specs=in_specs, out_specs=out_specs, out_shape=out_shape,
        compiler_params=pltpu.CompilerParams(
            dimension_semantics=("parallel",), vmem_limit_bytes=VMEM_LIMIT),
        name="post_mlp_next" if nxt is not None else "post_mlp",
    )(*args)


def _t5_causal_bucket(dist):
    n = np.maximum(dist, 0)
    max_exact = N_BUCKETS // 2
    large = max_exact + (np.log(np.maximum(n, 1) / max_exact)
                         / np.log(REL_MAX_DIST / max_exact)
                         * (N_BUCKETS - max_exact)).astype(np.int32)
    large = np.minimum(large, N_BUCKETS - 1)
    return np.where(n < max_exact, n, large).astype(np.int32)


def _bucket_ranges():
    buckets = _t5_causal_bucket(np.arange(WINDOW))
    assert np.all(np.diff(buckets) >= 0)
    out = []
    for k in np.unique(buckets):
        idx = np.nonzero(buckets == k)[0]
        out.append((int(k), int(idx[0]), int(idx[-1]) + 1))
    return out


def _swa_attn_kernel(sink_ref, rb_ref, q_ref, kp_ref, kc_ref, vtp_ref, vtc_ref,
                     o_ref, bias_sc, st_sc, e_sc, *, n_blocks):
    n = pl.program_id(1)
    band = 2 * WINDOW

    @pl.when((pl.program_id(0) == 0) & (n == 0))
    def _():
        krow = lax.broadcasted_iota(jnp.int32, (band, WINDOW), 0)
        qcol = lax.broadcasted_iota(jnp.int32, (band, WINDOW), 1)
        dist = qcol + WINDOW - krow
        for h in range(N_HEADS):
            tile = jnp.full((band, WINDOW), NEG_INF, F32)
            for k, lo, hi in _bucket_ranges():
                tile = jnp.where((dist >= lo) & (dist < hi),
                                 rb_ref[k, h] * LOG2_E, tile)
            cols = slice((h % 2) * WINDOW, (h % 2 + 1) * WINDOW)
            bias_sc[0, h // 2, :, cols] = tile
            bias_sc[1, h // 2, :, cols] = jnp.where(krow < WINDOW, NEG_INF, tile)

    first = (n == 0).astype(jnp.int32)
    keys = jnp.concatenate([kp_ref[...], kc_ref[...]], axis=0)
    vts = jnp.concatenate([vtp_ref[...], vtc_ref[...]], axis=1)
    lane = lax.broadcasted_iota(jnp.int32, (1, LANES), 1)
    col = lax.broadcasted_iota(jnp.int32, (1, band), 1)
    pairs_per_kv = N_PAIRS // KV_HEADS
    items = [(j, p) for j in range(n_blocks) for p in range(N_PAIRS)]

    def scores(item, slot):
        j, p = item
        g = p // pairs_per_kv
        q2 = q_ref[j * WINDOW:(j + 1) * WINDOW, p * LANES:(p + 1) * LANES]
        zero = jnp.zeros_like(q2)
        qs = jnp.concatenate([jnp.where(lane < HEAD_DIM, q2, zero),
                              jnp.where(lane >= HEAD_DIM, q2, zero)], axis=0)
        st_sc[slot] = lax.dot_general(
            keys[j * WINDOW:j * WINDOW + band, g * LANES:(g + 1) * LANES], qs,
            (((1,), (1,)), ((), ())), preferred_element_type=F32)

    def softmax(item, st_slot, slot):
        j, p = item
        st = st_sc[st_slot] + bias_sc[first if j == 0 else 0, p]
        sink = jnp.where(col < WINDOW, sink_ref[2 * p], sink_ref[2 * p + 1]) * LOG2_E
        m = jnp.maximum(_reduce_rows(st, jnp.maximum), sink)
        e_sc[slot] = jnp.exp2(st - m).astype(BF16)
        return jnp.exp2(sink - m)

    def values(item, slot, sink_term):
        j, p = item
        g = p // pairs_per_kv
        vt1 = jnp.concatenate(
            [vts[g * HEAD_DIM:(g + 1) * HEAD_DIM, j * WINDOW:j * WINDOW + band],
             jnp.ones((SUM_ROWS, band), BF16)], axis=0)
        acc = jnp.dot(vt1, e_sc[slot], preferred_element_type=F32)
        ot = acc[:HEAD_DIM] / (acc[HEAD_DIM:HEAD_DIM + 1] + sink_term)
        o2 = jnp.concatenate([ot[:, :WINDOW], ot[:, WINDOW:]], axis=0).T
        o_ref[j * WINDOW:(j + 1) * WINDOW, p * LANES:(p + 1) * LANES] = o2.astype(BF16)

    for t in range(3):
        scores(items[t], t)
    sink_term = softmax(items[0], 0, 0)
    for t, item in enumerate(items):
        if t + 3 < len(items):
            scores(items[t + 3], t % 3)
        values(item, t % 2, sink_term)
        if t + 1 < len(items):
            sink_term = softmax(items[t + 1], (t + 1) % 3, (t + 1) % 2)


def _swa_attn(sinks, rel_bias, q, kd, vt, bsz, seq, n_blocks):
    nblk = seq // WINDOW
    steps = nblk // n_blocks
    kvw = KV_HEADS * LANES
    vtw = KV_HEADS * HEAD_DIM
    tq = n_blocks * WINDOW
    prev_block = lambda b, n: b * nblk + jnp.maximum(n * n_blocks - 1, 0)
    smem = pl.BlockSpec(memory_space=pltpu.SMEM)
    return pl.pallas_call(
        functools.partial(_swa_attn_kernel, n_blocks=n_blocks),
        grid=(bsz, steps),
        in_specs=[
            smem, smem,
            pl.BlockSpec((tq, D_MODEL), lambda b, n: (b * steps + n, 0)),
            pl.BlockSpec((WINDOW, kvw), lambda b, n: (prev_block(b, n), 0)),
            pl.BlockSpec((tq, kvw), lambda b, n: (b * steps + n, 0)),
            pl.BlockSpec((vtw, WINDOW), lambda b, n: (0, prev_block(b, n))),
            pl.BlockSpec((vtw, tq), lambda b, n: (0, b * steps + n)),
        ],
        out_specs=pl.BlockSpec((tq, D_MODEL), lambda b, n: (b * steps + n, 0)),
        out_shape=jax.ShapeDtypeStruct((bsz * seq, D_MODEL), BF16),
        scratch_shapes=[pltpu.VMEM((2, N_PAIRS, 2 * WINDOW, 2 * WINDOW), F32),
                        pltpu.VMEM((3, 2 * WINDOW, 2 * WINDOW), F32),
                        pltpu.VMEM((2, 2 * WINDOW, 2 * WINDOW), BF16)],
        compiler_params=pltpu.CompilerParams(
            dimension_semantics=("arbitrary", "arbitrary")),
        name="swa_attn",
    )(sinks, rel_bias, q, kd, kd, vt, vt)


def kernel(x, g_attn, g_mlp, w_in_a, b_f, gq_a, gk_a, w_out_a, g_kv, w_kv, gk_b,
           w_q_b, gq_b, sinks, rel_bias, w_out_b, w_up, w_down):
    bsz, seq, d = x.shape
    t = bsz * seq
    row = lambda g: g.reshape(1, -1).astype(F32)

    x2 = x.reshape(t, d)
    qt, k, vt, f3 = _fox_pre(x2, row(g_attn[0]), w_in_a[0].T.astype(F32),
                            row(gq_a[0]), row(gk_a[0]), tm=PRE_TOKENS)
    caug = _fox_decay(f3, row(b_f[0]), bsz, seq)
    o, (w_up_b, w_down_b, w_out_a_b, w_out_b_b, w_q_b_b), wkv = _fox_attn(
        qt, k, vt, caug, [w_up, w_down, w_out_a, w_out_b, w_q_b], w_kv.astype(F32),
        bsz, seq, tq=FOX_TQ, tk=FOX_TK)
    nxt = (row(g_attn[1]), row(g_kv), w_q_b_b, wkv, row(gq_b[0]), row(gk_b))
    h, qb, kd, vt_b = _post_mlp(x2, o, w_out_a_b, row(g_mlp[0]),
                                w_up_b, w_down_b, 0, nxt, tm=MLP_TOKENS)

    o2 = _swa_attn(sinks[0].astype(F32), rel_bias.astype(F32), qb, kd, vt_b,
                   bsz, seq, n_blocks=SWA_BLOCKS)
    (out,) = _post_mlp(h, o2, w_out_b_b, row(g_mlp[1]),
                       w_up_b, w_down_b, 1, None, tm=MLP_TOKENS)
    return out.reshape(bsz, seq, d)
```

```python
import functools

import numpy as np
import jax
import jax.numpy as jnp
from jax import lax
from jax.experimental import pallas as pl
from jax.experimental.pallas import tpu as pltpu

D_MODEL = 1024
HEAD_DIM = 64
N_HEADS = 16
N_PAIRS = N_HEADS // 2
KV_HEADS = 2
WINDOW = 128
D_FF = 4 * D_MODEL
N_BUCKETS = 32
REL_MAX_DIST = 128
NORM_EPS = 1e-6
LANES = 128
MXU_N = 256
QK_SCALE = HEAD_DIM ** -0.5
LOG2_E = 1.4426950408889634
N_SPLIT = 3
SUM_ROWS = 16
ROW_CHUNK = 32

F32 = jnp.float32
BF16 = jnp.bfloat16
NEG_INF = float("-inf")

PRE_TOKENS = 1024
MLP_TOKENS = 512
MLP_FF_CHUNK = 1024
FOX_TQ, FOX_TK = 256, 512
FOX_SCORES_AHEAD = 2
FOX_PROBS_AHEAD = 1
SWA_BLOCKS = 8
VMEM_LIMIT = 56 * 1024 * 1024


def _rms_scale(x):
    return lax.rsqrt(jnp.mean(x * x, axis=-1, keepdims=True) + NORM_EPS)


def _head_pair_norm(y, g):
    g2 = jnp.concatenate([g, g], axis=1)
    lo = lax.broadcasted_iota(jnp.int32, (1, LANES), 1) < HEAD_DIM
    sq = y * y
    s_lo = jnp.sum(jnp.where(lo, sq, 0.0), axis=-1, keepdims=True)
    s_hi = jnp.sum(jnp.where(lo, 0.0, sq), axis=-1, keepdims=True)
    ms = jnp.where(lo, s_lo, s_hi) * (1.0 / HEAD_DIM)
    return (y * lax.rsqrt(ms + NORM_EPS)) * g2


def _split_bf16(x):
    terms = []
    for _ in range(N_SPLIT):
        t = x.astype(BF16)
        terms.append(t)
        x = x - t.astype(F32)
    return terms


def _reduce_rows(x, op, group=32):
    acc = x[0:group]
    for r in range(group, x.shape[0], group):
        acc = op(acc, x[r:r + group])
    red = jnp.max if op is jnp.maximum else jnp.sum
    return red(acc, axis=0, keepdims=True)


def _resident(shape):
    zeros = (0,) * len(shape)
    return pl.BlockSpec(shape, lambda *_: zeros, pipeline_mode=pl.Buffered(1))


def _layer(shape, layer):
    index = (layer,) + (0,) * len(shape)
    return pl.BlockSpec((None,) + tuple(shape), lambda *_: index,
                        pipeline_mode=pl.Buffered(1))


def _fox_pre_kernel(x_ref, g_ref, w_ref, gq_ref, gk_ref,
                    qt_ref, k_ref, vt_ref, f_ref):
    x = x_ref[...]
    hn = ((x * _rms_scale(x)) * g_ref[0:1, :]).astype(BF16)
    n_chunks = D_MODEL // MXU_N
    for c in range(3 * n_chunks):
        y = lax.dot_general(
            hn, w_ref[c * MXU_N:(c + 1) * MXU_N, :].astype(BF16),
            (((1,), (1,)), ((), ())), preferred_element_type=F32)
        part, cc = divmod(c, n_chunks)
        rows = slice(cc * MXU_N, (cc + 1) * MXU_N)
        if part == 2:
            vt_ref[rows, :] = y.T.astype(BF16)
            continue
        g2 = gq_ref[...] if part == 0 else gk_ref[...]
        yn = jnp.concatenate(
            [_head_pair_norm(y[:, h * LANES:(h + 1) * LANES], g2)
             for h in range(MXU_N // LANES)], axis=1)
        if part == 0:
            qt_ref[rows, :] = (yn * (QK_SCALE * LOG2_E)).T.astype(BF16)
        else:
            for h in range(MXU_N // LANES):
                k_ref[cc * (MXU_N // LANES) + h] = yn[:, h * LANES:(h + 1) * LANES].astype(BF16)
    w_f = w_ref[3 * D_MODEL:3 * D_MODEL + N_HEADS, :]
    w_f3 = jnp.concatenate(
        [w_f] * N_SPLIT + [jnp.zeros((LANES - N_SPLIT * N_HEADS, D_MODEL), F32)],
        axis=0).astype(BF16)
    f_ref[...] = lax.dot_general(hn, w_f3, (((1,), (1,)), ((), ())),
                                 preferred_element_type=F32)


def _fox_pre(x2, g, w_in, gq2, gk2, tm):
    t = x2.shape[0]
    tok = lambda w: pl.BlockSpec((tm, w), lambda i: (i, 0))
    return pl.pallas_call(
        _fox_pre_kernel,
        grid=(t // tm,),
        in_specs=[tok(D_MODEL), _resident(g.shape),
                  _resident(w_in.shape),
                  _resident((1, HEAD_DIM)), _resident((1, HEAD_DIM))],
        out_specs=[pl.BlockSpec((D_MODEL, tm), lambda i: (0, i)),
                   pl.BlockSpec((N_PAIRS, tm, LANES), lambda i: (0, i, 0)),
                   pl.BlockSpec((D_MODEL, tm), lambda i: (0, i)), tok(LANES)],
        out_shape=[jax.ShapeDtypeStruct((D_MODEL, t), BF16),
                   jax.ShapeDtypeStruct((N_PAIRS, t, LANES), BF16),
                   jax.ShapeDtypeStruct((D_MODEL, t), BF16),
                   jax.ShapeDtypeStruct((t, LANES), F32)],
        compiler_params=pltpu.CompilerParams(
            dimension_semantics=("parallel",), vmem_limit_bytes=VMEM_LIMIT),
        name="fox_pre",
    )(x2, g, w_in, gq2, gk2)


def _fox_decay_kernel(f_ref, b_ref, c_ref, *, blk):
    seq = f_ref.shape[0]
    lane = lax.broadcasted_iota(jnp.int32, (1, LANES), 1)
    r = lax.broadcasted_iota(jnp.int32, (blk, blk), 0)
    c = lax.broadcasted_iota(jnp.int32, (blk, blk), 1)
    lower = jnp.where(r >= c, 1.0, 0.0).astype(BF16)
    bias = jnp.concatenate(
        [b_ref[...]] * N_SPLIT + [jnp.zeros((1, LANES - N_SPLIT * N_HEADS), F32)], axis=1)
    carry = jnp.zeros((1, LANES), F32)
    for b in range(seq // blk):
        rows = slice(b * blk, (b + 1) * blk)
        x = f_ref[rows, :] + bias
        log_f = -(jnp.maximum(-x, 0.0) + jnp.log1p(jnp.exp(-jnp.abs(x))))
        cb = carry
        for term in _split_bf16(log_f):
            cb = cb + jnp.dot(lower, term, preferred_element_type=F32)
        carry = cb[blk - 1:blk, :]
        out = jnp.zeros((blk, LANES), BF16)
        for t, term in reversed(list(enumerate(_split_bf16(cb * LOG2_E)))):
            out = jnp.where(lane < (t + 1) * N_HEADS, term, out)
        c_ref[rows, :] = out


def _fox_decay(f3, b3, bsz, seq):
    spec = pl.BlockSpec((seq, LANES), lambda b: (b, 0))
    return pl.pallas_call(
        functools.partial(_fox_decay_kernel, blk=MXU_N),
        grid=(bsz,),
        in_specs=[spec, _resident((1, N_HEADS))],
        out_specs=spec,
        out_shape=jax.ShapeDtypeStruct(f3.shape, BF16),
        compiler_params=pltpu.CompilerParams(dimension_semantics=("parallel",)),
        name="fox_decay",
    )(f3, b3)


def _fox_attn_kernel(qt_ref, k_ref, vt_ref, c_ref, *rest, tq, tk, seq, n_cast,
                     s_ahead, e_ahead):
    w_refs, wkv_ref, o_ref = rest[:n_cast], rest[n_cast], rest[n_cast + 1]
    wb_refs, wkvb_ref = rest[n_cast + 2:2 * n_cast + 2], rest[2 * n_cast + 2]
    st_sc, pt_sc = rest[2 * n_cast + 3:]
    for w_ref, wb_ref in zip(w_refs, wb_refs):
        wb_ref[...] = w_ref[...].astype(BF16)
    wkv = wkv_ref[...]
    k_heads = [wkv[:, j * HEAD_DIM:(j + 1) * HEAD_DIM] for j in range(KV_HEADS)]
    wkvb_ref[...] = jnp.concatenate(
        [kh for kh in k_heads for _ in range(2)] + [wkv[:, KV_HEADS * HEAD_DIM:]],
        axis=1).astype(BF16)
    p = pl.program_id(1)
    feat = lax.broadcasted_iota(jnp.int32, (LANES, 1), 0)
    own = [feat < HEAD_DIM, feat >= HEAD_DIM]
    neg = [jnp.broadcast_to(
        jnp.where((feat < N_SPLIT * N_HEADS)
                  & ((feat & (N_HEADS - 1)) == 2 * p + hh), -1.0, 0.0).astype(BF16),
        (LANES, tq)) for hh in range(2)]

    items = []
    for i in range(seq // tq):
        n_keys = (i + 1) * tq
        starts = list(range(0, n_keys, tk))
        for ks in starts:
            items.append((i, ks, min(tk, n_keys - ks), ks == starts[-1]))

    def query_blocks(i):
        q2 = qt_ref[:, i * tq:(i + 1) * tq]
        return [jnp.concatenate(
            [jnp.where(own[hh], q2, jnp.zeros_like(q2)), neg[hh]], axis=0)
            for hh in range(2)]

    def scores(item, qa, slot, col_max):
        i, ks, width, is_last = item
        ka = jnp.concatenate([k_ref[ks:ks + width, :], c_ref[ks:ks + width, :]],
                             axis=1)
        if is_last:
            krow = lax.broadcasted_iota(jnp.int32, (width, tq), 0)
            qcol = lax.broadcasted_iota(jnp.int32, (width, tq), 1)
            visible = krow <= qcol + (i * tq - ks)

        def one(hh):
            st = jnp.dot(ka, qa[hh], preferred_element_type=F32)
            if is_last:
                st = jnp.where(visible, st, NEG_INF)
            st_sc[slot, hh, 0:width, :] = st
            col_max[hh] = _reduce_rows(st, jnp.maximum)
        return [functools.partial(one, hh) for hh in range(2)]

    def probabilities(item, t, col_max, m_old, res):
        width = item[2]
        chunks = list(range(0, width, ROW_CHUNK))
        pieces = []
        for hh in range(2):
            src = st_sc.at[t % n_st, hh]

            def new_max(hh=hh):
                m_new = jnp.maximum(m_old[hh], col_max[hh])
                res[hh] = (m_new, jnp.exp2(m_old[hh] - m_new))

            def exps(rows, hh=hh, src=src):
                for r in rows:
                    pt_sc[t % n_pt, hh, r:r + ROW_CHUNK, :] = jnp.exp2(
                        src[r:r + ROW_CHUNK, :] - res[hh][0]).astype(BF16)

            half = len(chunks) // 2
            pieces += [new_max, functools.partial(exps, chunks[:half]),
                       functools.partial(exps, chunks[half:])]
        return pieces

    def values(item, t, alpha, acc_old, new):
        _, ks, width, _ = item

        def one(hh):
            vt = jnp.concatenate(
                [vt_ref[hh * HEAD_DIM:(hh + 1) * HEAD_DIM, ks:ks + width],
                 jnp.ones((SUM_ROWS, width), BF16)], axis=0)
            new[hh] = alpha[hh] * acc_old[hh] + jnp.dot(
                vt, pt_sc[t % n_pt, hh, 0:width, :],
                preferred_element_type=F32)
        return [functools.partial(one, hh) for hh in range(2)]

    n_items = len(items)
    n_st, n_pt = st_sc.shape[0], pt_sc.shape[0]
    assert n_st > s_ahead - e_ahead and n_pt > e_ahead
    m_fresh = [jnp.full((1, tq), NEG_INF, F32)] * 2
    acc_fresh = [jnp.zeros((HEAD_DIM + SUM_ROWS, tq), F32)] * 2
    alpha, col_max = {}, {}
    qa_tile, qa = -1, None
    m_run, acc_run = m_fresh, acc_fresh
    for t in range(-s_ahead, n_items):
        matmuls, vector = [], []
        u, e = t + s_ahead, t + e_ahead
        if u < n_items:
            if items[u][0] != qa_tile:
                qa_tile, qa = items[u][0], query_blocks(items[u][0])
            col_max[u] = [None, None]
            matmuls += scores(items[u], qa, u % n_st, col_max[u])
        acc_new = [None, None]
        if t >= 0:
            matmuls += values(items[t], t, alpha.pop(t), acc_run, acc_new)
        res = [None, None]
        if 0 <= e < n_items:
            vector = probabilities(items[e], e, col_max.pop(e), m_run, res)
        per = -(-len(vector) // max(len(matmuls), 1))
        for n, mm in enumerate(matmuls):
            mm()
            for piece in vector[n * per:(n + 1) * per]:
                piece()
        for piece in vector[len(matmuls) * per:]:
            piece()
        if t >= 0:
            acc_run = acc_new
            if items[t][3]:
                i = items[t][0]
                ot = jnp.concatenate(
                    [acc[:HEAD_DIM] / acc[HEAD_DIM:HEAD_DIM + 1] for acc in acc_run],
                    axis=0)
                o_ref[i * tq:(i + 1) * tq, :] = ot.T.astype(BF16)
                acc_run = acc_fresh
        if 0 <= e < n_items:
            m_run = m_fresh if items[e][3] else [r[0] for r in res]
            alpha[e] = [r[1] for r in res]


def _fox_attn(qt, k, vt, caug, weights, w_kv, bsz, seq, tq, tk,
              s_ahead=FOX_SCORES_AHEAD, e_ahead=FOX_PROBS_AHEAD):
    tok = pl.BlockSpec((None, seq, LANES), lambda b, p: (p, b, 0))
    feat = pl.BlockSpec((LANES, seq), lambda b, p: (p, b))
    steps = bsz * N_PAIRS
    flat = [w.reshape(-1, w.shape[-1]) for w in weights]
    slab = lambda rows, cols: pl.BlockSpec((rows // steps, cols),
                                           lambda b, p: (b * N_PAIRS + p, 0))
    slabs = [slab(*w.shape) for w in flat]
    kv_rows, kv_cols = w_kv.shape
    kv_out_cols = kv_cols + KV_HEADS * HEAD_DIM
    outs = pl.pallas_call(
        functools.partial(_fox_attn_kernel, tq=tq, tk=tk, seq=seq,
                          n_cast=len(flat), s_ahead=s_ahead, e_ahead=e_ahead),
        grid=(bsz, N_PAIRS),
        in_specs=[feat, tok, feat,
                  pl.BlockSpec((seq, LANES), lambda b, p: (b, 0))]
        + slabs + [slab(kv_rows, kv_cols)],
        out_specs=[tok] + slabs + [slab(kv_rows, kv_out_cols)],
        out_shape=[jax.ShapeDtypeStruct(k.shape, BF16)]
        + [jax.ShapeDtypeStruct(w.shape, BF16) for w in flat]
        + [jax.ShapeDtypeStruct((kv_rows, kv_out_cols), BF16)],
        scratch_shapes=[pltpu.VMEM((s_ahead - e_ahead + 2, 2, tk, tq), F32),
                        pltpu.VMEM((e_ahead + 1, 2, tk, tq), BF16)],
        compiler_params=pltpu.CompilerParams(
            dimension_semantics=("parallel", "parallel")),
        name="fox_attn",
    )(qt, k, vt, caug, *flat, w_kv)
    return (outs[0], [wb.reshape(w.shape) for wb, w in zip(outs[1:-1], weights)],
            outs[-1])


def _post_mlp_kernel(*refs, layer, emit_next, o_by_pair, ff_chunk):
    if emit_next:
        (h_ref, o_ref, wo_ref, gm_ref, wup_ref, wdn_ref,
         ga_ref, gkv_ref, wq_ref, wkv_ref, gq_ref, gk_ref,
         out_ref, qb_ref, kd_ref, vt_ref) = refs
    else:
        h_ref, o_ref, wo_ref, gm_ref, wup_ref, wdn_ref, out_ref = refs
    if o_by_pair:
        o = jnp.concatenate([o_ref[p] for p in range(N_PAIRS)], axis=1)
    else:
        o = o_ref[...]
    h1 = h_ref[...] + jnp.dot(o, wo_ref[...], preferred_element_type=F32)
    m = ((h1 * _rms_scale(h1)) * gm_ref[layer:layer + 1, :]).astype(BF16)
    acc = h1
    for c in range(D_FF // ff_chunk):
        u = jnp.dot(m, wup_ref[:, c * ff_chunk:(c + 1) * ff_chunk],
                    preferred_element_type=F32)
        u = jnp.square(jnp.maximum(u, 0.0)).astype(BF16)
        acc = acc + jnp.dot(u, wdn_ref[c * ff_chunk:(c + 1) * ff_chunk, :],
                            preferred_element_type=F32)
    out_ref[...] = acc
    if emit_next:
        hn = acc * _rms_scale(acc)
        a = (hn * ga_ref[layer + 1:layer + 2, :]).astype(BF16)
        for c in range(D_MODEL // MXU_N):
            y = jnp.dot(a, wq_ref[:, c * MXU_N:(c + 1) * MXU_N],
                        preferred_element_type=F32)
            for half in range(MXU_N // LANES):
                yy = y[:, half * LANES:(half + 1) * LANES]
                dst = pl.ds(c * MXU_N + half * LANES, LANES)
                qb_ref[:, dst] = (_head_pair_norm(yy, gq_ref[...])
                                  * (QK_SCALE * LOG2_E)).astype(BF16)
        kvn = (hn * gkv_ref[...]).astype(BF16)
        kv = jnp.dot(kvn, wkv_ref[...], preferred_element_type=F32)
        for j in range(KV_HEADS):
            sl = slice(j * LANES, (j + 1) * LANES)
            kd_ref[:, sl] = _head_pair_norm(kv[:, sl], gk_ref[...]).astype(BF16)
        vt_ref[...] = kv[:, KV_HEADS * LANES:].T.astype(BF16)


def _post_mlp(h, o, wo, gm, wup, wdn, layer, nxt, tm, ff_chunk=MLP_FF_CHUNK):
    t = h.shape[0]
    tok = lambda w: pl.BlockSpec((tm, w), lambda i: (i, 0))
    o_by_pair = o.ndim == 3
    o_spec = (pl.BlockSpec((N_PAIRS, tm, LANES), lambda i: (0, i, 0))
              if o_by_pair else tok(D_MODEL))
    in_specs = [tok(D_MODEL), o_spec, _layer((D_MODEL, D_MODEL), 0),
                _resident(gm.shape), _layer((D_MODEL, D_FF), layer),
                _layer((D_FF, D_MODEL), layer)]
    out_specs = [tok(D_MODEL)]
    out_shape = [jax.ShapeDtypeStruct((t, D_MODEL), F32)]
    args = [h, o, wo, gm, wup, wdn]
    if nxt is not None:
        kvw = KV_HEADS * LANES + KV_HEADS * HEAD_DIM
        in_specs += [_resident(nxt[0].shape), _resident((1, D_MODEL)),
                     _layer((D_MODEL, D_MODEL), 0), _resident((D_MODEL, kvw)),
                     _resident((1, HEAD_DIM)), _resident((1, HEAD_DIM))]
        out_specs += [tok(D_MODEL), tok(KV_HEADS * LANES),
                      pl.BlockSpec((KV_HEADS * HEAD_DIM, tm), lambda i: (0, i))]
        out_shape += [jax.ShapeDtypeStruct((t, D_MODEL), BF16),
                      jax.ShapeDtypeStruct((t, KV_HEADS * LANES), BF16),
                      jax.ShapeDtypeStruct((KV_HEADS * HEAD_DIM, t), BF16)]
        args += list(nxt)
    return pl.pallas_call(
        functools.partial(_post_mlp_kernel, layer=layer, emit_next=nxt is not None,
                          o_by_pair=o_by_pair, ff_chunk=ff_chunk),
        grid=(t // tm,),
        in_specs=in_specs, out_specs=out_specs, out_shape=out_shape,
        compiler_params=pltpu.CompilerParams(
            dimension_semantics=("parallel",), vmem_limit_bytes=VMEM_LIMIT),
        name="post_mlp_next" if nxt is not None else "post_mlp",
    )(*args)


def _t5_causal_bucket(dist):
    n = np.maximum(dist, 0)
    max_exact = N_BUCKETS // 2
    large = max_exact + (np.log(np.maximum(n, 1) / max_exact)
                         / np.log(REL_MAX_DIST / max_exact)
                         * (N_BUCKETS - max_exact)).astype(np.int32)
    large = np.minimum(large, N_BUCKETS - 1)
    return np.where(n < max_exact, n, large).astype(np.int32)


def _bucket_ranges():
    buckets = _t5_causal_bucket(np.arange(WINDOW))
    assert np.all(np.diff(buckets) >= 0)
    out = []
    for k in np.unique(buckets):
        idx = np.nonzero(buckets == k)[0]
        out.append((int(k), int(idx[0]), int(idx[-1]) + 1))
    return out


def _swa_attn_kernel(sink_ref, rb_ref, q_ref, kp_ref, kc_ref, vtp_ref, vtc_ref,
                     o_ref, bias_sc, st_sc, e_sc, *, n_blocks):
    n = pl.program_id(1)
    band = 2 * WINDOW

    @pl.when((pl.program_id(0) == 0) & (n == 0))
    def _():
        krow = lax.broadcasted_iota(jnp.int32, (band, WINDOW), 0)
        qcol = lax.broadcasted_iota(jnp.int32, (band, WINDOW), 1)
        dist = qcol + WINDOW - krow
        for h in range(N_HEADS):
            tile = jnp.full((band, WINDOW), NEG_INF, F32)
            for k, lo, hi in _bucket_ranges():
                tile = jnp.where((dist >= lo) & (dist < hi),
                                 rb_ref[h, k] * LOG2_E, tile)
            cols = slice((h % 2) * WINDOW, (h % 2 + 1) * WINDOW)
            bias_sc[0, h // 2, :, cols] = tile
            bias_sc[1, h // 2, :, cols] = jnp.where(krow < WINDOW, NEG_INF, tile)

    first = (n == 0).astype(jnp.int32)
    keys = jnp.concatenate([kp_ref[...], kc_ref[...]], axis=0)
    vts = jnp.concatenate([vtp_ref[...], vtc_ref[...]], axis=1)
    lane = lax.broadcasted_iota(jnp.int32, (1, LANES), 1)
    col = lax.broadcasted_iota(jnp.int32, (1, band), 1)
    pairs_per_kv = N_PAIRS // KV_HEADS
    items = [(j, p) for j in range(n_blocks) for p in range(N_PAIRS)]

    def scores(item, slot):
        j, p = item
        g = p // pairs_per_kv
        q2 = q_ref[j * WINDOW:(j + 1) * WINDOW, p * LANES:(p + 1) * LANES]
        zero = jnp.zeros_like(q2)
        qs = jnp.concatenate([jnp.where(lane < HEAD_DIM, q2, zero),
                              jnp.where(lane >= HEAD_DIM, q2, zero)], axis=0)
        st_sc[slot] = lax.dot_general(
            keys[j * WINDOW:j * WINDOW + band, g * LANES:(g + 1) * LANES], qs,
            (((1,), (1,)), ((), ())), preferred_element_type=F32)

    def softmax(item, st_slot, slot):
        j, p = item
        st = st_sc[st_slot] + bias_sc[first if j == 0 else 0, p]
        sink = jnp.where(col < WINDOW, sink_ref[2 * p], sink_ref[2 * p + 1]) * LOG2_E
        m = jnp.maximum(_reduce_rows(st, jnp.maximum), sink)
        e_sc[slot] = jnp.exp2(st - m).astype(BF16)
        return jnp.exp2(sink - m)

    def values(item, slot, sink_term):
        j, p = item
        g = p // pairs_per_kv
        vt1 = jnp.concatenate(
            [vts[g * HEAD_DIM:(g + 1) * HEAD_DIM, j * WINDOW:j * WINDOW + band],
             jnp.ones((SUM_ROWS, band), BF16)], axis=0)
        acc = jnp.dot(vt1, e_sc[slot], preferred_element_type=F32)
        ot = acc[:HEAD_DIM] / (acc[HEAD_DIM:HEAD_DIM + 1] + sink_term)
        o2 = jnp.concatenate([ot[:, :WINDOW], ot[:, WINDOW:]], axis=0).T
        o_ref[j * WINDOW:(j + 1) * WINDOW, p * LANES:(p + 1) * LANES] = o2.astype(BF16)

    for t in range(3):
        scores(items[t], t)
    sink_term = softmax(items[0], 0, 0)
    for t, item in enumerate(items):
        if t + 3 < len(items):
            scores(items[t + 3], t % 3)
        values(item, t % 2, sink_term)
        if t + 1 < len(items):
            sink_term = softmax(items[t + 1], (t + 1) % 3, (t + 1) % 2)


def _swa_attn(sinks, rel_bias, q, kd, vt, bsz, seq, n_blocks):
    nblk = seq // WINDOW
    steps = nblk // n_blocks
    kvw = KV_HEADS * LANES
    vtw = KV_HEADS * HEAD_DIM
    tq = n_blocks * WINDOW
    prev_block = lambda b, n: b * nblk + jnp.maximum(n * n_blocks - 1, 0)
    smem = pl.BlockSpec(memory_space=pltpu.SMEM)
    return pl.pallas_call(
        functools.partial(_swa_attn_kernel, n_blocks=n_blocks),
        grid=(bsz, steps),
        in_specs=[
            smem, smem,
            pl.BlockSpec((tq, D_MODEL), lambda b, n: (b * steps + n, 0)),
            pl.BlockSpec((WINDOW, kvw), lambda b, n: (prev_block(b, n), 0)),
            pl.BlockSpec((tq, kvw), lambda b, n: (b * steps + n, 0)),
            pl.BlockSpec((vtw, WINDOW), lambda b, n: (0, prev_block(b, n))),
            pl.BlockSpec((vtw, tq), lambda b, n: (0, b * steps + n)),
        ],
        out_specs=pl.BlockSpec((tq, D_MODEL), lambda b, n: (b * steps + n, 0)),
        out_shape=jax.ShapeDtypeStruct((bsz * seq, D_MODEL), BF16),
        scratch_shapes=[pltpu.VMEM((2, N_PAIRS, 2 * WINDOW, 2 * WINDOW), F32),
                        pltpu.VMEM((3, 2 * WINDOW, 2 * WINDOW), F32),
                        pltpu.VMEM((2, 2 * WINDOW, 2 * WINDOW), BF16)],
        compiler_params=pltpu.CompilerParams(
            dimension_semantics=("arbitrary", "arbitrary")),
        name="swa_attn",
    )(sinks, rel_bias, q, kd, kd, vt, vt)


def kernel(x, g_attn, g_mlp, w_in_a, b_f, gq_a, gk_a, w_out_a, g_kv, w_kv, gk_b,
           w_q_b, gq_b, sinks, rel_bias, w_out_b, w_up, w_down):
    bsz, seq, d = x.shape
    t = bsz * seq
    row = lambda g: g.reshape(1, -1).astype(F32)

    x2 = x.reshape(t, d)
    qt, k, vt, f3 = _fox_pre(x2, g_attn.astype(F32), w_in_a[0].T.astype(F32),
                            row(gq_a[0]), row(gk_a[0]), tm=PRE_TOKENS)
    caug = _fox_decay(f3, row(b_f[0]), bsz, seq)
    o, (w_up_b, w_down_b, w_out_a_b, w_out_b_b, w_q_b_b), wkv = _fox_attn(
        qt, k, vt, caug, [w_up, w_down, w_out_a, w_out_b, w_q_b], w_kv.astype(F32),
        bsz, seq, tq=FOX_TQ, tk=FOX_TK)
    nxt = (g_attn.astype(F32), row(g_kv), w_q_b_b, wkv, row(gq_b[0]), row(gk_b))
    h, qb, kd, vt_b = _post_mlp(x2, o, w_out_a_b, g_mlp.astype(F32),
                                w_up_b, w_down_b, 0, nxt, tm=MLP_TOKENS)

    o2 = _swa_attn(sinks[0].astype(F32), rel_bias.T.astype(F32), qb, kd, vt_b,
                   bsz, seq, n_blocks=SWA_BLOCKS)
    (out,) = _post_mlp(h, o2, w_out_b_b, g_mlp.astype(F32),
                       w_up_b, w_down_b, 1, None, tm=MLP_TOKENS)
    return out.reshape(bsz, seq, d)
```

```python
import functools

import numpy as np
import jax
import jax.numpy as jnp
from jax import lax
from jax.experimental import pallas as pl
from jax.experimental.pallas import tpu as pltpu

D_MODEL = 1024
HEAD_DIM = 64
N_HEADS = 16
N_PAIRS = N_HEADS // 2
KV_HEADS = 2
WINDOW = 128
D_FF = 4 * D_MODEL
N_BUCKETS = 32
REL_MAX_DIST = 128
NORM_EPS = 1e-6
LANES = 128
MXU_N = 256
QK_SCALE = HEAD_DIM ** -0.5
LOG2_E = 1.4426950408889634
N_SPLIT = 3
SUM_ROWS = 16
ROW_CHUNK = 32

F32 = jnp.float32
BF16 = jnp.bfloat16
NEG_INF = float("-inf")

PRE_TOKENS = 1024
MLP_TOKENS = 512
MLP_FF_CHUNK = 1024
FOX_TQ, FOX_TK = 256, 512
FOX_SCORES_AHEAD = 2
FOX_PROBS_AHEAD = 1
SWA_BLOCKS = 16
VMEM_LIMIT = 56 * 1024 * 1024


def _rms_scale(x):
    return lax.rsqrt(jnp.mean(x * x, axis=-1, keepdims=True) + NORM_EPS)


def _head_pair_norm(y, g):
    g2 = jnp.concatenate([g, g], axis=1)
    lo = lax.broadcasted_iota(jnp.int32, (1, LANES), 1) < HEAD_DIM
    sq = y * y
    s_lo = jnp.sum(jnp.where(lo, sq, 0.0), axis=-1, keepdims=True)
    s_hi = jnp.sum(jnp.where(lo, 0.0, sq), axis=-1, keepdims=True)
    ms = jnp.where(lo, s_lo, s_hi) * (1.0 / HEAD_DIM)
    return (y * lax.rsqrt(ms + NORM_EPS)) * g2


def _split_bf16(x):
    terms = []
    for _ in range(N_SPLIT):
        t = x.astype(BF16)
        terms.append(t)
        x = x - t.astype(F32)
    return terms


def _reduce_rows(x, op, group=32):
    acc = x[0:group]
    for r in range(group, x.shape[0], group):
        acc = op(acc, x[r:r + group])
    red = jnp.max if op is jnp.maximum else jnp.sum
    return red(acc, axis=0, keepdims=True)


def _resident(shape):
    zeros = (0,) * len(shape)
    return pl.BlockSpec(shape, lambda *_: zeros, pipeline_mode=pl.Buffered(1))


def _layer(shape, layer):
    index = (layer,) + (0,) * len(shape)
    return pl.BlockSpec((None,) + tuple(shape), lambda *_: index,
                        pipeline_mode=pl.Buffered(1))


def _fox_pre_kernel(x_ref, g_ref, w_ref, gq_ref, gk_ref,
                    qt_ref, k_ref, vt_ref, f_ref):
    x = x_ref[...]
    hn = ((x * _rms_scale(x)) * g_ref[0:1, :]).astype(BF16)
    n_chunks = D_MODEL // MXU_N
    for c in range(3 * n_chunks):
        y = lax.dot_general(
            hn, w_ref[c * MXU_N:(c + 1) * MXU_N, :].astype(BF16),
            (((1,), (1,)), ((), ())), preferred_element_type=F32)
        part, cc = divmod(c, n_chunks)
        rows = slice(cc * MXU_N, (cc + 1) * MXU_N)
        if part == 2:
            vt_ref[rows, :] = y.T.astype(BF16)
            continue
        g2 = gq_ref[...] if part == 0 else gk_ref[...]
        yn = jnp.concatenate(
            [_head_pair_norm(y[:, h * LANES:(h + 1) * LANES], g2)
             for h in range(MXU_N // LANES)], axis=1)
        if part == 0:
            qt_ref[rows, :] = (yn * (QK_SCALE * LOG2_E)).T.astype(BF16)
        else:
            for h in range(MXU_N // LANES):
                k_ref[cc * (MXU_N // LANES) + h] = yn[:, h * LANES:(h + 1) * LANES].astype(BF16)
    w_f = w_ref[3 * D_MODEL:3 * D_MODEL + N_HEADS, :]
    w_f3 = jnp.concatenate(
        [w_f] * N_SPLIT + [jnp.zeros((LANES - N_SPLIT * N_HEADS, D_MODEL), F32)],
        axis=0).astype(BF16)
    f_ref[...] = lax.dot_general(hn, w_f3, (((1,), (1,)), ((), ())),
                                 preferred_element_type=F32)


def _fox_pre(x2, g, w_in, gq2, gk2, tm):
    t = x2.shape[0]
    tok = lambda w: pl.BlockSpec((tm, w), lambda i: (i, 0))
    return pl.pallas_call(
        _fox_pre_kernel,
        grid=(t // tm,),
        in_specs=[tok(D_MODEL), _resident(g.shape),
                  _resident(w_in.shape),
                  _resident((1, HEAD_DIM)), _resident((1, HEAD_DIM))],
        out_specs=[pl.BlockSpec((D_MODEL, tm), lambda i: (0, i)),
                   pl.BlockSpec((N_PAIRS, tm, LANES), lambda i: (0, i, 0)),
                   pl.BlockSpec((D_MODEL, tm), lambda i: (0, i)), tok(LANES)],
        out_shape=[jax.ShapeDtypeStruct((D_MODEL, t), BF16),
                   jax.ShapeDtypeStruct((N_PAIRS, t, LANES), BF16),
                   jax.ShapeDtypeStruct((D_MODEL, t), BF16),
                   jax.ShapeDtypeStruct((t, LANES), F32)],
        compiler_params=pltpu.CompilerParams(
            dimension_semantics=("parallel",), vmem_limit_bytes=VMEM_LIMIT),
        name="fox_pre",
    )(x2, g, w_in, gq2, gk2)


def _fox_decay_kernel(f_ref, b_ref, c_ref, *, blk):
    seq = f_ref.shape[0]
    lane = lax.broadcasted_iota(jnp.int32, (1, LANES), 1)
    r = lax.broadcasted_iota(jnp.int32, (blk, blk), 0)
    c = lax.broadcasted_iota(jnp.int32, (blk, blk), 1)
    lower = jnp.where(r >= c, 1.0, 0.0).astype(BF16)
    bias = jnp.concatenate(
        [b_ref[...]] * N_SPLIT + [jnp.zeros((1, LANES - N_SPLIT * N_HEADS), F32)], axis=1)
    carry = jnp.zeros((1, LANES), F32)
    for b in range(seq // blk):
        rows = slice(b * blk, (b + 1) * blk)
        x = f_ref[rows, :] + bias
        log_f = -(jnp.maximum(-x, 0.0) + jnp.log1p(jnp.exp(-jnp.abs(x))))
        cb = carry
        for term in _split_bf16(log_f):
            cb = cb + jnp.dot(lower, term, preferred_element_type=F32)
        carry = cb[blk - 1:blk, :]
        out = jnp.zeros((blk, LANES), BF16)
        for t, term in reversed(list(enumerate(_split_bf16(cb * LOG2_E)))):
            out = jnp.where(lane < (t + 1) * N_HEADS, term, out)
        c_ref[rows, :] = out


def _fox_decay(f3, b3, bsz, seq):
    spec = pl.BlockSpec((seq, LANES), lambda b: (b, 0))
    return pl.pallas_call(
        functools.partial(_fox_decay_kernel, blk=MXU_N),
        grid=(bsz,),
        in_specs=[spec, _resident((1, N_HEADS))],
        out_specs=spec,
        out_shape=jax.ShapeDtypeStruct(f3.shape, BF16),
        compiler_params=pltpu.CompilerParams(dimension_semantics=("parallel",)),
        name="fox_decay",
    )(f3, b3)


def _fox_attn_kernel(qt_ref, k_ref, vt_ref, c_ref, *rest, tq, tk, seq, n_cast,
                     s_ahead, e_ahead):
    w_refs, wkv_ref, o_ref = rest[:n_cast], rest[n_cast], rest[n_cast + 1]
    wb_refs, wkvb_ref = rest[n_cast + 2:2 * n_cast + 2], rest[2 * n_cast + 2]
    st_sc, pt_sc = rest[2 * n_cast + 3:]
    for w_ref, wb_ref in zip(w_refs, wb_refs):
        wb_ref[...] = w_ref[...].astype(BF16)
    wkv = wkv_ref[...]
    k_heads = [wkv[:, j * HEAD_DIM:(j + 1) * HEAD_DIM] for j in range(KV_HEADS)]
    wkvb_ref[...] = jnp.concatenate(
        [kh for kh in k_heads for _ in range(2)] + [wkv[:, KV_HEADS * HEAD_DIM:]],
        axis=1).astype(BF16)
    p = pl.program_id(1)
    feat = lax.broadcasted_iota(jnp.int32, (LANES, 1), 0)
    own = [feat < HEAD_DIM, feat >= HEAD_DIM]
    neg = [jnp.broadcast_to(
        jnp.where((feat < N_SPLIT * N_HEADS)
                  & ((feat & (N_HEADS - 1)) == 2 * p + hh), -1.0, 0.0).astype(BF16),
        (LANES, tq)) for hh in range(2)]

    items = []
    for i in range(seq // tq):
        n_keys = (i + 1) * tq
        starts = list(range(0, n_keys, tk))
        for ks in starts:
            items.append((i, ks, min(tk, n_keys - ks), ks == starts[-1]))

    def query_blocks(i):
        q2 = qt_ref[:, i * tq:(i + 1) * tq]
        return [jnp.concatenate(
            [jnp.where(own[hh], q2, jnp.zeros_like(q2)), neg[hh]], axis=0)
            for hh in range(2)]

    def scores(item, qa, slot, col_max):
        i, ks, width, is_last = item
        ka = jnp.concatenate([k_ref[ks:ks + width, :], c_ref[ks:ks + width, :]],
                             axis=1)
        if is_last:
            krow = lax.broadcasted_iota(jnp.int32, (width, tq), 0)
            qcol = lax.broadcasted_iota(jnp.int32, (width, tq), 1)
            visible = krow <= qcol + (i * tq - ks)

        def one(hh):
            st = jnp.dot(ka, qa[hh], preferred_element_type=F32)
            if is_last:
                st = jnp.where(visible, st, NEG_INF)
            st_sc[slot, hh, 0:width, :] = st
            col_max[hh] = _reduce_rows(st, jnp.maximum)
        return [functools.partial(one, hh) for hh in range(2)]

    def probabilities(item, t, col_max, m_old, res):
        width = item[2]
        chunks = list(range(0, width, ROW_CHUNK))
        pieces = []
        for hh in range(2):
            src = st_sc.at[t % n_st, hh]

            def new_max(hh=hh):
                m_new = jnp.maximum(m_old[hh], col_max[hh])
                res[hh] = (m_new, jnp.exp2(m_old[hh] - m_new))

            def exps(rows, hh=hh, src=src):
                for r in rows:
                    pt_sc[t % n_pt, hh, r:r + ROW_CHUNK, :] = jnp.exp2(
                        src[r:r + ROW_CHUNK, :] - res[hh][0]).astype(BF16)

            half = len(chunks) // 2
            pieces += [new_max, functools.partial(exps, chunks[:half]),
                       functools.partial(exps, chunks[half:])]
        return pieces

    def values(item, t, alpha, acc_old, new):
        _, ks, width, _ = item

        def one(hh):
            vt = jnp.concatenate(
                [vt_ref[hh * HEAD_DIM:(hh + 1) * HEAD_DIM, ks:ks + width],
                 jnp.ones((SUM_ROWS, width), BF16)], axis=0)
            new[hh] = alpha[hh] * acc_old[hh] + jnp.dot(
                vt, pt_sc[t % n_pt, hh, 0:width, :],
                preferred_element_type=F32)
        return [functools.partial(one, hh) for hh in range(2)]

    n_items = len(items)
    n_st, n_pt = st_sc.shape[0], pt_sc.shape[0]
    assert n_st > s_ahead - e_ahead and n_pt > e_ahead
    m_fresh = [jnp.full((1, tq), NEG_INF, F32)] * 2
    acc_fresh = [jnp.zeros((HEAD_DIM + SUM_ROWS, tq), F32)] * 2
    alpha, col_max = {}, {}
    qa_tile, qa = -1, None
    m_run, acc_run = m_fresh, acc_fresh
    for t in range(-s_ahead, n_items):
        matmuls, vector = [], []
        u, e = t + s_ahead, t + e_ahead
        if u < n_items:
            if items[u][0] != qa_tile:
                qa_tile, qa = items[u][0], query_blocks(items[u][0])
            col_max[u] = [None, None]
            matmuls += scores(items[u], qa, u % n_st, col_max[u])
        acc_new = [None, None]
        if t >= 0:
            matmuls += values(items[t], t, alpha.pop(t), acc_run, acc_new)
        res = [None, None]
        if 0 <= e < n_items:
            vector = probabilities(items[e], e, col_max.pop(e), m_run, res)
        per = -(-len(vector) // max(len(matmuls), 1))
        for n, mm in enumerate(matmuls):
            mm()
            for piece in vector[n * per:(n + 1) * per]:
                piece()
        for piece in vector[len(matmuls) * per:]:
            piece()
        if t >= 0:
            acc_run = acc_new
            if items[t][3]:
                i = items[t][0]
                ot = jnp.concatenate(
                    [acc[:HEAD_DIM] / acc[HEAD_DIM:HEAD_DIM + 1] for acc in acc_run],
                    axis=0)
                o_ref[i * tq:(i + 1) * tq, :] = ot.T.astype(BF16)
                acc_run = acc_fresh
        if 0 <= e < n_items:
            m_run = m_fresh if items[e][3] else [r[0] for r in res]
            alpha[e] = [r[1] for r in res]


def _fox_attn(qt, k, vt, caug, weights, w_kv, bsz, seq, tq, tk,
              s_ahead=FOX_SCORES_AHEAD, e_ahead=FOX_PROBS_AHEAD):
    tok = pl.BlockSpec((None, seq, LANES), lambda b, p: (p, b, 0))
    feat = pl.BlockSpec((LANES, seq), lambda b, p: (p, b))
    steps = bsz * N_PAIRS
    flat = [w.reshape(-1, w.shape[-1]) for w in weights]
    slab = lambda rows, cols: pl.BlockSpec((rows // steps, cols),
                                           lambda b, p: (b * N_PAIRS + p, 0))
    slabs = [slab(*w.shape) for w in flat]
    kv_rows, kv_cols = w_kv.shape
    kv_out_cols = kv_cols + KV_HEADS * HEAD_DIM
    outs = pl.pallas_call(
        functools.partial(_fox_attn_kernel, tq=tq, tk=tk, seq=seq,
                          n_cast=len(flat), s_ahead=s_ahead, e_ahead=e_ahead),
        grid=(bsz, N_PAIRS),
        in_specs=[feat, tok, feat,
                  pl.BlockSpec((seq, LANES), lambda b, p: (b, 0))]
        + slabs + [slab(kv_rows, kv_cols)],
        out_specs=[tok] + slabs + [slab(kv_rows, kv_out_cols)],
        out_shape=[jax.ShapeDtypeStruct(k.shape, BF16)]
        + [jax.ShapeDtypeStruct(w.shape, BF16) for w in flat]
        + [jax.ShapeDtypeStruct((kv_rows, kv_out_cols), BF16)],
        scratch_shapes=[pltpu.VMEM((s_ahead - e_ahead + 2, 2, tk, tq), F32),
                        pltpu.VMEM((e_ahead + 1, 2, tk, tq), BF16)],
        compiler_params=pltpu.CompilerParams(
            dimension_semantics=("parallel", "parallel")),
        name="fox_attn",
    )(qt, k, vt, caug, *flat, w_kv)
    return (outs[0], [wb.reshape(w.shape) for wb, w in zip(outs[1:-1], weights)],
            outs[-1])


def _post_mlp_kernel(*refs, layer, emit_next, o_by_pair, ff_chunk):
    if emit_next:
        (h_ref, o_ref, wo_ref, gm_ref, wup_ref, wdn_ref,
         ga_ref, gkv_ref, wq_ref, wkv_ref, gq_ref, gk_ref,
         out_ref, qb_ref, kd_ref, vt_ref) = refs
    else:
        h_ref, o_ref, wo_ref, gm_ref, wup_ref, wdn_ref, out_ref = refs
    if o_by_pair:
        o = jnp.concatenate([o_ref[p] for p in range(N_PAIRS)], axis=1)
    else:
        o = o_ref[...]
    h1 = h_ref[...] + jnp.dot(o, wo_ref[...], preferred_element_type=F32)
    m = ((h1 * _rms_scale(h1)) * gm_ref[layer:layer + 1, :]).astype(BF16)
    acc = h1
    for c in range(D_FF // ff_chunk):
        u = jnp.dot(m, wup_ref[:, c * ff_chunk:(c + 1) * ff_chunk],
                    preferred_element_type=F32)
        u = jnp.square(jnp.maximum(u, 0.0)).astype(BF16)
        acc = acc + jnp.dot(u, wdn_ref[c * ff_chunk:(c + 1) * ff_chunk, :],
                            preferred_element_type=F32)
    out_ref[...] = acc
    if emit_next:
        hn = acc * _rms_scale(acc)
        a = (hn * ga_ref[layer + 1:layer + 2, :]).astype(BF16)
        for c in range(D_MODEL // MXU_N):
            y = jnp.dot(a, wq_ref[:, c * MXU_N:(c + 1) * MXU_N],
                        preferred_element_type=F32)
            for half in range(MXU_N // LANES):
                yy = y[:, half * LANES:(half + 1) * LANES]
                dst = pl.ds(c * MXU_N + half * LANES, LANES)
                qb_ref[:, dst] = (_head_pair_norm(yy, gq_ref[...])
                                  * (QK_SCALE * LOG2_E)).astype(BF16)
        kvn = (hn * gkv_ref[...]).astype(BF16)
        kv = jnp.dot(kvn, wkv_ref[...], preferred_element_type=F32)
        for j in range(KV_HEADS):
            sl = slice(j * LANES, (j + 1) * LANES)
            kd_ref[:, sl] = _head_pair_norm(kv[:, sl], gk_ref[...]).astype(BF16)
        vt_ref[...] = kv[:, KV_HEADS * LANES:].T.astype(BF16)


def _post_mlp(h, o, wo, gm, wup, wdn, layer, nxt, tm, ff_chunk=MLP_FF_CHUNK):
    t = h.shape[0]
    tok = lambda w: pl.BlockSpec((tm, w), lambda i: (i, 0))
    o_by_pair = o.ndim == 3
    o_spec = (pl.BlockSpec((N_PAIRS, tm, LANES), lambda i: (0, i, 0))
              if o_by_pair else tok(D_MODEL))
    in_specs = [tok(D_MODEL), o_spec, _layer((D_MODEL, D_MODEL), 0),
                _resident(gm.shape), _layer((D_MODEL, D_FF), layer),
                _layer((D_FF, D_MODEL), layer)]
    out_specs = [tok(D_MODEL)]
    out_shape = [jax.ShapeDtypeStruct((t, D_MODEL), F32)]
    args = [h, o, wo, gm, wup, wdn]
    if nxt is not None:
        kvw = KV_HEADS * LANES + KV_HEADS * HEAD_DIM
        in_specs += [_resident(nxt[0].shape), _resident((1, D_MODEL)),
                     _layer((D_MODEL, D_MODEL), 0), _resident((D_MODEL, kvw)),
                     _resident((1, HEAD_DIM)), _resident((1, HEAD_DIM))]
        out_specs += [tok(D_MODEL), tok(KV_HEADS * LANES),
                      pl.BlockSpec((KV_HEADS * HEAD_DIM, tm), lambda i: (0, i))]
        out_shape += [jax.ShapeDtypeStruct((t, D_MODEL), BF16),
                      jax.ShapeDtypeStruct((t, KV_HEADS * LANES), BF16),
                      jax.ShapeDtypeStruct((KV_HEADS * HEAD_DIM, t), BF16)]
        args += list(nxt)
    return pl.pallas_call(
        functools.partial(_post_mlp_kernel, layer=layer, emit_next=nxt is not None,
                          o_by_pair=o_by_pair, ff_chunk=ff_chunk),
        grid=(t // tm,),
        in_specs=in_specs, out_specs=out_specs, out_shape=out_shape,
        compiler_params=pltpu.CompilerParams(
            dimension_semantics=("parallel",), vmem_limit_bytes=VMEM_LIMIT),
        name="post_mlp_next" if nxt is not None else "post_mlp",
    )(*args)


def _t5_causal_bucket(dist):
    n = np.maximum(dist, 0)
    max_exact = N_BUCKETS // 2
    large = max_exact + (np.log(np.maximum(n, 1) / max_exact)
                         / np.log(REL_MAX_DIST / max_exact)
                         * (N_BUCKETS - max_exact)).astype(np.int32)
    large = np.minimum(large, N_BUCKETS - 1)
    return np.where(n < max_exact, n, large).astype(np.int32)


def _bucket_ranges():
    buckets = _t5_causal_bucket(np.arange(WINDOW))
    assert np.all(np.diff(buckets) >= 0)
    out = []
    for k in np.unique(buckets):
        idx = np.nonzero(buckets == k)[0]
        out.append((int(k), int(idx[0]), int(idx[-1]) + 1))
    return out


def _swa_attn_kernel(sink_ref, rb_ref, q_ref, kp_ref, kc_ref, vtp_ref, vtc_ref,
                     o_ref, bias_sc, st_sc, e_sc, *, n_blocks):
    n = pl.program_id(1)
    band = 2 * WINDOW

    @pl.when((pl.program_id(0) == 0) & (n == 0))
    def _():
        krow = lax.broadcasted_iota(jnp.int32, (band, WINDOW), 0)
        qcol = lax.broadcasted_iota(jnp.int32, (band, WINDOW), 1)
        dist = qcol + WINDOW - krow
        for h in range(N_HEADS):
            tile = jnp.full((band, WINDOW), NEG_INF, F32)
            for k, lo, hi in _bucket_ranges():
                tile = jnp.where((dist >= lo) & (dist < hi),
                                 rb_ref[h, k] * LOG2_E, tile)
            cols = slice((h % 2) * WINDOW, (h % 2 + 1) * WINDOW)
            bias_sc[0, h // 2, :, cols] = tile
            bias_sc[1, h // 2, :, cols] = jnp.where(krow < WINDOW, NEG_INF, tile)

    first = (n == 0).astype(jnp.int32)
    keys = jnp.concatenate([kp_ref[...], kc_ref[...]], axis=0)
    vts = jnp.concatenate([vtp_ref[...], vtc_ref[...]], axis=1)
    lane = lax.broadcasted_iota(jnp.int32, (1, LANES), 1)
    col = lax.broadcasted_iota(jnp.int32, (1, band), 1)
    pairs_per_kv = N_PAIRS // KV_HEADS
    items = [(j, p) for j in range(n_blocks) for p in range(N_PAIRS)]

    def scores(item, slot):
        j, p = item
        g = p // pairs_per_kv
        q2 = q_ref[j * WINDOW:(j + 1) * WINDOW, p * LANES:(p + 1) * LANES]
        zero = jnp.zeros_like(q2)
        qs = jnp.concatenate([jnp.where(lane < HEAD_DIM, q2, zero),
                              jnp.where(lane >= HEAD_DIM, q2, zero)], axis=0)
        st_sc[slot] = lax.dot_general(
            keys[j * WINDOW:j * WINDOW + band, g * LANES:(g + 1) * LANES], qs,
            (((1,), (1,)), ((), ())), preferred_element_type=F32)

    def softmax(item, st_slot, slot):
        j, p = item
        st = st_sc[st_slot] + bias_sc[first if j == 0 else 0, p]
        sink = jnp.where(col < WINDOW, sink_ref[2 * p], sink_ref[2 * p + 1]) * LOG2_E
        m = jnp.maximum(_reduce_rows(st, jnp.maximum), sink)
        e_sc[slot] = jnp.exp2(st - m).astype(BF16)
        return jnp.exp2(sink - m)

    def values(item, slot, sink_term):
        j, p = item
        g = p // pairs_per_kv
        vt1 = jnp.concatenate(
            [vts[g * HEAD_DIM:(g + 1) * HEAD_DIM, j * WINDOW:j * WINDOW + band],
             jnp.ones((SUM_ROWS, band), BF16)], axis=0)
        acc = jnp.dot(vt1, e_sc[slot], preferred_element_type=F32)
        ot = acc[:HEAD_DIM] / (acc[HEAD_DIM:HEAD_DIM + 1] + sink_term)
        o2 = jnp.concatenate([ot[:, :WINDOW], ot[:, WINDOW:]], axis=0).T
        o_ref[j * WINDOW:(j + 1) * WINDOW, p * LANES:(p + 1) * LANES] = o2.astype(BF16)

    for t in range(3):
        scores(items[t], t)
    sink_term = softmax(items[0], 0, 0)
    for t, item in enumerate(items):
        if t + 3 < len(items):
            scores(items[t + 3], t % 3)
        values(item, t % 2, sink_term)
        if t + 1 < len(items):
            sink_term = softmax(items[t + 1], (t + 1) % 3, (t + 1) % 2)


def _swa_attn(sinks, rel_bias, q, kd, vt, bsz, seq, n_blocks):
    nblk = seq // WINDOW
    steps = nblk // n_blocks
    kvw = KV_HEADS * LANES
    vtw = KV_HEADS * HEAD_DIM
    tq = n_blocks * WINDOW
    prev_block = lambda b, n: b * nblk + jnp.maximum(n * n_blocks - 1, 0)
    smem = pl.BlockSpec(memory_space=pltpu.SMEM)
    return pl.pallas_call(
        functools.partial(_swa_attn_kernel, n_blocks=n_blocks),
        grid=(bsz, steps),
        in_specs=[
            smem, smem,
            pl.BlockSpec((tq, D_MODEL), lambda b, n: (b * steps + n, 0)),
            pl.BlockSpec((WINDOW, kvw), lambda b, n: (prev_block(b, n), 0)),
            pl.BlockSpec((tq, kvw), lambda b, n: (b * steps + n, 0)),
            pl.BlockSpec((vtw, WINDOW), lambda b, n: (0, prev_block(b, n))),
            pl.BlockSpec((vtw, tq), lambda b, n: (0, b * steps + n)),
        ],
        out_specs=pl.BlockSpec((tq, D_MODEL), lambda b, n: (b * steps + n, 0)),
        out_shape=jax.ShapeDtypeStruct((bsz * seq, D_MODEL), BF16),
        scratch_shapes=[pltpu.VMEM((2, N_PAIRS, 2 * WINDOW, 2 * WINDOW), F32),
                        pltpu.VMEM((3, 2 * WINDOW, 2 * WINDOW), F32),
                        pltpu.VMEM((2, 2 * WINDOW, 2 * WINDOW), BF16)],
        compiler_params=pltpu.CompilerParams(
            dimension_semantics=("arbitrary", "arbitrary")),
        name="swa_attn",
    )(sinks, rel_bias, q, kd, kd, vt, vt)


def kernel(x, g_attn, g_mlp, w_in_a, b_f, gq_a, gk_a, w_out_a, g_kv, w_kv, gk_b,
           w_q_b, gq_b, sinks, rel_bias, w_out_b, w_up, w_down):
    bsz, seq, d = x.shape
    t = bsz * seq
    row = lambda g: g.reshape(1, -1).astype(F32)

    x2 = x.reshape(t, d)
    qt, k, vt, f3 = _fox_pre(x2, g_attn.astype(F32), w_in_a[0].T.astype(F32),
                            row(gq_a[0]), row(gk_a[0]), tm=PRE_TOKENS)
    caug = _fox_decay(f3, row(b_f[0]), bsz, seq)
    o, (w_up_b, w_down_b, w_out_a_b, w_out_b_b, w_q_b_b), wkv = _fox_attn(
        qt, k, vt, caug, [w_up, w_down, w_out_a, w_out_b, w_q_b], w_kv.astype(F32),
        bsz, seq, tq=FOX_TQ, tk=FOX_TK)
    nxt = (g_attn.astype(F32), row(g_kv), w_q_b_b, wkv, row(gq_b[0]), row(gk_b))
    h, qb, kd, vt_b = _post_mlp(x2, o, w_out_a_b, g_mlp.astype(F32),
                                w_up_b, w_down_b, 0, nxt, tm=MLP_TOKENS)

    o2 = _swa_attn(sinks[0].astype(F32), rel_bias.T.astype(F32), qb, kd, vt_b,
                   bsz, seq, n_blocks=SWA_BLOCKS)
    (out,) = _post_mlp(h, o2, w_out_b_b, g_mlp.astype(F32),
                       w_up_b, w_down_b, 1, None, tm=MLP_TOKENS)
    return out.reshape(bsz, seq, d)
```

```python
import functools

import numpy as np
import jax
import jax.numpy as jnp
from jax import lax
from jax.experimental import pallas as pl
from jax.experimental.pallas import tpu as pltpu

D_MODEL = 1024
HEAD_DIM = 64
N_HEADS = 16
N_PAIRS = N_HEADS // 2
KV_HEADS = 2
WINDOW = 128
D_FF = 4 * D_MODEL
N_BUCKETS = 32
REL_MAX_DIST = 128
NORM_EPS = 1e-6
LANES = 128
MXU_N = 256
QK_SCALE = HEAD_DIM ** -0.5
LOG2_E = 1.4426950408889634
N_SPLIT = 3
SUM_ROWS = 16
ROW_CHUNK = 32

F32 = jnp.float32
BF16 = jnp.bfloat16
NEG_INF = float("-inf")

PRE_TOKENS = 1024
MLP_TOKENS = 512
MLP_TOKENS_LAST = 1024
MLP_FF_CHUNK = 1024
FOX_TQ, FOX_TK = 256, 512
FOX_SCORES_AHEAD = 2
FOX_PROBS_AHEAD = 1
SWA_BLOCKS = 8
VMEM_LIMIT = 56 * 1024 * 1024


def _rms_scale(x):
    return lax.rsqrt(jnp.mean(x * x, axis=-1, keepdims=True) + NORM_EPS)


def _head_pair_norm(y, g):
    g2 = jnp.concatenate([g, g], axis=1)
    lo = lax.broadcasted_iota(jnp.int32, (1, LANES), 1) < HEAD_DIM
    sq = y * y
    s_lo = jnp.sum(jnp.where(lo, sq, 0.0), axis=-1, keepdims=True)
    s_hi = jnp.sum(jnp.where(lo, 0.0, sq), axis=-1, keepdims=True)
    ms = jnp.where(lo, s_lo, s_hi) * (1.0 / HEAD_DIM)
    return (y * lax.rsqrt(ms + NORM_EPS)) * g2


def _split_bf16(x):
    terms = []
    for _ in range(N_SPLIT):
        t = x.astype(BF16)
        terms.append(t)
        x = x - t.astype(F32)
    return terms


def _reduce_rows(x, op, group=32):
    acc = x[0:group]
    for r in range(group, x.shape[0], group):
        acc = op(acc, x[r:r + group])
    red = jnp.max if op is jnp.maximum else jnp.sum
    return red(acc, axis=0, keepdims=True)


def _resident(shape):
    zeros = (0,) * len(shape)
    return pl.BlockSpec(shape, lambda *_: zeros, pipeline_mode=pl.Buffered(1))


def _layer(shape, layer):
    index = (layer,) + (0,) * len(shape)
    return pl.BlockSpec((None,) + tuple(shape), lambda *_: index,
                        pipeline_mode=pl.Buffered(1))


def _fox_pre_kernel(x_ref, g_ref, w_ref, gq_ref, gk_ref,
                    qt_ref, k_ref, vt_ref, f_ref):
    x = x_ref[...]
    hn = ((x * _rms_scale(x)) * g_ref[0:1, :]).astype(BF16)
    n_chunks = D_MODEL // MXU_N
    for c in range(3 * n_chunks):
        y = lax.dot_general(
            hn, w_ref[c * MXU_N:(c + 1) * MXU_N, :].astype(BF16),
            (((1,), (1,)), ((), ())), preferred_element_type=F32)
        part, cc = divmod(c, n_chunks)
        rows = slice(cc * MXU_N, (cc + 1) * MXU_N)
        if part == 2:
            vt_ref[rows, :] = y.T.astype(BF16)
            continue
        g2 = gq_ref[...] if part == 0 else gk_ref[...]
        yn = jnp.concatenate(
            [_head_pair_norm(y[:, h * LANES:(h + 1) * LANES], g2)
             for h in range(MXU_N // LANES)], axis=1)
        if part == 0:
            qt_ref[rows, :] = (yn * (QK_SCALE * LOG2_E)).T.astype(BF16)
        else:
            for h in range(MXU_N // LANES):
                k_ref[cc * (MXU_N // LANES) + h] = yn[:, h * LANES:(h + 1) * LANES].astype(BF16)
    w_f = w_ref[3 * D_MODEL:3 * D_MODEL + N_HEADS, :]
    w_f3 = jnp.concatenate(
        [w_f] * N_SPLIT + [jnp.zeros((LANES - N_SPLIT * N_HEADS, D_MODEL), F32)],
        axis=0).astype(BF16)
    f_ref[...] = lax.dot_general(hn, w_f3, (((1,), (1,)), ((), ())),
                                 preferred_element_type=F32)


def _fox_pre(x2, g, w_in, gq2, gk2, tm):
    t = x2.shape[0]
    tok = lambda w: pl.BlockSpec((tm, w), lambda i: (i, 0))
    return pl.pallas_call(
        _fox_pre_kernel,
        grid=(t // tm,),
        in_specs=[tok(D_MODEL), _resident(g.shape),
                  _resident(w_in.shape),
                  _resident((1, HEAD_DIM)), _resident((1, HEAD_DIM))],
        out_specs=[pl.BlockSpec((D_MODEL, tm), lambda i: (0, i)),
                   pl.BlockSpec((N_PAIRS, tm, LANES), lambda i: (0, i, 0)),
                   pl.BlockSpec((D_MODEL, tm), lambda i: (0, i)), tok(LANES)],
        out_shape=[jax.ShapeDtypeStruct((D_MODEL, t), BF16),
                   jax.ShapeDtypeStruct((N_PAIRS, t, LANES), BF16),
                   jax.ShapeDtypeStruct((D_MODEL, t), BF16),
                   jax.ShapeDtypeStruct((t, LANES), F32)],
        compiler_params=pltpu.CompilerParams(
            dimension_semantics=("parallel",), vmem_limit_bytes=VMEM_LIMIT),
        name="fox_pre",
    )(x2, g, w_in, gq2, gk2)


def _fox_decay_kernel(f_ref, b_ref, c_ref, *, blk):
    seq = f_ref.shape[0]
    lane = lax.broadcasted_iota(jnp.int32, (1, LANES), 1)
    r = lax.broadcasted_iota(jnp.int32, (blk, blk), 0)
    c = lax.broadcasted_iota(jnp.int32, (blk, blk), 1)
    lower = jnp.where(r >= c, 1.0, 0.0).astype(BF16)
    bias = jnp.concatenate(
        [b_ref[...]] * N_SPLIT + [jnp.zeros((1, LANES - N_SPLIT * N_HEADS), F32)], axis=1)
    carry = jnp.zeros((1, LANES), F32)
    for b in range(seq // blk):
        rows = slice(b * blk, (b + 1) * blk)
        x = f_ref[rows, :] + bias
        log_f = -(jnp.maximum(-x, 0.0) + jnp.log1p(jnp.exp(-jnp.abs(x))))
        cb = carry
        for term in _split_bf16(log_f):
            cb = cb + jnp.dot(lower, term, preferred_element_type=F32)
        carry = cb[blk - 1:blk, :]
        out = jnp.zeros((blk, LANES), BF16)
        for t, term in reversed(list(enumerate(_split_bf16(cb * LOG2_E)))):
            out = jnp.where(lane < (t + 1) * N_HEADS, term, out)
        c_ref[rows, :] = out


def _fox_decay(f3, b3, bsz, seq):
    spec = pl.BlockSpec((seq, LANES), lambda b: (b, 0))
    return pl.pallas_call(
        functools.partial(_fox_decay_kernel, blk=MXU_N),
        grid=(bsz,),
        in_specs=[spec, _resident((1, N_HEADS))],
        out_specs=spec,
        out_shape=jax.ShapeDtypeStruct(f3.shape, BF16),
        compiler_params=pltpu.CompilerParams(dimension_semantics=("parallel",)),
        name="fox_decay",
    )(f3, b3)


def _fox_attn_kernel(qt_ref, k_ref, vt_ref, c_ref, *rest, tq, tk, seq, n_cast,
                     s_ahead, e_ahead):
    w_refs, wkv_ref, o_ref = rest[:n_cast], rest[n_cast], rest[n_cast + 1]
    wb_refs, wkvb_ref = rest[n_cast + 2:2 * n_cast + 2], rest[2 * n_cast + 2]
    st_sc, pt_sc = rest[2 * n_cast + 3:]
    for w_ref, wb_ref in zip(w_refs, wb_refs):
        wb_ref[...] = w_ref[...].astype(BF16)
    wkv = wkv_ref[...]
    k_heads = [wkv[:, j * HEAD_DIM:(j + 1) * HEAD_DIM] for j in range(KV_HEADS)]
    wkvb_ref[...] = jnp.concatenate(
        [kh for kh in k_heads for _ in range(2)] + [wkv[:, KV_HEADS * HEAD_DIM:]],
        axis=1).astype(BF16)
    p = pl.program_id(1)
    feat = lax.broadcasted_iota(jnp.int32, (LANES, 1), 0)
    own = [feat < HEAD_DIM, feat >= HEAD_DIM]
    neg = [jnp.broadcast_to(
        jnp.where((feat < N_SPLIT * N_HEADS)
                  & ((feat & (N_HEADS - 1)) == 2 * p + hh), -1.0, 0.0).astype(BF16),
        (LANES, tq)) for hh in range(2)]

    items = []
    for i in range(seq // tq):
        n_keys = (i + 1) * tq
        starts = list(range(0, n_keys, tk))
        for ks in starts:
            items.append((i, ks, min(tk, n_keys - ks), ks == starts[-1]))

    def query_blocks(i):
        q2 = qt_ref[:, i * tq:(i + 1) * tq]
        return [jnp.concatenate(
            [jnp.where(own[hh], q2, jnp.zeros_like(q2)), neg[hh]], axis=0)
            for hh in range(2)]

    def scores(item, qa, slot, col_max):
        i, ks, width, is_last = item
        ka = jnp.concatenate([k_ref[ks:ks + width, :], c_ref[ks:ks + width, :]],
                             axis=1)
        if is_last:
            krow = lax.broadcasted_iota(jnp.int32, (width, tq), 0)
            qcol = lax.broadcasted_iota(jnp.int32, (width, tq), 1)
            visible = krow <= qcol + (i * tq - ks)

        def one(hh):
            st = jnp.dot(ka, qa[hh], preferred_element_type=F32)
            if is_last:
                st = jnp.where(visible, st, NEG_INF)
            st_sc[slot, hh, 0:width, :] = st
            col_max[hh] = _reduce_rows(st, jnp.maximum)
        return [functools.partial(one, hh) for hh in range(2)]

    def probabilities(item, t, col_max, m_old, res):
        width = item[2]
        chunks = list(range(0, width, ROW_CHUNK))
        pieces = []
        for hh in range(2):
            src = st_sc.at[t % n_st, hh]

            def new_max(hh=hh):
                m_new = jnp.maximum(m_old[hh], col_max[hh])
                res[hh] = (m_new, jnp.exp2(m_old[hh] - m_new))

            def exps(rows, hh=hh, src=src):
                for r in rows:
                    pt_sc[t % n_pt, hh, r:r + ROW_CHUNK, :] = jnp.exp2(
                        src[r:r + ROW_CHUNK, :] - res[hh][0]).astype(BF16)

            half = len(chunks) // 2
            pieces += [new_max, functools.partial(exps, chunks[:half]),
                       functools.partial(exps, chunks[half:])]
        return pieces

    def values(item, t, alpha, acc_old, new):
        _, ks, width, _ = item

        def one(hh):
            vt = jnp.concatenate(
                [vt_ref[hh * HEAD_DIM:(hh + 1) * HEAD_DIM, ks:ks + width],
                 jnp.ones((SUM_ROWS, width), BF16)], axis=0)
            new[hh] = alpha[hh] * acc_old[hh] + jnp.dot(
                vt, pt_sc[t % n_pt, hh, 0:width, :],
                preferred_element_type=F32)
        return [functools.partial(one, hh) for hh in range(2)]

    n_items = len(items)
    n_st, n_pt = st_sc.shape[0], pt_sc.shape[0]
    assert n_st > s_ahead - e_ahead and n_pt > e_ahead
    m_fresh = [jnp.full((1, tq), NEG_INF, F32)] * 2
    acc_fresh = [jnp.zeros((HEAD_DIM + SUM_ROWS, tq), F32)] * 2
    alpha, col_max = {}, {}
    qa_tile, qa = -1, None
    m_run, acc_run = m_fresh, acc_fresh
    for t in range(-s_ahead, n_items):
        matmuls, vector = [], []
        u, e = t + s_ahead, t + e_ahead
        if u < n_items:
            if items[u][0] != qa_tile:
                qa_tile, qa = items[u][0], query_blocks(items[u][0])
            col_max[u] = [None, None]
            matmuls += scores(items[u], qa, u % n_st, col_max[u])
        acc_new = [None, None]
        if t >= 0:
            matmuls += values(items[t], t, alpha.pop(t), acc_run, acc_new)
        res = [None, None]
        if 0 <= e < n_items:
            vector = probabilities(items[e], e, col_max.pop(e), m_run, res)
        per = -(-len(vector) // max(len(matmuls), 1))
        for n, mm in enumerate(matmuls):
            mm()
            for piece in vector[n * per:(n + 1) * per]:
                piece()
        for piece in vector[len(matmuls) * per:]:
            piece()
        if t >= 0:
            acc_run = acc_new
            if items[t][3]:
                i = items[t][0]
                ot = jnp.concatenate(
                    [acc[:HEAD_DIM] / acc[HEAD_DIM:HEAD_DIM + 1] for acc in acc_run],
                    axis=0)
                o_ref[i * tq:(i + 1) * tq, :] = ot.T.astype(BF16)
                acc_run = acc_fresh
        if 0 <= e < n_items:
            m_run = m_fresh if items[e][3] else [r[0] for r in res]
            alpha[e] = [r[1] for r in res]


def _fox_attn(qt, k, vt, caug, weights, w_kv, bsz, seq, tq, tk,
              s_ahead=FOX_SCORES_AHEAD, e_ahead=FOX_PROBS_AHEAD):
    tok = pl.BlockSpec((None, seq, LANES), lambda b, p: (p, b, 0))
    feat = pl.BlockSpec((LANES, seq), lambda b, p: (p, b))
    steps = bsz * N_PAIRS
    flat = [w.reshape(-1, w.shape[-1]) for w in weights]
    slab = lambda rows, cols: pl.BlockSpec((rows // steps, cols),
                                           lambda b, p: (b * N_PAIRS + p, 0))
    slabs = [slab(*w.shape) for w in flat]
    kv_rows, kv_cols = w_kv.shape
    kv_out_cols = kv_cols + KV_HEADS * HEAD_DIM
    outs = pl.pallas_call(
        functools.partial(_fox_attn_kernel, tq=tq, tk=tk, seq=seq,
                          n_cast=len(flat), s_ahead=s_ahead, e_ahead=e_ahead),
        grid=(bsz, N_PAIRS),
        in_specs=[feat, tok, feat,
                  pl.BlockSpec((seq, LANES), lambda b, p: (b, 0))]
        + slabs + [slab(kv_rows, kv_cols)],
        out_specs=[tok] + slabs + [slab(kv_rows, kv_out_cols)],
        out_shape=[jax.ShapeDtypeStruct(k.shape, BF16)]
        + [jax.ShapeDtypeStruct(w.shape, BF16) for w in flat]
        + [jax.ShapeDtypeStruct((kv_rows, kv_out_cols), BF16)],
        scratch_shapes=[pltpu.VMEM((s_ahead - e_ahead + 2, 2, tk, tq), F32),
                        pltpu.VMEM((e_ahead + 1, 2, tk, tq), BF16)],
        compiler_params=pltpu.CompilerParams(
            dimension_semantics=("parallel", "parallel")),
        name="fox_attn",
    )(qt, k, vt, caug, *flat, w_kv)
    return (outs[0], [wb.reshape(w.shape) for wb, w in zip(outs[1:-1], weights)],
            outs[-1])


def _post_mlp_kernel(*refs, layer, emit_next, o_by_pair, ff_chunk):
    if emit_next:
        (h_ref, o_ref, wo_ref, gm_ref, wup_ref, wdn_ref,
         ga_ref, gkv_ref, wq_ref, wkv_ref, gq_ref, gk_ref,
         out_ref, qb_ref, kd_ref, vt_ref) = refs
    else:
        h_ref, o_ref, wo_ref, gm_ref, wup_ref, wdn_ref, out_ref = refs
    if o_by_pair:
        o = jnp.concatenate([o_ref[p] for p in range(N_PAIRS)], axis=1)
    else:
        o = o_ref[...]
    h1 = h_ref[...] + jnp.dot(o, wo_ref[...], preferred_element_type=F32)
    m = ((h1 * _rms_scale(h1)) * gm_ref[layer:layer + 1, :]).astype(BF16)
    acc = h1
    for c in range(D_FF // ff_chunk):
        u = jnp.dot(m, wup_ref[:, c * ff_chunk:(c + 1) * ff_chunk],
                    preferred_element_type=F32)
        u = jnp.square(jnp.maximum(u, 0.0)).astype(BF16)
        acc = acc + jnp.dot(u, wdn_ref[c * ff_chunk:(c + 1) * ff_chunk, :],
                            preferred_element_type=F32)
    out_ref[...] = acc
    if emit_next:
        hn = acc * _rms_scale(acc)
        a = (hn * ga_ref[layer + 1:layer + 2, :]).astype(BF16)
        for c in range(D_MODEL // MXU_N):
            y = jnp.dot(a, wq_ref[:, c * MXU_N:(c + 1) * MXU_N],
                        preferred_element_type=F32)
            for half in range(MXU_N // LANES):
                yy = y[:, half * LANES:(half + 1) * LANES]
                dst = pl.ds(c * MXU_N + half * LANES, LANES)
                qb_ref[:, dst] = (_head_pair_norm(yy, gq_ref[...])
                                  * (QK_SCALE * LOG2_E)).astype(BF16)
        kvn = (hn * gkv_ref[...]).astype(BF16)
        kv = jnp.dot(kvn, wkv_ref[...], preferred_element_type=F32)
        for j in range(KV_HEADS):
            sl = slice(j * LANES, (j + 1) * LANES)
            kd_ref[:, sl] = _head_pair_norm(kv[:, sl], gk_ref[...]).astype(BF16)
        vt_ref[...] = kv[:, KV_HEADS * LANES:].T.astype(BF16)


def _post_mlp(h, o, wo, gm, wup, wdn, layer, nxt, tm, ff_chunk=MLP_FF_CHUNK):
    t = h.shape[0]
    tok = lambda w: pl.BlockSpec((tm, w), lambda i: (i, 0))
    o_by_pair = o.ndim == 3
    o_spec = (pl.BlockSpec((N_PAIRS, tm, LANES), lambda i: (0, i, 0))
              if o_by_pair else tok(D_MODEL))
    in_specs = [tok(D_MODEL), o_spec, _layer((D_MODEL, D_MODEL), 0),
                _resident(gm.shape), _layer((D_MODEL, D_FF), layer),
                _layer((D_FF, D_MODEL), layer)]
    out_specs = [tok(D_MODEL)]
    out_shape = [jax.ShapeDtypeStruct((t, D_MODEL), F32)]
    args = [h, o, wo, gm, wup, wdn]
    if nxt is not None:
        kvw = KV_HEADS * LANES + KV_HEADS * HEAD_DIM
        in_specs += [_resident(nxt[0].shape), _resident((1, D_MODEL)),
                     _layer((D_MODEL, D_MODEL), 0), _resident((D_MODEL, kvw)),
                     _resident((1, HEAD_DIM)), _resident((1, HEAD_DIM))]
        out_specs += [tok(D_MODEL), tok(KV_HEADS * LANES),
                      pl.BlockSpec((KV_HEADS * HEAD_DIM, tm), lambda i: (0, i))]
        out_shape += [jax.ShapeDtypeStruct((t, D_MODEL), BF16),
                      jax.ShapeDtypeStruct((t, KV_HEADS * LANES), BF16),
                      jax.ShapeDtypeStruct((KV_HEADS * HEAD_DIM, t), BF16)]
        args += list(nxt)
    return pl.pallas_call(
        functools.partial(_post_mlp_kernel, layer=layer, emit_next=nxt is not None,
                          o_by_pair=o_by_pair, ff_chunk=ff_chunk),
        grid=(t // tm,),
        in_specs=in_specs, out_specs=out_specs, out_shape=out_shape,
        compiler_params=pltpu.CompilerParams(
            dimension_semantics=("parallel",), vmem_limit_bytes=VMEM_LIMIT),
        name="post_mlp_next" if nxt is not None else "post_mlp",
    )(*args)


def _t5_causal_bucket(dist):
    n = np.maximum(dist, 0)
    max_exact = N_BUCKETS // 2
    large = max_exact + (np.log(np.maximum(n, 1) / max_exact)
                         / np.log(REL_MAX_DIST / max_exact)
                         * (N_BUCKETS - max_exact)).astype(np.int32)
    large = np.minimum(large, N_BUCKETS - 1)
    return np.where(n < max_exact, n, large).astype(np.int32)


def _bucket_ranges():
    buckets = _t5_causal_bucket(np.arange(WINDOW))
    assert np.all(np.diff(buckets) >= 0)
    out = []
    for k in np.unique(buckets):
        idx = np.nonzero(buckets == k)[0]
        out.append((int(k), int(idx[0]), int(idx[-1]) + 1))
    return out


def _swa_attn_kernel(sink_ref, rb_ref, q_ref, kp_ref, kc_ref, vtp_ref, vtc_ref,
                     o_ref, bias_sc, st_sc, e_sc, *, n_blocks):
    n = pl.program_id(1)
    band = 2 * WINDOW

    @pl.when((pl.program_id(0) == 0) & (n == 0))
    def _():
        krow = lax.broadcasted_iota(jnp.int32, (band, WINDOW), 0)
        qcol = lax.broadcasted_iota(jnp.int32, (band, WINDOW), 1)
        dist = qcol + WINDOW - krow
        for h in range(N_HEADS):
            tile = jnp.full((band, WINDOW), NEG_INF, F32)
            for k, lo, hi in _bucket_ranges():
                tile = jnp.where((dist >= lo) & (dist < hi),
                                 rb_ref[h, k] * LOG2_E, tile)
            cols = slice((h % 2) * WINDOW, (h % 2 + 1) * WINDOW)
            bias_sc[0, h // 2, :, cols] = tile
            bias_sc[1, h // 2, :, cols] = jnp.where(krow < WINDOW, NEG_INF, tile)

    first = (n == 0).astype(jnp.int32)
    keys = jnp.concatenate([kp_ref[...], kc_ref[...]], axis=0)
    vts = jnp.concatenate([vtp_ref[...], vtc_ref[...]], axis=1)
    lane = lax.broadcasted_iota(jnp.int32, (1, LANES), 1)
    col = lax.broadcasted_iota(jnp.int32, (1, band), 1)
    pairs_per_kv = N_PAIRS // KV_HEADS
    items = [(j, p) for j in range(n_blocks) for p in range(N_PAIRS)]

    def scores(item, slot):
        j, p = item
        g = p // pairs_per_kv
        q2 = q_ref[j * WINDOW:(j + 1) * WINDOW, p * LANES:(p + 1) * LANES]
        zero = jnp.zeros_like(q2)
        qs = jnp.concatenate([jnp.where(lane < HEAD_DIM, q2, zero),
                              jnp.where(lane >= HEAD_DIM, q2, zero)], axis=0)
        st_sc[slot] = lax.dot_general(
            keys[j * WINDOW:j * WINDOW + band, g * LANES:(g + 1) * LANES], qs,
            (((1,), (1,)), ((), ())), preferred_element_type=F32)

    def softmax(item, st_slot, slot):
        j, p = item
        st = st_sc[st_slot] + bias_sc[first if j == 0 else 0, p]
        sink = jnp.where(col < WINDOW, sink_ref[2 * p], sink_ref[2 * p + 1]) * LOG2_E
        m = jnp.maximum(_reduce_rows(st, jnp.maximum), sink)
        e_sc[slot] = jnp.exp2(st - m).astype(BF16)
        return jnp.exp2(sink - m)

    def values(item, slot, sink_term):
        j, p = item
        g = p // pairs_per_kv
        vt1 = jnp.concatenate(
            [vts[g * HEAD_DIM:(g + 1) * HEAD_DIM, j * WINDOW:j * WINDOW + band],
             jnp.ones((SUM_ROWS, band), BF16)], axis=0)
        acc = jnp.dot(vt1, e_sc[slot], preferred_element_type=F32)
        ot = acc[:HEAD_DIM] / (acc[HEAD_DIM:HEAD_DIM + 1] + sink_term)
        o2 = jnp.concatenate([ot[:, :WINDOW], ot[:, WINDOW:]], axis=0).T
        o_ref[j * WINDOW:(j + 1) * WINDOW, p * LANES:(p + 1) * LANES] = o2.astype(BF16)

    for t in range(3):
        scores(items[t], t)
    sink_term = softmax(items[0], 0, 0)
    for t, item in enumerate(items):
        if t + 3 < len(items):
            scores(items[t + 3], t % 3)
        values(item, t % 2, sink_term)
        if t + 1 < len(items):
            sink_term = softmax(items[t + 1], (t + 1) % 3, (t + 1) % 2)


def _swa_attn(sinks, rel_bias, q, kd, vt, bsz, seq, n_blocks):
    nblk = seq // WINDOW
    steps = nblk // n_blocks
    kvw = KV_HEADS * LANES
    vtw = KV_HEADS * HEAD_DIM
    tq = n_blocks * WINDOW
    prev_block = lambda b, n: b * nblk + jnp.maximum(n * n_blocks - 1, 0)
    smem = pl.BlockSpec(memory_space=pltpu.SMEM)
    return pl.pallas_call(
        functools.partial(_swa_attn_kernel, n_blocks=n_blocks),
        grid=(bsz, steps),
        in_specs=[
            smem, smem,
            pl.BlockSpec((tq, D_MODEL), lambda b, n: (b * steps + n, 0)),
            pl.BlockSpec((WINDOW, kvw), lambda b, n: (prev_block(b, n), 0)),
            pl.BlockSpec((tq, kvw), lambda b, n: (b * steps + n, 0)),
            pl.BlockSpec((vtw, WINDOW), lambda b, n: (0, prev_block(b, n))),
            pl.BlockSpec((vtw, tq), lambda b, n: (0, b * steps + n)),
        ],
        out_specs=pl.BlockSpec((tq, D_MODEL), lambda b, n: (b * steps + n, 0)),
        out_shape=jax.ShapeDtypeStruct((bsz * seq, D_MODEL), BF16),
        scratch_shapes=[pltpu.VMEM((2, N_PAIRS, 2 * WINDOW, 2 * WINDOW), F32),
                        pltpu.VMEM((3, 2 * WINDOW, 2 * WINDOW), F32),
                        pltpu.VMEM((2, 2 * WINDOW, 2 * WINDOW), BF16)],
        compiler_params=pltpu.CompilerParams(
            dimension_semantics=("arbitrary", "arbitrary")),
        name="swa_attn",
    )(sinks, rel_bias, q, kd, kd, vt, vt)


def kernel(x, g_attn, g_mlp, w_in_a, b_f, gq_a, gk_a, w_out_a, g_kv, w_kv, gk_b,
           w_q_b, gq_b, sinks, rel_bias, w_out_b, w_up, w_down):
    bsz, seq, d = x.shape
    t = bsz * seq
    row = lambda g: g.reshape(1, -1).astype(F32)

    x2 = x.reshape(t, d)
    qt, k, vt, f3 = _fox_pre(x2, g_attn.astype(F32), w_in_a[0].T.astype(F32),
                            row(gq_a[0]), row(gk_a[0]), tm=PRE_TOKENS)
    caug = _fox_decay(f3, row(b_f[0]), bsz, seq)
    o, (w_up_b, w_down_b, w_out_a_b, w_out_b_b, w_q_b_b), wkv = _fox_attn(
        qt, k, vt, caug, [w_up, w_down, w_out_a, w_out_b, w_q_b], w_kv.astype(F32),
        bsz, seq, tq=FOX_TQ, tk=FOX_TK)
    nxt = (g_attn.astype(F32), row(g_kv), w_q_b_b, wkv, row(gq_b[0]), row(gk_b))
    h, qb, kd, vt_b = _post_mlp(x2, o, w_out_a_b, g_mlp.astype(F32),
                                w_up_b, w_down_b, 0, nxt, tm=MLP_TOKENS)

    o2 = _swa_attn(sinks[0].astype(F32), rel_bias.T.astype(F32), qb, kd, vt_b,
                   bsz, seq, n_blocks=SWA_BLOCKS)
    (out,) = _post_mlp(h, o2, w_out_b_b, g_mlp.astype(F32),
                       w_up_b, w_down_b, 1, None, tm=MLP_TOKENS_LAST)
    return out.reshape(bsz, seq, d)
```

```python
import functools

import numpy as np
import jax
import jax.numpy as jnp
from jax import lax
from jax.experimental import pallas as pl
from jax.experimental.pallas import tpu as pltpu

D_MODEL = 1024
HEAD_DIM = 64
N_HEADS = 16
N_PAIRS = N_HEADS // 2
KV_HEADS = 2
WINDOW = 128
D_FF = 4 * D_MODEL
N_BUCKETS = 32
REL_MAX_DIST = 128
NORM_EPS = 1e-6
LANES = 128
MXU_N = 256
QK_SCALE = HEAD_DIM ** -0.5
LOG2_E = 1.4426950408889634
N_SPLIT = 3
SUM_ROWS = 16
ROW_CHUNK = 32

F32 = jnp.float32
BF16 = jnp.bfloat16
NEG_INF = float("-inf")

PRE_TOKENS = 1024
MLP_TOKENS = 512
MLP_TOKENS_LAST = 1024
MLP_FF_CHUNK = 1024
FOX_TQ, FOX_TK = 256, 512
FOX_SCORES_AHEAD = 2
FOX_PROBS_AHEAD = 1
SWA_BLOCKS = 8
VMEM_LIMIT = 56 * 1024 * 1024


def _rms_scale(x):
    return lax.rsqrt(jnp.mean(x * x, axis=-1, keepdims=True) + NORM_EPS)


def _head_pair_norm(y, g):
    g2 = jnp.concatenate([g, g], axis=1)
    lo = lax.broadcasted_iota(jnp.int32, (1, LANES), 1) < HEAD_DIM
    sq = y * y
    s_lo = jnp.sum(jnp.where(lo, sq, 0.0), axis=-1, keepdims=True)
    s_hi = jnp.sum(jnp.where(lo, 0.0, sq), axis=-1, keepdims=True)
    ms = jnp.where(lo, s_lo, s_hi) * (1.0 / HEAD_DIM)
    return (y * lax.rsqrt(ms + NORM_EPS)) * g2


def _split_bf16(x):
    terms = []
    for _ in range(N_SPLIT):
        t = x.astype(BF16)
        terms.append(t)
        x = x - t.astype(F32)
    return terms


def _reduce_rows(x, op, group=32):
    acc = x[0:group]
    for r in range(group, x.shape[0], group):
        acc = op(acc, x[r:r + group])
    red = jnp.max if op is jnp.maximum else jnp.sum
    return red(acc, axis=0, keepdims=True)


def _resident(shape):
    zeros = (0,) * len(shape)
    return pl.BlockSpec(shape, lambda *_: zeros, pipeline_mode=pl.Buffered(1))


def _layer(shape, layer):
    index = (layer,) + (0,) * len(shape)
    return pl.BlockSpec((None,) + tuple(shape), lambda *_: index,
                        pipeline_mode=pl.Buffered(1))


def _fox_pre_kernel(x_ref, g_ref, w_ref, gq_ref, gk_ref,
                    qt_ref, k_ref, vt_ref, f_ref):
    x = x_ref[...]
    hn = ((x * _rms_scale(x)) * g_ref[0:1, :]).astype(BF16)
    n_chunks = D_MODEL // MXU_N
    for c in range(3 * n_chunks):
        y = lax.dot_general(
            hn, w_ref[c * MXU_N:(c + 1) * MXU_N, :].astype(BF16),
            (((1,), (1,)), ((), ())), preferred_element_type=F32)
        part, cc = divmod(c, n_chunks)
        rows = slice(cc * MXU_N, (cc + 1) * MXU_N)
        if part == 2:
            vt_ref[rows, :] = y.T.astype(BF16)
            continue
        g2 = gq_ref[...] if part == 0 else gk_ref[...]
        yn = jnp.concatenate(
            [_head_pair_norm(y[:, h * LANES:(h + 1) * LANES], g2)
             for h in range(MXU_N // LANES)], axis=1)
        if part == 0:
            qt_ref[rows, :] = (yn * (QK_SCALE * LOG2_E)).T.astype(BF16)
        else:
            for h in range(MXU_N // LANES):
                k_ref[cc * (MXU_N // LANES) + h] = yn[:, h * LANES:(h + 1) * LANES].astype(BF16)
    w_f = w_ref[3 * D_MODEL:3 * D_MODEL + N_HEADS, :]
    w_f3 = jnp.concatenate(
        [w_f] * N_SPLIT + [jnp.zeros((LANES - N_SPLIT * N_HEADS, D_MODEL), F32)],
        axis=0).astype(BF16)
    f_ref[...] = lax.dot_general(hn, w_f3, (((1,), (1,)), ((), ())),
                                 preferred_element_type=F32)


def _fox_pre(x2, g, w_in, gq2, gk2, tm):
    t = x2.shape[0]
    tok = lambda w: pl.BlockSpec((tm, w), lambda i: (i, 0))
    return pl.pallas_call(
        _fox_pre_kernel,
        grid=(t // tm,),
        in_specs=[tok(D_MODEL), _resident(g.shape),
                  _resident(w_in.shape),
                  _resident((1, HEAD_DIM)), _resident((1, HEAD_DIM))],
        out_specs=[pl.BlockSpec((D_MODEL, tm), lambda i: (0, i)),
                   pl.BlockSpec((N_PAIRS, tm, LANES), lambda i: (0, i, 0)),
                   pl.BlockSpec((D_MODEL, tm), lambda i: (0, i)), tok(LANES)],
        out_shape=[jax.ShapeDtypeStruct((D_MODEL, t), BF16),
                   jax.ShapeDtypeStruct((N_PAIRS, t, LANES), BF16),
                   jax.ShapeDtypeStruct((D_MODEL, t), BF16),
                   jax.ShapeDtypeStruct((t, LANES), F32)],
        compiler_params=pltpu.CompilerParams(
            dimension_semantics=("parallel",), vmem_limit_bytes=VMEM_LIMIT),
        name="fox_pre",
    )(x2, g, w_in, gq2, gk2)


def _fox_decay_kernel(f_ref, b_ref, c_ref, *, blk):
    seq = f_ref.shape[0]
    lane = lax.broadcasted_iota(jnp.int32, (1, LANES), 1)
    r = lax.broadcasted_iota(jnp.int32, (blk, blk), 0)
    c = lax.broadcasted_iota(jnp.int32, (blk, blk), 1)
    lower = jnp.where(r >= c, 1.0, 0.0).astype(BF16)
    bias = jnp.concatenate(
        [b_ref[...]] * N_SPLIT + [jnp.zeros((1, LANES - N_SPLIT * N_HEADS), F32)], axis=1)
    carry = jnp.zeros((1, LANES), F32)
    for b in range(seq // blk):
        rows = slice(b * blk, (b + 1) * blk)
        x = f_ref[rows, :] + bias
        log_f = -(jnp.maximum(-x, 0.0) + jnp.log1p(jnp.exp(-jnp.abs(x))))
        cb = carry
        for term in _split_bf16(log_f):
            cb = cb + jnp.dot(lower, term, preferred_element_type=F32)
        carry = cb[blk - 1:blk, :]
        out = jnp.zeros((blk, LANES), BF16)
        for t, term in reversed(list(enumerate(_split_bf16(cb * LOG2_E)))):
            out = jnp.where(lane < (t + 1) * N_HEADS, term, out)
        c_ref[rows, :] = out


def _fox_decay(f3, b3, bsz, seq):
    spec = pl.BlockSpec((seq, LANES), lambda b: (b, 0))
    return pl.pallas_call(
        functools.partial(_fox_decay_kernel, blk=MXU_N),
        grid=(bsz,),
        in_specs=[spec, _resident((1, N_HEADS))],
        out_specs=spec,
        out_shape=jax.ShapeDtypeStruct(f3.shape, BF16),
        compiler_params=pltpu.CompilerParams(dimension_semantics=("parallel",)),
        name="fox_decay",
    )(f3, b3)


def _fox_attn_kernel(qt_ref, k_ref, vt_ref, c_ref, *rest, tq, tk, seq, n_cast,
                     s_ahead, e_ahead):
    w_refs, wkv_ref, o_ref = rest[:n_cast], rest[n_cast], rest[n_cast + 1]
    wb_refs, wkvb_ref = rest[n_cast + 2:2 * n_cast + 2], rest[2 * n_cast + 2]
    st_sc, pt_sc = rest[2 * n_cast + 3:]
    for w_ref, wb_ref in zip(w_refs, wb_refs):
        wb_ref[...] = w_ref[...].astype(BF16)
    wkv = wkv_ref[...]
    k_heads = [wkv[:, j * HEAD_DIM:(j + 1) * HEAD_DIM] for j in range(KV_HEADS)]
    wkvb_ref[...] = jnp.concatenate(
        [kh for kh in k_heads for _ in range(2)] + [wkv[:, KV_HEADS * HEAD_DIM:]],
        axis=1).astype(BF16)
    p = pl.program_id(1)
    feat = lax.broadcasted_iota(jnp.int32, (LANES, 1), 0)
    own = [feat < HEAD_DIM, feat >= HEAD_DIM]
    neg = [jnp.broadcast_to(
        jnp.where((feat < N_SPLIT * N_HEADS)
                  & ((feat & (N_HEADS - 1)) == 2 * p + hh), -1.0, 0.0).astype(BF16),
        (LANES, tq)) for hh in range(2)]

    items = []
    for i in range(seq // tq):
        n_keys = (i + 1) * tq
        starts = list(range(0, n_keys, tk))
        for ks in starts:
            items.append((i, ks, min(tk, n_keys - ks), ks == starts[-1]))

    def query_blocks(i):
        q2 = qt_ref[:, i * tq:(i + 1) * tq]
        return [jnp.concatenate(
            [jnp.where(own[hh], q2, jnp.zeros_like(q2)), neg[hh]], axis=0)
            for hh in range(2)]

    def scores(item, qa, slot, col_max):
        i, ks, width, is_last = item
        ka = jnp.concatenate([k_ref[ks:ks + width, :], c_ref[ks:ks + width, :]],
                             axis=1)
        if is_last:
            krow = lax.broadcasted_iota(jnp.int32, (width, tq), 0)
            qcol = lax.broadcasted_iota(jnp.int32, (width, tq), 1)
            visible = krow <= qcol + (i * tq - ks)

        def one(hh):
            st = jnp.dot(ka, qa[hh], preferred_element_type=F32)
            if is_last:
                st = jnp.where(visible, st, NEG_INF)
            st_sc[slot, hh, 0:width, :] = st
            col_max[hh] = _reduce_rows(st, jnp.maximum)
        return [functools.partial(one, hh) for hh in range(2)]

    def probabilities(item, t, col_max, m_old, res):
        width = item[2]
        chunks = list(range(0, width, ROW_CHUNK))
        pieces = []
        for hh in range(2):
            src = st_sc.at[t % n_st, hh]

            def new_max(hh=hh):
                m_new = jnp.maximum(m_old[hh], col_max[hh])
                res[hh] = (m_new, jnp.exp2(m_old[hh] - m_new))

            def exps(rows, hh=hh, src=src):
                for r in rows:
                    pt_sc[t % n_pt, hh, r:r + ROW_CHUNK, :] = jnp.exp2(
                        src[r:r + ROW_CHUNK, :] - res[hh][0]).astype(BF16)

            half = len(chunks) // 2
            pieces += [new_max, functools.partial(exps, chunks[:half]),
                       functools.partial(exps, chunks[half:])]
        return pieces

    def values(item, t, alpha, acc_old, new):
        _, ks, width, _ = item

        def one(hh):
            vt = jnp.concatenate(
                [vt_ref[hh * HEAD_DIM:(hh + 1) * HEAD_DIM, ks:ks + width],
                 jnp.ones((SUM_ROWS, width), BF16)], axis=0)
            new[hh] = alpha[hh] * acc_old[hh] + jnp.dot(
                vt, pt_sc[t % n_pt, hh, 0:width, :],
                preferred_element_type=F32)
        return [functools.partial(one, hh) for hh in range(2)]

    n_items = len(items)
    n_st, n_pt = st_sc.shape[0], pt_sc.shape[0]
    assert n_st > s_ahead - e_ahead and n_pt > e_ahead
    m_fresh = [jnp.full((1, tq), NEG_INF, F32)] * 2
    acc_fresh = [jnp.zeros((HEAD_DIM + SUM_ROWS, tq), F32)] * 2
    alpha, col_max = {}, {}
    qa_tile, qa = -1, None
    m_run, acc_run = m_fresh, acc_fresh
    for t in range(-s_ahead, n_items):
        matmuls, vector = [], []
        u, e = t + s_ahead, t + e_ahead
        if u < n_items:
            if items[u][0] != qa_tile:
                qa_tile, qa = items[u][0], query_blocks(items[u][0])
            col_max[u] = [None, None]
            matmuls += scores(items[u], qa, u % n_st, col_max[u])
        acc_new = [None, None]
        if t >= 0:
            matmuls += values(items[t], t, alpha.pop(t), acc_run, acc_new)
        res = [None, None]
        if 0 <= e < n_items:
            vector = probabilities(items[e], e, col_max.pop(e), m_run, res)
        per = -(-len(vector) // max(len(matmuls), 1))
        for n, mm in enumerate(matmuls):
            mm()
            for piece in vector[n * per:(n + 1) * per]:
                piece()
        for piece in vector[len(matmuls) * per:]:
            piece()
        if t >= 0:
            acc_run = acc_new
            if items[t][3]:
                i = items[t][0]
                ot = jnp.concatenate(
                    [acc[:HEAD_DIM] / acc[HEAD_DIM:HEAD_DIM + 1] for acc in acc_run],
                    axis=0)
                o_ref[i * tq:(i + 1) * tq, :] = ot.T.astype(BF16)
                acc_run = acc_fresh
        if 0 <= e < n_items:
            m_run = m_fresh if items[e][3] else [r[0] for r in res]
            alpha[e] = [r[1] for r in res]


def _fox_attn(qt, k, vt, caug, weights, w_kv, bsz, seq, tq, tk,
              s_ahead=FOX_SCORES_AHEAD, e_ahead=FOX_PROBS_AHEAD):
    tok = pl.BlockSpec((None, seq, LANES), lambda b, p: (p, b, 0))
    feat = pl.BlockSpec((LANES, seq), lambda b, p: (p, b))
    steps = bsz * N_PAIRS
    flat = [w.reshape(-1, w.shape[-1]) for w in weights]
    slab = lambda rows, cols: pl.BlockSpec((rows // steps, cols),
                                           lambda b, p: (b * N_PAIRS + p, 0))
    slabs = [slab(*w.shape) for w in flat]
    kv_rows, kv_cols = w_kv.shape
    kv_out_cols = kv_cols + KV_HEADS * HEAD_DIM
    outs = pl.pallas_call(
        functools.partial(_fox_attn_kernel, tq=tq, tk=tk, seq=seq,
                          n_cast=len(flat), s_ahead=s_ahead, e_ahead=e_ahead),
        grid=(bsz, N_PAIRS),
        in_specs=[feat, tok, feat,
                  pl.BlockSpec((seq, LANES), lambda b, p: (b, 0))]
        + slabs + [slab(kv_rows, kv_cols)],
        out_specs=[tok] + slabs + [slab(kv_rows, kv_out_cols)],
        out_shape=[jax.ShapeDtypeStruct(k.shape, BF16)]
        + [jax.ShapeDtypeStruct(w.shape, BF16) for w in flat]
        + [jax.ShapeDtypeStruct((kv_rows, kv_out_cols), BF16)],
        scratch_shapes=[pltpu.VMEM((s_ahead - e_ahead + 2, 2, tk, tq), F32),
                        pltpu.VMEM((e_ahead + 1, 2, tk, tq), BF16)],
        compiler_params=pltpu.CompilerParams(
            dimension_semantics=("parallel", "parallel")),
        name="fox_attn",
    )(qt, k, vt, caug, *flat, w_kv)
    return (outs[0], [wb.reshape(w.shape) for wb, w in zip(outs[1:-1], weights)],
            outs[-1])


def _post_mlp_kernel(*refs, layer, emit_next, o_by_pair, ff_chunk):
    if emit_next:
        (h_ref, o_ref, wo_ref, gm_ref, wup_ref, wdn_ref,
         ga_ref, gkv_ref, wq_ref, wkv_ref, gq_ref, gk_ref,
         out_ref, qb_ref, kd_ref, vt_ref) = refs
    else:
        h_ref, o_ref, wo_ref, gm_ref, wup_ref, wdn_ref, out_ref = refs
    if o_by_pair:
        o = jnp.concatenate([o_ref[p] for p in range(N_PAIRS)], axis=1)
    else:
        o = o_ref[...]
    h1 = h_ref[...] + jnp.dot(o, wo_ref[...], preferred_element_type=F32)
    m = ((h1 * _rms_scale(h1)) * gm_ref[layer:layer + 1, :]).astype(BF16)
    acc = h1
    for c in range(D_FF // ff_chunk):
        u = jnp.dot(m, wup_ref[:, c * ff_chunk:(c + 1) * ff_chunk],
                    preferred_element_type=F32)
        u = jnp.square(jnp.maximum(u, 0.0)).astype(BF16)
        acc = acc + jnp.dot(u, wdn_ref[c * ff_chunk:(c + 1) * ff_chunk, :],
                            preferred_element_type=F32)
    out_ref[...] = acc
    if emit_next:
        hn = acc * _rms_scale(acc)
        a = (hn * ga_ref[layer + 1:layer + 2, :]).astype(BF16)
        for c in range(D_MODEL // MXU_N):
            y = jnp.dot(a, wq_ref[:, c * MXU_N:(c + 1) * MXU_N],
                        preferred_element_type=F32)
            for half in range(MXU_N // LANES):
                yy = y[:, half * LANES:(half + 1) * LANES]
                dst = pl.ds(c * MXU_N + half * LANES, LANES)
                qb_ref[:, dst] = (_head_pair_norm(yy, gq_ref[...])
                                  * (QK_SCALE * LOG2_E)).astype(BF16)
        kvn = (hn * gkv_ref[...]).astype(BF16)
        kv = jnp.dot(kvn, wkv_ref[...], preferred_element_type=F32)
        for j in range(KV_HEADS):
            sl = slice(j * LANES, (j + 1) * LANES)
            kd_ref[:, sl] = _head_pair_norm(kv[:, sl], gk_ref[...]).astype(BF16)
        vt_ref[...] = kv[:, KV_HEADS * LANES:].T.astype(BF16)


def _post_mlp(h, o, wo, gm, wup, wdn, layer, nxt, tm, ff_chunk=MLP_FF_CHUNK):
    t = h.shape[0]
    tok = lambda w: pl.BlockSpec((tm, w), lambda i: (i, 0))
    o_by_pair = o.ndim == 3
    o_spec = (pl.BlockSpec((N_PAIRS, tm, LANES), lambda i: (0, i, 0))
              if o_by_pair else tok(D_MODEL))
    in_specs = [tok(D_MODEL), o_spec, _layer((D_MODEL, D_MODEL), 0),
                _resident(gm.shape), _layer((D_MODEL, D_FF), layer),
                _layer((D_FF, D_MODEL), layer)]
    out_specs = [tok(D_MODEL)]
    out_shape = [jax.ShapeDtypeStruct((t, D_MODEL), F32)]
    args = [h, o, wo, gm, wup, wdn]
    if nxt is not None:
        kvw = KV_HEADS * LANES + KV_HEADS * HEAD_DIM
        in_specs += [_resident(nxt[0].shape), _resident((1, D_MODEL)),
                     _layer((D_MODEL, D_MODEL), 0), _resident((D_MODEL, kvw)),
                     _resident((1, HEAD_DIM)), _resident((1, HEAD_DIM))]
        out_specs += [tok(D_MODEL), tok(KV_HEADS * LANES),
                      pl.BlockSpec((KV_HEADS * HEAD_DIM, tm), lambda i: (0, i))]
        out_shape += [jax.ShapeDtypeStruct((t, D_MODEL), BF16),
                      jax.ShapeDtypeStruct((t, KV_HEADS * LANES), BF16),
                      jax.ShapeDtypeStruct((KV_HEADS * HEAD_DIM, t), BF16)]
        args += list(nxt)
    return pl.pallas_call(
        functools.partial(_post_mlp_kernel, layer=layer, emit_next=nxt is not None,
                          o_by_pair=o_by_pair, ff_chunk=ff_chunk),
        grid=(t // tm,),
        in_specs=in_specs, out_specs=out_specs, out_shape=out_shape,
        compiler_params=pltpu.CompilerParams(
            dimension_semantics=("parallel",), vmem_limit_bytes=VMEM_LIMIT),
        name="post_mlp_next" if nxt is not None else "post_mlp",
    )(*args)


def _t5_causal_bucket(dist):
    n = np.maximum(dist, 0)
    max_exact = N_BUCKETS // 2
    large = max_exact + (np.log(np.maximum(n, 1) / max_exact)
                         / np.log(REL_MAX_DIST / max_exact)
                         * (N_BUCKETS - max_exact)).astype(np.int32)
    large = np.minimum(large, N_BUCKETS - 1)
    return np.where(n < max_exact, n, large).astype(np.int32)


def _bucket_ranges():
    buckets = _t5_causal_bucket(np.arange(WINDOW))
    assert np.all(np.diff(buckets) >= 0)
    out = []
    for k in np.unique(buckets):
        idx = np.nonzero(buckets == k)[0]
        out.append((int(k), int(idx[0]), int(idx[-1]) + 1))
    return out


def _swa_attn_kernel(sink_ref, rb_ref, q_ref, kp_ref, kc_ref, vtp_ref, vtc_ref,
                     o_ref, bias_sc, st_sc, e_sc, *, n_blocks):
    n = pl.program_id(1)
    band = 2 * WINDOW

    @pl.when((pl.program_id(0) == 0) & (n == 0))
    def _():
        krow = lax.broadcasted_iota(jnp.int32, (band, WINDOW), 0)
        qcol = lax.broadcasted_iota(jnp.int32, (band, WINDOW), 1)
        dist = qcol + WINDOW - krow
        for h in range(N_HEADS):
            tile = jnp.full((band, WINDOW), NEG_INF, F32)
            for k, lo, hi in _bucket_ranges():
                tile = jnp.where((dist >= lo) & (dist < hi),
                                 rb_ref[h, k] * LOG2_E, tile)
            cols = slice((h % 2) * WINDOW, (h % 2 + 1) * WINDOW)
            bias_sc[0, h // 2, :, cols] = tile
            bias_sc[1, h // 2, :, cols] = jnp.where(krow < WINDOW, NEG_INF, tile)

    first = (n == 0).astype(jnp.int32)
    keys = jnp.concatenate([kp_ref[...], kc_ref[...]], axis=0)
    vts = jnp.concatenate([vtp_ref[...], vtc_ref[...]], axis=1)
    lane = lax.broadcasted_iota(jnp.int32, (1, LANES), 1)
    col = lax.broadcasted_iota(jnp.int32, (1, band), 1)
    pairs_per_kv = N_PAIRS // KV_HEADS
    items = [(j, p) for j in range(n_blocks) for p in range(N_PAIRS)]

    def scores(item, slot):
        j, p = item
        g = p // pairs_per_kv
        q2 = q_ref[j * WINDOW:(j + 1) * WINDOW, p * LANES:(p + 1) * LANES]
        zero = jnp.zeros_like(q2)
        qs = jnp.concatenate([jnp.where(lane < HEAD_DIM, q2, zero),
                              jnp.where(lane >= HEAD_DIM, q2, zero)], axis=0)
        st_sc[slot] = lax.dot_general(
            keys[j * WINDOW:j * WINDOW + band, g * LANES:(g + 1) * LANES], qs,
            (((1,), (1,)), ((), ())), preferred_element_type=F32)

    def softmax(item, st_slot, slot):
        j, p = item
        st = st_sc[st_slot] + bias_sc[first if j == 0 else 0, p]
        sink = jnp.where(col < WINDOW, sink_ref[2 * p], sink_ref[2 * p + 1]) * LOG2_E
        m = jnp.maximum(_reduce_rows(st, jnp.maximum), sink)
        e_sc[slot] = jnp.exp2(st - m).astype(BF16)
        return jnp.exp2(sink - m)

    def values(item, slot, sink_term):
        j, p = item
        g = p // pairs_per_kv
        vt1 = jnp.concatenate(
            [vts[g * HEAD_DIM:(g + 1) * HEAD_DIM, j * WINDOW:j * WINDOW + band],
             jnp.ones((SUM_ROWS, band), BF16)], axis=0)
        acc = jnp.dot(vt1, e_sc[slot], preferred_element_type=F32)
        ot = acc[:HEAD_DIM] / (acc[HEAD_DIM:HEAD_DIM + 1] + sink_term)
        o2 = jnp.concatenate([ot[:, :WINDOW], ot[:, WINDOW:]], axis=0).T
        o_ref[j * WINDOW:(j + 1) * WINDOW, p * LANES:(p + 1) * LANES] = o2.astype(BF16)

    for t in range(4):
        scores(items[t], t)
    sink_term = softmax(items[0], 0, 0)
    for t, item in enumerate(items):
        if t + 4 < len(items):
            scores(items[t + 4], t % 4)
        values(item, t % 2, sink_term)
        if t + 1 < len(items):
            sink_term = softmax(items[t + 1], (t + 1) % 4, (t + 1) % 2)


def _swa_attn(sinks, rel_bias, q, kd, vt, bsz, seq, n_blocks):
    nblk = seq // WINDOW
    steps = nblk // n_blocks
    kvw = KV_HEADS * LANES
    vtw = KV_HEADS * HEAD_DIM
    tq = n_blocks * WINDOW
    prev_block = lambda b, n: b * nblk + jnp.maximum(n * n_blocks - 1, 0)
    smem = pl.BlockSpec(memory_space=pltpu.SMEM)
    return pl.pallas_call(
        functools.partial(_swa_attn_kernel, n_blocks=n_blocks),
        grid=(bsz, steps),
        in_specs=[
            smem, smem,
            pl.BlockSpec((tq, D_MODEL), lambda b, n: (b * steps + n, 0)),
            pl.BlockSpec((WINDOW, kvw), lambda b, n: (prev_block(b, n), 0)),
            pl.BlockSpec((tq, kvw), lambda b, n: (b * steps + n, 0)),
            pl.BlockSpec((vtw, WINDOW), lambda b, n: (0, prev_block(b, n))),
            pl.BlockSpec((vtw, tq), lambda b, n: (0, b * steps + n)),
        ],
        out_specs=pl.BlockSpec((tq, D_MODEL), lambda b, n: (b * steps + n, 0)),
        out_shape=jax.ShapeDtypeStruct((bsz * seq, D_MODEL), BF16),
        scratch_shapes=[pltpu.VMEM((2, N_PAIRS, 2 * WINDOW, 2 * WINDOW), F32),
                        pltpu.VMEM((4, 2 * WINDOW, 2 * WINDOW), F32),
                        pltpu.VMEM((2, 2 * WINDOW, 2 * WINDOW), BF16)],
        compiler_params=pltpu.CompilerParams(
            dimension_semantics=("arbitrary", "arbitrary")),
        name="swa_attn",
    )(sinks, rel_bias, q, kd, kd, vt, vt)


def kernel(x, g_attn, g_mlp, w_in_a, b_f, gq_a, gk_a, w_out_a, g_kv, w_kv, gk_b,
           w_q_b, gq_b, sinks, rel_bias, w_out_b, w_up, w_down):
    bsz, seq, d = x.shape
    t = bsz * seq
    row = lambda g: g.reshape(1, -1).astype(F32)

    x2 = x.reshape(t, d)
    qt, k, vt, f3 = _fox_pre(x2, g_attn.astype(F32), w_in_a[0].T.astype(F32),
                            row(gq_a[0]), row(gk_a[0]), tm=PRE_TOKENS)
    caug = _fox_decay(f3, row(b_f[0]), bsz, seq)
    o, (w_up_b, w_down_b, w_out_a_b, w_out_b_b, w_q_b_b), wkv = _fox_attn(
        qt, k, vt, caug, [w_up, w_down, w_out_a, w_out_b, w_q_b], w_kv.astype(F32),
        bsz, seq, tq=FOX_TQ, tk=FOX_TK)
    nxt = (g_attn.astype(F32), row(g_kv), w_q_b_b, wkv, row(gq_b[0]), row(gk_b))
    h, qb, kd, vt_b = _post_mlp(x2, o, w_out_a_b, g_mlp.astype(F32),
                                w_up_b, w_down_b, 0, nxt, tm=MLP_TOKENS)

    o2 = _swa_attn(sinks[0].astype(F32), rel_bias.T.astype(F32), qb, kd, vt_b,
                   bsz, seq, n_blocks=SWA_BLOCKS)
    (out,) = _post_mlp(h, o2, w_out_b_b, g_mlp.astype(F32),
                       w_up_b, w_down_b, 1, None, tm=MLP_TOKENS_LAST)
    return out.reshape(bsz, seq, d)
```

```python
import functools

import numpy as np
import jax
import jax.numpy as jnp
from jax import lax
from jax.experimental import pallas as pl
from jax.experimental.pallas import tpu as pltpu

D_MODEL = 1024
HEAD_DIM = 64
N_HEADS = 16
N_PAIRS = N_HEADS // 2
KV_HEADS = 2
WINDOW = 128
D_FF = 4 * D_MODEL
N_BUCKETS = 32
REL_MAX_DIST = 128
NORM_EPS = 1e-6
LANES = 128
MXU_N = 256
QK_SCALE = HEAD_DIM ** -0.5
LOG2_E = 1.4426950408889634
N_SPLIT = 3
SUM_ROWS = 16
ROW_CHUNK = 32

F32 = jnp.float32
BF16 = jnp.bfloat16
NEG_INF = float("-inf")

PRE_TOKENS = 1024
MLP_TOKENS = 512
MLP_TOKENS_LAST = 1024
MLP_FF_CHUNK = 1024
FOX_TQ, FOX_TK = 256, 512
FOX_SCORES_AHEAD = 2
FOX_PROBS_AHEAD = 1
SWA_BLOCKS = 8
VMEM_LIMIT = 56 * 1024 * 1024


def _rms_scale(x):
    return lax.rsqrt(jnp.mean(x * x, axis=-1, keepdims=True) + NORM_EPS)


def _head_pair_norm(y, g):
    g2 = jnp.concatenate([g, g], axis=1)
    lo = lax.broadcasted_iota(jnp.int32, (1, LANES), 1) < HEAD_DIM
    sq = y * y
    s_lo = jnp.sum(jnp.where(lo, sq, 0.0), axis=-1, keepdims=True)
    s_hi = jnp.sum(jnp.where(lo, 0.0, sq), axis=-1, keepdims=True)
    ms = jnp.where(lo, s_lo, s_hi) * (1.0 / HEAD_DIM)
    return (y * lax.rsqrt(ms + NORM_EPS)) * g2


def _split_bf16(x):
    terms = []
    for _ in range(N_SPLIT):
        t = x.astype(BF16)
        terms.append(t)
        x = x - t.astype(F32)
    return terms


def _reduce_rows(x, op, group=32):
    acc = x[0:group]
    for r in range(group, x.shape[0], group):
        acc = op(acc, x[r:r + group])
    red = jnp.max if op is jnp.maximum else jnp.sum
    return red(acc, axis=0, keepdims=True)


def _resident(shape):
    zeros = (0,) * len(shape)
    return pl.BlockSpec(shape, lambda *_: zeros, pipeline_mode=pl.Buffered(1))


def _layer(shape, layer):
    index = (layer,) + (0,) * len(shape)
    return pl.BlockSpec((None,) + tuple(shape), lambda *_: index,
                        pipeline_mode=pl.Buffered(1))


def _fox_pre_kernel(x_ref, g_ref, w_ref, gq_ref, gk_ref,
                    qt_ref, k_ref, vt_ref, f_ref):
    x = x_ref[...]
    hn = ((x * _rms_scale(x)) * g_ref[0:1, :]).astype(BF16)
    n_chunks = D_MODEL // MXU_N
    for c in range(3 * n_chunks):
        y = lax.dot_general(
            hn, w_ref[c * MXU_N:(c + 1) * MXU_N, :].astype(BF16),
            (((1,), (1,)), ((), ())), preferred_element_type=F32)
        part, cc = divmod(c, n_chunks)
        rows = slice(cc * MXU_N, (cc + 1) * MXU_N)
        if part == 2:
            vt_ref[rows, :] = y.T.astype(BF16)
            continue
        g2 = gq_ref[...] if part == 0 else gk_ref[...]
        yn = jnp.concatenate(
            [_head_pair_norm(y[:, h * LANES:(h + 1) * LANES], g2)
             for h in range(MXU_N // LANES)], axis=1)
        if part == 0:
            qt_ref[rows, :] = (yn * (QK_SCALE * LOG2_E)).T.astype(BF16)
        else:
            for h in range(MXU_N // LANES):
                k_ref[cc * (MXU_N // LANES) + h] = yn[:, h * LANES:(h + 1) * LANES].astype(BF16)
    w_f = w_ref[3 * D_MODEL:3 * D_MODEL + N_HEADS, :]
    w_f3 = jnp.concatenate(
        [w_f] * N_SPLIT + [jnp.zeros((LANES - N_SPLIT * N_HEADS, D_MODEL), F32)],
        axis=0).astype(BF16)
    f_ref[...] = lax.dot_general(hn, w_f3, (((1,), (1,)), ((), ())),
                                 preferred_element_type=F32)


def _fox_pre(x2, g, w_in, gq2, gk2, tm):
    t = x2.shape[0]
    tok = lambda w: pl.BlockSpec((tm, w), lambda i: (i, 0))
    return pl.pallas_call(
        _fox_pre_kernel,
        grid=(t // tm,),
        in_specs=[tok(D_MODEL), _resident(g.shape),
                  _resident(w_in.shape),
                  _resident((1, HEAD_DIM)), _resident((1, HEAD_DIM))],
        out_specs=[pl.BlockSpec((D_MODEL, tm), lambda i: (0, i)),
                   pl.BlockSpec((N_PAIRS, tm, LANES), lambda i: (0, i, 0)),
                   pl.BlockSpec((D_MODEL, tm), lambda i: (0, i)), tok(LANES)],
        out_shape=[jax.ShapeDtypeStruct((D_MODEL, t), BF16),
                   jax.ShapeDtypeStruct((N_PAIRS, t, LANES), BF16),
                   jax.ShapeDtypeStruct((D_MODEL, t), BF16),
                   jax.ShapeDtypeStruct((t, LANES), F32)],
        compiler_params=pltpu.CompilerParams(
            dimension_semantics=("parallel",), vmem_limit_bytes=VMEM_LIMIT),
        name="fox_pre",
    )(x2, g, w_in, gq2, gk2)


def _decay_terms(f_ref, b_ref, c_ref, blk=MXU_N):
    seq = f_ref.shape[0]
    lane = lax.broadcasted_iota(jnp.int32, (1, LANES), 1)
    r = lax.broadcasted_iota(jnp.int32, (blk, blk), 0)
    c = lax.broadcasted_iota(jnp.int32, (blk, blk), 1)
    lower = jnp.where(r >= c, 1.0, 0.0).astype(BF16)
    bias = jnp.concatenate(
        [b_ref[...]] * N_SPLIT + [jnp.zeros((1, LANES - N_SPLIT * N_HEADS), F32)], axis=1)
    carry = jnp.zeros((1, LANES), F32)
    for b in range(seq // blk):
        rows = slice(b * blk, (b + 1) * blk)
        x = f_ref[rows, :] + bias
        log_f = -(jnp.maximum(-x, 0.0) + jnp.log1p(jnp.exp(-jnp.abs(x))))
        cb = carry
        for term in _split_bf16(log_f):
            cb = cb + jnp.dot(lower, term, preferred_element_type=F32)
        carry = cb[blk - 1:blk, :]
        out = jnp.zeros((blk, LANES), BF16)
        for t, term in reversed(list(enumerate(_split_bf16(cb * LOG2_E)))):
            out = jnp.where(lane < (t + 1) * N_HEADS, term, out)
        c_ref[rows, :] = out


def _fox_attn_kernel(qt_ref, k_ref, vt_ref, f_ref, bf_ref, *rest, tq, tk, seq, n_cast,
                     s_ahead, e_ahead):
    w_refs, wkv_ref, o_ref = rest[:n_cast], rest[n_cast], rest[n_cast + 1]
    wb_refs, wkvb_ref = rest[n_cast + 2:2 * n_cast + 2], rest[2 * n_cast + 2]
    st_sc, pt_sc, c_ref = rest[2 * n_cast + 3:]

    @pl.when(pl.program_id(1) == 0)
    def _():
        _decay_terms(f_ref, bf_ref, c_ref)

    for w_ref, wb_ref in zip(w_refs, wb_refs):
        wb_ref[...] = w_ref[...].astype(BF16)
    wkv = wkv_ref[...]
    k_heads = [wkv[:, j * HEAD_DIM:(j + 1) * HEAD_DIM] for j in range(KV_HEADS)]
    wkvb_ref[...] = jnp.concatenate(
        [kh for kh in k_heads for _ in range(2)] + [wkv[:, KV_HEADS * HEAD_DIM:]],
        axis=1).astype(BF16)
    p = pl.program_id(1)
    feat = lax.broadcasted_iota(jnp.int32, (LANES, 1), 0)
    own = [feat < HEAD_DIM, feat >= HEAD_DIM]
    neg = [jnp.broadcast_to(
        jnp.where((feat < N_SPLIT * N_HEADS)
                  & ((feat & (N_HEADS - 1)) == 2 * p + hh), -1.0, 0.0).astype(BF16),
        (LANES, tq)) for hh in range(2)]

    items = []
    for i in range(seq // tq):
        n_keys = (i + 1) * tq
        starts = list(range(0, n_keys, tk))
        for ks in starts:
            items.append((i, ks, min(tk, n_keys - ks), ks == starts[-1]))

    def query_blocks(i):
        q2 = qt_ref[:, i * tq:(i + 1) * tq]
        return [jnp.concatenate(
            [jnp.where(own[hh], q2, jnp.zeros_like(q2)), neg[hh]], axis=0)
            for hh in range(2)]

    def scores(item, qa, slot, col_max):
        i, ks, width, is_last = item
        ka = jnp.concatenate([k_ref[ks:ks + width, :], c_ref[ks:ks + width, :]],
                             axis=1)
        if is_last:
            krow = lax.broadcasted_iota(jnp.int32, (width, tq), 0)
            qcol = lax.broadcasted_iota(jnp.int32, (width, tq), 1)
            visible = krow <= qcol + (i * tq - ks)

        def one(hh):
            st = jnp.dot(ka, qa[hh], preferred_element_type=F32)
            if is_last:
                st = jnp.where(visible, st, NEG_INF)
            st_sc[slot, hh, 0:width, :] = st
            col_max[hh] = _reduce_rows(st, jnp.maximum)
        return [functools.partial(one, hh) for hh in range(2)]

    def probabilities(item, t, col_max, m_old, res):
        width = item[2]
        chunks = list(range(0, width, ROW_CHUNK))
        pieces = []
        for hh in range(2):
            src = st_sc.at[t % n_st, hh]

            def new_max(hh=hh):
                m_new = jnp.maximum(m_old[hh], col_max[hh])
                res[hh] = (m_new, jnp.exp2(m_old[hh] - m_new))

            def exps(rows, hh=hh, src=src):
                for r in rows:
                    pt_sc[t % n_pt, hh, r:r + ROW_CHUNK, :] = jnp.exp2(
                        src[r:r + ROW_CHUNK, :] - res[hh][0]).astype(BF16)

            half = len(chunks) // 2
            pieces += [new_max, functools.partial(exps, chunks[:half]),
                       functools.partial(exps, chunks[half:])]
        return pieces

    def values(item, t, alpha, acc_old, new):
        _, ks, width, _ = item

        def one(hh):
            vt = jnp.concatenate(
                [vt_ref[hh * HEAD_DIM:(hh + 1) * HEAD_DIM, ks:ks + width],
                 jnp.ones((SUM_ROWS, width), BF16)], axis=0)
            new[hh] = alpha[hh] * acc_old[hh] + jnp.dot(
                vt, pt_sc[t % n_pt, hh, 0:width, :],
                preferred_element_type=F32)
        return [functools.partial(one, hh) for hh in range(2)]

    n_items = len(items)
    n_st, n_pt = st_sc.shape[0], pt_sc.shape[0]
    assert n_st > s_ahead - e_ahead and n_pt > e_ahead
    m_fresh = [jnp.full((1, tq), NEG_INF, F32)] * 2
    acc_fresh = [jnp.zeros((HEAD_DIM + SUM_ROWS, tq), F32)] * 2
    alpha, col_max = {}, {}
    qa_tile, qa = -1, None
    m_run, acc_run = m_fresh, acc_fresh
    for t in range(-s_ahead, n_items):
        matmuls, vector = [], []
        u, e = t + s_ahead, t + e_ahead
        if u < n_items:
            if items[u][0] != qa_tile:
                qa_tile, qa = items[u][0], query_blocks(items[u][0])
            col_max[u] = [None, None]
            matmuls += scores(items[u], qa, u % n_st, col_max[u])
        acc_new = [None, None]
        if t >= 0:
            matmuls += values(items[t], t, alpha.pop(t), acc_run, acc_new)
        res = [None, None]
        if 0 <= e < n_items:
            vector = probabilities(items[e], e, col_max.pop(e), m_run, res)
        per = -(-len(vector) // max(len(matmuls), 1))
        for n, mm in enumerate(matmuls):
            mm()
            for piece in vector[n * per:(n + 1) * per]:
                piece()
        for piece in vector[len(matmuls) * per:]:
            piece()
        if t >= 0:
            acc_run = acc_new
            if items[t][3]:
                i = items[t][0]
                ot = jnp.concatenate(
                    [acc[:HEAD_DIM] / acc[HEAD_DIM:HEAD_DIM + 1] for acc in acc_run],
                    axis=0)
                o_ref[i * tq:(i + 1) * tq, :] = ot.T.astype(BF16)
                acc_run = acc_fresh
        if 0 <= e < n_items:
            m_run = m_fresh if items[e][3] else [r[0] for r in res]
            alpha[e] = [r[1] for r in res]


def _fox_attn(qt, k, vt, f3, bf, weights, w_kv, bsz, seq, tq, tk,
              s_ahead=FOX_SCORES_AHEAD, e_ahead=FOX_PROBS_AHEAD):
    tok = pl.BlockSpec((None, seq, LANES), lambda b, p: (p, b, 0))
    feat = pl.BlockSpec((LANES, seq), lambda b, p: (p, b))
    steps = bsz * N_PAIRS
    flat = [w.reshape(-1, w.shape[-1]) for w in weights]
    slab = lambda rows, cols: pl.BlockSpec((rows // steps, cols),
                                           lambda b, p: (b * N_PAIRS + p, 0))
    slabs = [slab(*w.shape) for w in flat]
    kv_rows, kv_cols = w_kv.shape
    kv_out_cols = kv_cols + KV_HEADS * HEAD_DIM
    outs = pl.pallas_call(
        functools.partial(_fox_attn_kernel, tq=tq, tk=tk, seq=seq,
                          n_cast=len(flat), s_ahead=s_ahead, e_ahead=e_ahead),
        grid=(bsz, N_PAIRS),
        in_specs=[feat, tok, feat,
                  pl.BlockSpec((seq, LANES), lambda b, p: (b, 0)),
                  _resident((1, N_HEADS))]
        + slabs + [slab(kv_rows, kv_cols)],
        out_specs=[tok] + slabs + [slab(kv_rows, kv_out_cols)],
        out_shape=[jax.ShapeDtypeStruct(k.shape, BF16)]
        + [jax.ShapeDtypeStruct(w.shape, BF16) for w in flat]
        + [jax.ShapeDtypeStruct((kv_rows, kv_out_cols), BF16)],
        scratch_shapes=[pltpu.VMEM((s_ahead - e_ahead + 2, 2, tk, tq), F32),
                        pltpu.VMEM((e_ahead + 1, 2, tk, tq), BF16),
                        pltpu.VMEM((seq, LANES), BF16)],
        compiler_params=pltpu.CompilerParams(
            dimension_semantics=("parallel", "arbitrary")),
        name="fox_attn",
    )(qt, k, vt, f3, bf, *flat, w_kv)
    return (outs[0], [wb.reshape(w.shape) for wb, w in zip(outs[1:-1], weights)],
            outs[-1])


def _post_mlp_kernel(*refs, layer, emit_next, o_by_pair, ff_chunk):
    if emit_next:
        (h_ref, o_ref, wo_ref, gm_ref, wup_ref, wdn_ref,
         ga_ref, gkv_ref, wq_ref, wkv_ref, gq_ref, gk_ref,
         out_ref, qb_ref, kd_ref, vt_ref) = refs
    else:
        h_ref, o_ref, wo_ref, gm_ref, wup_ref, wdn_ref, out_ref = refs
    if o_by_pair:
        o = jnp.concatenate([o_ref[p] for p in range(N_PAIRS)], axis=1)
    else:
        o = o_ref[...]
    h1 = h_ref[...] + jnp.dot(o, wo_ref[...], preferred_element_type=F32)
    m = ((h1 * _rms_scale(h1)) * gm_ref[layer:layer + 1, :]).astype(BF16)
    acc = h1
    for c in range(D_FF // ff_chunk):
        u = jnp.dot(m, wup_ref[:, c * ff_chunk:(c + 1) * ff_chunk],
                    preferred_element_type=F32)
        u = jnp.square(jnp.maximum(u, 0.0)).astype(BF16)
        acc = acc + jnp.dot(u, wdn_ref[c * ff_chunk:(c + 1) * ff_chunk, :],
                            preferred_element_type=F32)
    out_ref[...] = acc
    if emit_next:
        hn = acc * _rms_scale(acc)
        a = (hn * ga_ref[layer + 1:layer + 2, :]).astype(BF16)
        for c in range(D_MODEL // MXU_N):
            y = jnp.dot(a, wq_ref[:, c * MXU_N:(c + 1) * MXU_N],
                        preferred_element_type=F32)
            for half in range(MXU_N // LANES):
                yy = y[:, half * LANES:(half + 1) * LANES]
                dst = pl.ds(c * MXU_N + half * LANES, LANES)
                qb_ref[:, dst] = (_head_pair_norm(yy, gq_ref[...])
                                  * (QK_SCALE * LOG2_E)).astype(BF16)
        kvn = (hn * gkv_ref[...]).astype(BF16)
        kv = jnp.dot(kvn, wkv_ref[...], preferred_element_type=F32)
        for j in range(KV_HEADS):
            sl = slice(j * LANES, (j + 1) * LANES)
            kd_ref[:, sl] = _head_pair_norm(kv[:, sl], gk_ref[...]).astype(BF16)
        vt_ref[...] = kv[:, KV_HEADS * LANES:].T.astype(BF16)


def _post_mlp(h, o, wo, gm, wup, wdn, layer, nxt, tm, ff_chunk=MLP_FF_CHUNK):
    t = h.shape[0]
    tok = lambda w: pl.BlockSpec((tm, w), lambda i: (i, 0))
    o_by_pair = o.ndim == 3
    o_spec = (pl.BlockSpec((N_PAIRS, tm, LANES), lambda i: (0, i, 0))
              if o_by_pair else tok(D_MODEL))
    in_specs = [tok(D_MODEL), o_spec, _layer((D_MODEL, D_MODEL), 0),
                _resident(gm.shape), _layer((D_MODEL, D_FF), layer),
                _layer((D_FF, D_MODEL), layer)]
    out_specs = [tok(D_MODEL)]
    out_shape = [jax.ShapeDtypeStruct((t, D_MODEL), F32)]
    args = [h, o, wo, gm, wup, wdn]
    if nxt is not None:
        kvw = KV_HEADS * LANES + KV_HEADS * HEAD_DIM
        in_specs += [_resident(nxt[0].shape), _resident((1, D_MODEL)),
                     _layer((D_MODEL, D_MODEL), 0), _resident((D_MODEL, kvw)),
                     _resident((1, HEAD_DIM)), _resident((1, HEAD_DIM))]
        out_specs += [tok(D_MODEL), tok(KV_HEADS * LANES),
                      pl.BlockSpec((KV_HEADS * HEAD_DIM, tm), lambda i: (0, i))]
        out_shape += [jax.ShapeDtypeStruct((t, D_MODEL), BF16),
                      jax.ShapeDtypeStruct((t, KV_HEADS * LANES), BF16),
                      jax.ShapeDtypeStruct((KV_HEADS * HEAD_DIM, t), BF16)]
        args += list(nxt)
    return pl.pallas_call(
        functools.partial(_post_mlp_kernel, layer=layer, emit_next=nxt is not None,
                          o_by_pair=o_by_pair, ff_chunk=ff_chunk),
        grid=(t // tm,),
        in_specs=in_specs, out_specs=out_specs, out_shape=out_shape,
        compiler_params=pltpu.CompilerParams(
            dimension_semantics=("parallel",), vmem_limit_bytes=VMEM_LIMIT),
        name="post_mlp_next" if nxt is not None else "post_mlp",
    )(*args)


def _t5_causal_bucket(dist):
    n = np.maximum(dist, 0)
    max_exact = N_BUCKETS // 2
    large = max_exact + (np.log(np.maximum(n, 1) / max_exact)
                         / np.log(REL_MAX_DIST / max_exact)
                         * (N_BUCKETS - max_exact)).astype(np.int32)
    large = np.minimum(large, N_BUCKETS - 1)
    return np.where(n < max_exact, n, large).astype(np.int32)


def _bucket_ranges():
    buckets = _t5_causal_bucket(np.arange(WINDOW))
    assert np.all(np.diff(buckets) >= 0)
    out = []
    for k in np.unique(buckets):
        idx = np.nonzero(buckets == k)[0]
        out.append((int(k), int(idx[0]), int(idx[-1]) + 1))
    return out


def _swa_attn_kernel(sink_ref, rb_ref, q_ref, kp_ref, kc_ref, vtp_ref, vtc_ref,
                     o_ref, bias_sc, st_sc, e_sc, *, n_blocks):
    n = pl.program_id(1)
    band = 2 * WINDOW

    @pl.when((pl.program_id(0) == 0) & (n == 0))
    def _():
        krow = lax.broadcasted_iota(jnp.int32, (band, WINDOW), 0)
        qcol = lax.broadcasted_iota(jnp.int32, (band, WINDOW), 1)
        dist = qcol + WINDOW - krow
        for h in range(N_HEADS):
            tile = jnp.full((band, WINDOW), NEG_INF, F32)
            for k, lo, hi in _bucket_ranges():
                tile = jnp.where((dist >= lo) & (dist < hi),
                                 rb_ref[h, k] * LOG2_E, tile)
            cols = slice((h % 2) * WINDOW, (h % 2 + 1) * WINDOW)
            bias_sc[0, h // 2, :, cols] = tile
            bias_sc[1, h // 2, :, cols] = jnp.where(krow < WINDOW, NEG_INF, tile)

    first = (n == 0).astype(jnp.int32)
    keys = jnp.concatenate([kp_ref[...], kc_ref[...]], axis=0)
    vts = jnp.concatenate([vtp_ref[...], vtc_ref[...]], axis=1)
    lane = lax.broadcasted_iota(jnp.int32, (1, LANES), 1)
    col = lax.broadcasted_iota(jnp.int32, (1, band), 1)
    pairs_per_kv = N_PAIRS // KV_HEADS
    items = [(j, p) for j in range(n_blocks) for p in range(N_PAIRS)]

    def scores(item, slot):
        j, p = item
        g = p // pairs_per_kv
        q2 = q_ref[j * WINDOW:(j + 1) * WINDOW, p * LANES:(p + 1) * LANES]
        zero = jnp.zeros_like(q2)
        qs = jnp.concatenate([jnp.where(lane < HEAD_DIM, q2, zero),
                              jnp.where(lane >= HEAD_DIM, q2, zero)], axis=0)
        st_sc[slot] = lax.dot_general(
            keys[j * WINDOW:j * WINDOW + band, g * LANES:(g + 1) * LANES], qs,
            (((1,), (1,)), ((), ())), preferred_element_type=F32)

    def softmax(item, st_slot, slot):
        j, p = item
        st = st_sc[st_slot] + bias_sc[first if j == 0 else 0, p]
        sink = jnp.where(col < WINDOW, sink_ref[2 * p], sink_ref[2 * p + 1]) * LOG2_E
        m = jnp.maximum(_reduce_rows(st, jnp.maximum), sink)
        e_sc[slot] = jnp.exp2(st - m).astype(BF16)
        return jnp.exp2(sink - m)

    def values(item, slot, sink_term):
        j, p = item
        g = p // pairs_per_kv
        vt1 = jnp.concatenate(
            [vts[g * HEAD_DIM:(g + 1) * HEAD_DIM, j * WINDOW:j * WINDOW + band],
             jnp.ones((SUM_ROWS, band), BF16)], axis=0)
        acc = jnp.dot(vt1, e_sc[slot], preferred_element_type=F32)
        ot = acc[:HEAD_DIM] / (acc[HEAD_DIM:HEAD_DIM + 1] + sink_term)
        o2 = jnp.concatenate([ot[:, :WINDOW], ot[:, WINDOW:]], axis=0).T
        o_ref[j * WINDOW:(j + 1) * WINDOW, p * LANES:(p + 1) * LANES] = o2.astype(BF16)

    for t in range(4):
        scores(items[t], t)
    sink_term = softmax(items[0], 0, 0)
    for t, item in enumerate(items):
        if t + 4 < len(items):
            scores(items[t + 4], t % 4)
        values(item, t % 2, sink_term)
        if t + 1 < len(items):
            sink_term = softmax(items[t + 1], (t + 1) % 4, (t + 1) % 2)


def _swa_attn(sinks, rel_bias, q, kd, vt, bsz, seq, n_blocks):
    nblk = seq // WINDOW
    steps = nblk // n_blocks
    kvw = KV_HEADS * LANES
    vtw = KV_HEADS * HEAD_DIM
    tq = n_blocks * WINDOW
    prev_block = lambda b, n: b * nblk + jnp.maximum(n * n_blocks - 1, 0)
    smem = pl.BlockSpec(memory_space=pltpu.SMEM)
    return pl.pallas_call(
        functools.partial(_swa_attn_kernel, n_blocks=n_blocks),
        grid=(bsz, steps),
        in_specs=[
            smem, smem,
            pl.BlockSpec((tq, D_MODEL), lambda b, n: (b * steps + n, 0)),
            pl.BlockSpec((WINDOW, kvw), lambda b, n: (prev_block(b, n), 0)),
            pl.BlockSpec((tq, kvw), lambda b, n: (b * steps + n, 0)),
            pl.BlockSpec((vtw, WINDOW), lambda b, n: (0, prev_block(b, n))),
            pl.BlockSpec((vtw, tq), lambda b, n: (0, b * steps + n)),
        ],
        out_specs=pl.BlockSpec((tq, D_MODEL), lambda b, n: (b * steps + n, 0)),
        out_shape=jax.ShapeDtypeStruct((bsz * seq, D_MODEL), BF16),
        scratch_shapes=[pltpu.VMEM((2, N_PAIRS, 2 * WINDOW, 2 * WINDOW), F32),
                        pltpu.VMEM((4, 2 * WINDOW, 2 * WINDOW), F32),
                        pltpu.VMEM((2, 2 * WINDOW, 2 * WINDOW), BF16)],
        compiler_params=pltpu.CompilerParams(
            dimension_semantics=("arbitrary", "arbitrary")),
        name="swa_attn",
    )(sinks, rel_bias, q, kd, kd, vt, vt)


def kernel(x, g_attn, g_mlp, w_in_a, b_f, gq_a, gk_a, w_out_a, g_kv, w_kv, gk_b,
           w_q_b, gq_b, sinks, rel_bias, w_out_b, w_up, w_down):
    bsz, seq, d = x.shape
    t = bsz * seq
    row = lambda g: g.reshape(1, -1).astype(F32)

    x2 = x.reshape(t, d)
    qt, k, vt, f3 = _fox_pre(x2, g_attn.astype(F32), w_in_a[0].T.astype(F32),
                            row(gq_a[0]), row(gk_a[0]), tm=PRE_TOKENS)
    o, (w_up_b, w_down_b, w_out_a_b, w_out_b_b, w_q_b_b), wkv = _fox_attn(
        qt, k, vt, f3, row(b_f[0]), [w_up, w_down, w_out_a, w_out_b, w_q_b], w_kv.astype(F32),
        bsz, seq, tq=FOX_TQ, tk=FOX_TK)
    nxt = (g_attn.astype(F32), row(g_kv), w_q_b_b, wkv, row(gq_b[0]), row(gk_b))
    h, qb, kd, vt_b = _post_mlp(x2, o, w_out_a_b, g_mlp.astype(F32),
                                w_up_b, w_down_b, 0, nxt, tm=MLP_TOKENS)

    o2 = _swa_attn(sinks[0].astype(F32), rel_bias.T.astype(F32), qb, kd, vt_b,
                   bsz, seq, n_blocks=SWA_BLOCKS)
    (out,) = _post_mlp(h, o2, w_out_b_b, g_mlp.astype(F32),
                       w_up_b, w_down_b, 1, None, tm=MLP_TOKENS_LAST)
    return out.reshape(bsz, seq, d)
```
